```python
import math
import jax, jax.numpy as jnp
from jax import lax
import numpy as np

D_MODEL = 1024
BATCH = 1
SEQ = 16384
DEPTH = 4

HD = 64
BLK = 128
MEM_LEN = 256
NEG_INF = -1e30
TINY = 1e-30
EPS = 1e-6

A_HEADS = 8
A_KV = 2
CMP_LEN = 32
CMP_STRIDE = 16
CMP_HIDDEN = 128
SEL_BLK = 64
SEL_TOPK = 16
A_WINDOW = 512
SEL_FORCE = 1e4
B_HEADS = 8
B_HEADDIM = 64
B_GROUPS = 2
B_STATE = 128
B_CONV = 4
B_CHUNK = 128
B_INNER = B_HEADS * B_HEADDIM
B_CONV_DIM = B_INNER + 2 * B_GROUPS * B_STATE
C_HEADS = 8
C_KV = 2
C_WINDOW = 128
ROPE_THETA = 150000.0
D_SLOTS = 8
D_PATTERNS = ((128, 1), (512, 4), (2048, 16))
D_NPAT = 3
M_HEADS = 4
N_BUCKETS = 32
MAX_DIST = 2048
N_BIAS_HEADS = A_HEADS + D_NPAT * D_SLOTS

IN_SPLITS = (
    A_HEADS * HD, 6 * A_KV * HD, 3 * A_HEADS, A_HEADS * HD,
    B_CONV_DIM, B_HEADS, B_INNER,
    C_HEADS * HD, 2 * C_KV * HD, C_HEADS * HD,
    D_NPAT * D_SLOTS * HD, 2 * D_SLOTS * HD, D_SLOTS * HD,
    M_HEADS * HD, M_HEADS * HD,
)
D_IN = 8224
MIX_WIDTH = A_HEADS * HD + B_INNER + C_HEADS * HD + D_SLOTS * HD + M_HEADS * HD

kernel_name = "hymba_parallel_hybrid_nsa_ssd_swa_dilated"


def rms_norm(x, w):
    xf = x.astype(jnp.float32)
    y = xf * lax.rsqrt(jnp.mean(xf * xf, axis=-1, keepdims=True) + EPS)
    return (y * w.astype(jnp.float32)).astype(x.dtype)


def t5_bucket(dist):
    dist = jnp.maximum(dist, 0)
    exact = N_BUCKETS // 2
    rel = jnp.maximum(dist, exact).astype(jnp.float32)
    large = exact + (jnp.log(rel / exact) / math.log(MAX_DIST / exact) * (N_BUCKETS - exact)).astype(jnp.int32)
    return jnp.where(dist < exact, dist, jnp.minimum(large, N_BUCKETS - 1))


def masked_softmax(s, mask):
    s = jnp.where(mask, s, NEG_INF)
    p = jnp.where(mask, jnp.exp(s - jnp.max(s, axis=-1, keepdims=True)), 0.0)
    return p / jnp.maximum(jnp.sum(p, axis=-1, keepdims=True), TINY)


def rope(t, positions):
    half = HD // 2
    inv = ROPE_THETA ** (-jnp.arange(half, dtype=jnp.float32) / half)
    ang = positions.astype(jnp.float32)[..., None] * inv
    cos, sin = jnp.cos(ang)[:, :, None, :], jnp.sin(ang)[:, :, None, :]
    t1, t2 = t[..., :half].astype(jnp.float32), t[..., half:].astype(jnp.float32)
    return jnp.concatenate([t1 * cos - t2 * sin, t2 * cos + t1 * sin], axis=-1).astype(t.dtype)


def banded_attention(q, k, v, max_dist, rel_cols=None, dist_scale=1, sinks=None):
    b, L, hq, hd = q.shape
    hkv = k.shape[2]
    g = hq // hkv
    nb = L // BLK
    n_prev = -(-max_dist // BLK)
    kw = (n_prev + 1) * BLK
    qb = q.reshape(b, nb, BLK, hkv, g, hd)

    def windows(t):
        tp = jnp.pad(t, ((0, 0), (n_prev * BLK, 0), (0, 0), (0, 0))).reshape(b, nb + n_prev, BLK, hkv, hd)
        return jnp.concatenate([tp[:, o:o + nb] for o in range(n_prev + 1)], axis=2)

    kb, vb = windows(k), windows(v)
    s = jnp.einsum('bnqhgd,bnkhd->bnhgqk', qb, kb, preferred_element_type=jnp.float32) * hd ** -0.5
    qi = jnp.arange(BLK)[:, None]
    kj = jnp.arange(kw)[None, :]
    dist = n_prev * BLK + qi - kj
    kpos = (jnp.arange(nb)[:, None, None] - n_prev) * BLK + kj[None]
    mask = (dist >= 0) & (dist <= max_dist) & (kpos >= 0)
    if rel_cols is not None:
        bias = jnp.transpose(rel_cols[t5_bucket(dist * dist_scale)], (2, 0, 1)).reshape(hkv, g, BLK, kw)
        s = s + bias.astype(jnp.float32)
    s = jnp.where(mask[None, :, None, None], s, NEG_INF)
    m = jnp.max(s, axis=-1, keepdims=True)
    if sinks is not None:
        sk = sinks.astype(jnp.float32).reshape(1, 1, hkv, g, 1, 1)
        m = jnp.maximum(m, sk)
    p = jnp.where(mask[None, :, None, None], jnp.exp(s - m), 0.0)
    denom = jnp.sum(p, axis=-1, keepdims=True)
    if sinks is not None:
        denom = denom + jnp.exp(sk - m)
    o = jnp.einsum('bnhgqk,bnkhd->bnqhgd', (p / denom).astype(v.dtype), vb)
    lse = (m + jnp.log(denom))[..., 0]
    return o.reshape(b, L, hq, hd), jnp.transpose(lse, (0, 1, 4, 2, 3)).reshape(b, L, hq)


def compress(t, pe, w1, w2):
    b, s, hkv, hd = t.shape
    r = CMP_LEN // CMP_STRIDE
    ncmp = s // CMP_STRIDE
    u = jnp.pad(t, ((0, 0), (0, (r - 1) * CMP_STRIDE), (0, 0), (0, 0))).reshape(b, ncmp + r - 1, CMP_STRIDE, hkv, hd)
    blocks = jnp.concatenate([u[:, i:i + ncmp] for i in range(r)], axis=2) + pe[:, None, :]
    flat = jnp.transpose(blocks, (0, 1, 3, 2, 4)).reshape(b, ncmp, hkv, CMP_LEN * hd)
    return jax.nn.gelu(flat @ w1) @ w2


def nsa_mixer(q, k_c, v_c, k_s, v_s, k_w, v_w, gate_logits, pe, w1, w2, rel_a):
    b, s, hq, hd = q.shape
    g = hq // A_KV
    nb = s // BLK
    nsel = s // SEL_BLK
    n_top = min(SEL_TOPK, nsel)
    r = CMP_LEN // CMP_STRIDE
    scale = hd ** -0.5
    kc = compress(k_c, pe[0], w1[0], w2[0])
    vc = compress(v_c, pe[1], w1[1], w2[1])
    ncmp = kc.shape[1]
    cmp_end = jnp.arange(ncmp) * CMP_STRIDE + (CMP_LEN - 1)
    ksb = jnp.transpose(k_s.reshape(b, nsel, SEL_BLK, A_KV, hd), (0, 3, 1, 2, 4))
    vsb = jnp.transpose(v_s.reshape(b, nsel, SEL_BLK, A_KV, hd), (0, 3, 1, 2, 4))
    kwp = jnp.pad(k_w, ((0, 0), (A_WINDOW, 0), (0, 0), (0, 0)))
    vwp = jnp.pad(v_w, ((0, 0), (A_WINDOW, 0), (0, 0), (0, 0)))
    qg = q.reshape(b, s, A_KV, g, hd)
    gates = jax.nn.sigmoid(gate_logits.astype(jnp.float32)).reshape(b, s, A_KV, g, 3)
    rel3 = rel_a.reshape(N_BUCKETS, A_KV, g)
    gather = jax.vmap(jax.vmap(lambda tb, ix: tb[ix]))
    lookup = jax.vmap(lambda tab, bk: tab[bk], in_axes=(1, 1), out_axes=1)

    def head_bias(dist):
        return jnp.transpose(rel_a[t5_bucket(dist)], (2, 0, 1)).reshape(A_KV, g, *dist.shape).astype(jnp.float32)

    def block(n):
        t0 = n * BLK
        pos = t0 + jnp.arange(BLK)
        qb = lax.dynamic_slice_in_dim(qg, t0, BLK, axis=1)
        dist_c = pos[:, None] - cmp_end[None, :]
        s_c = jnp.einsum('bqhgd,bjhd->bhgqj', qb, kc, preferred_element_type=jnp.float32) * scale + head_bias(dist_c)
        p_c = masked_softmax(s_c, dist_c >= 0)
        o_c = jnp.einsum('bhgqj,bjhd->bqhgd', p_c.astype(vc.dtype), vc)
        pcs = jnp.pad(jnp.sum(p_c, axis=2), ((0, 0), (0, 0), (0, 0), (r - 1, 0)))
        unit = sum(pcs[..., r - 1 - i:r - 1 - i + ncmp] for i in range(r))
        imp = unit.reshape(b, A_KV, BLK, nsel, SEL_BLK // CMP_STRIDE).sum(-1)
        blk_ids = jnp.arange(nsel)[None, :]
        forced = (blk_ids == (pos // SEL_BLK)[:, None]) | (blk_ids == 0)
        valid = blk_ids * SEL_BLK <= pos[:, None]
        imp = jnp.where(forced, SEL_FORCE, jnp.where(valid, imp, -1.0))
        top_val, idx = lax.top_k(imp, n_top)
        kg = gather(ksb, idx)
        vg = gather(vsb, idx)
        kpos = idx[..., None] * SEL_BLK + jnp.arange(SEL_BLK)
        dist_s = pos[:, None, None] - kpos
        mask_s = (dist_s >= 0) & (top_val >= 0)[..., None]
        bias_s = jnp.transpose(lookup(rel3, t5_bucket(dist_s)), (0, 1, 5, 2, 3, 4)).astype(jnp.float32)
        s_s = jnp.einsum('bqhgd,bhqnld->bhgqnl', qb, kg, preferred_element_type=jnp.float32) * scale + bias_s
        nk = n_top * SEL_BLK
        p_s = masked_softmax(s_s.reshape(b, A_KV, g, BLK, nk), mask_s.reshape(b, A_KV, 1, BLK, nk))
        o_s = jnp.einsum('bhgqk,bhqkd->bqhgd', p_s.astype(vg.dtype), vg.reshape(b, A_KV, BLK, nk, hd))
        kw = lax.dynamic_slice_in_dim(kwp, t0, BLK + A_WINDOW, axis=1)
        vw = lax.dynamic_slice_in_dim(vwp, t0, BLK + A_WINDOW, axis=1)
        kpos_w = t0 - A_WINDOW + jnp.arange(BLK + A_WINDOW)
        dist_w = pos[:, None] - kpos_w[None, :]
        mask_w = (dist_w >= 0) & (dist_w < A_WINDOW) & (kpos_w[None, :] >= 0)
        s_w = jnp.einsum('bqhgd,bkhd->bhgqk', qb, kw, preferred_element_type=jnp.float32) * scale + head_bias(dist_w)
        p_w = masked_softmax(s_w, mask_w)
        o_w = jnp.einsum('bhgqk,bkhd->bqhgd', p_w.astype(vw.dtype), vw)
        gb = lax.dynamic_slice_in_dim(gates, t0, BLK, axis=1)
        o = gb[..., 0:1] * o_c + gb[..., 1:2] * o_s + gb[..., 2:3] * o_w
        return o.reshape(b, BLK, hq, hd).astype(q.dtype)

    out = lax.map(block, jnp.arange(nb))
    return jnp.transpose(out, (1, 0, 2, 3, 4)).reshape(b, s, hq, hd)


def causal_depthwise_conv(x, w, bias):
    out = lax.conv_general_dilated(x, w.astype(x.dtype)[:, None, :], window_strides=(1,),
                                   padding=[(B_CONV - 1, 0)], dimension_numbers=('NWC', 'WIO', 'NWC'),
                                   feature_group_count=x.shape[-1])
    return out + bias


def ssd_scan(xdt, adt, bm, cm):
    b, s, G, J, P = xdt.shape
    N = bm.shape[-1]
    L = B_CHUNK
    nc = s // L
    x = xdt.reshape(b, nc, L, G, J, P)
    bm = bm.reshape(b, nc, L, G, N)
    cm = cm.reshape(b, nc, L, G, N)
    acs = jnp.cumsum(jnp.transpose(adt.reshape(b, nc, L, G, J), (0, 1, 3, 4, 2)).astype(jnp.float32), axis=-1)
    causal = jnp.tril(jnp.ones((L, L), dtype=bool))
    decay = jnp.exp(jnp.where(causal, acs[..., :, None] - acs[..., None, :], NEG_INF))
    cb = jnp.einsum('bclgn,bcsgn->bcgls', cm, bm, preferred_element_type=jnp.float32)
    y_diag = jnp.einsum('bcgls,bcgjls,bcsgjp->bclgjp', cb, decay, x)
    states = jnp.einsum('bclgn,bcgjl,bclgjp->bcgjpn', bm, jnp.exp(acs[..., -1:] - acs), x).astype(jnp.float32)

    def step(h, inp):
        st, dec = inp
        return h * dec[..., None, None] + st, h

    init = jnp.zeros((b, G, J, P, N), jnp.float32)
    _, prev = lax.scan(step, init, (jnp.moveaxis(states, 1, 0), jnp.moveaxis(jnp.exp(acs[..., -1]), 1, 0)))
    y_off = jnp.einsum('bclgn,bcgjpn,bcgjl->bclgjp', cm, jnp.moveaxis(prev, 0, 1), jnp.exp(acs))
    return (y_diag + y_off).reshape(b, s, G, J, P)


def ssd_mixer(xbc, dt_raw, z, conv_w, conv_b, dt_bias, a_log, d_skip, norm_w):
    b, s, _ = xbc.shape
    J = B_HEADS // B_GROUPS
    xbc = jax.nn.silu(causal_depthwise_conv(xbc, conv_w, conv_b))
    xs = xbc[..., :B_INNER].reshape(b, s, B_GROUPS, J, B_HEADDIM)
    bm = xbc[..., B_INNER:B_INNER + B_GROUPS * B_STATE].reshape(b, s, B_GROUPS, B_STATE)
    cm = xbc[..., B_INNER + B_GROUPS * B_STATE:].reshape(b, s, B_GROUPS, B_STATE)
    dt = jax.nn.softplus(dt_raw.astype(jnp.float32) + dt_bias).reshape(b, s, B_GROUPS, J)
    a = -jnp.exp(a_log.astype(jnp.float32)).reshape(B_GROUPS, J)
    y = ssd_scan(xs * dt[..., None], dt * a, bm, cm) + xs * d_skip.reshape(B_GROUPS, J)[:, :, None]
    yz = y.reshape(b, s, B_GROUPS, J * B_HEADDIM) * jax.nn.silu(z.reshape(b, s, B_GROUPS, J * B_HEADDIM))
    return rms_norm(yz, norm_w.reshape(B_GROUPS, J * B_HEADDIM)).reshape(b, s, B_INNER)


def strided_window_attention(q, k, v, dil, steps, rel_cols):
    b, s, h, hd = q.shape
    L = s // dil
    Lp = -(-L // BLK) * BLK

    def to_sub(t):
        t = jnp.transpose(t.reshape(b, L, dil, t.shape[2], hd), (0, 2, 1, 3, 4)).reshape(b * dil, L, t.shape[2], hd)
        return jnp.pad(t, ((0, 0), (0, Lp - L), (0, 0), (0, 0)))

    o, lse = banded_attention(to_sub(q), to_sub(k), to_sub(v), steps, rel_cols=rel_cols, dist_scale=dil)
    o = jnp.transpose(o[:, :L].reshape(b, dil, L, h, hd), (0, 2, 1, 3, 4)).reshape(b, s, h, hd)
    lse = jnp.transpose(lse[:, :L].reshape(b, dil, L, h), (0, 2, 1, 3)).reshape(b, s, h)
    return o, lse


def dilated_mixer(q, k, v, rel_d):
    outs, lses = [], []
    for gi, (window, dil) in enumerate(D_PATTERNS):
        o, lse = strided_window_attention(q[:, :, gi], k, v, dil, window // dil,
                                          rel_d[:, gi * D_SLOTS:(gi + 1) * D_SLOTS])
        outs.append(o)
        lses.append(lse)
    alpha = jax.nn.softmax(jnp.stack(lses, axis=2), axis=2)
    return jnp.einsum('bsph,bsphd->bshd', alpha.astype(q.dtype), jnp.stack(outs, axis=2))


def memory_attention(q, mem, norm_w, w_kv):
    b, s, h, hd = q.shape
    kv = jnp.einsum('bmd,de->bme', rms_norm(mem, norm_w), w_kv).reshape(b, mem.shape[1], 2, h, hd)
    sc = jnp.einsum('bshd,bmhd->bhsm', q, kv[:, :, 0], preferred_element_type=jnp.float32) * hd ** -0.5
    p = jax.nn.softmax(sc, axis=-1)
    return jnp.einsum('bhsm,bmhd->bshd', p.astype(kv.dtype), kv[:, :, 1])


def setup_inputs(seed: int = 0) -> dict:
    key = jax.random.key(seed)
    ks = jax.random.split(key, 20)
    f32 = jnp.float32

    def nrm(k, shape, scale):
        return jax.random.normal(k, shape, f32) * scale

    dt = jnp.exp(jax.random.uniform(ks[12], (DEPTH, B_HEADS), f32, math.log(1e-3), math.log(1e-1)))
    return {
        "x": nrm(ks[0], (BATCH, SEQ, D_MODEL), 1.0),
        "mem": nrm(ks[1], (BATCH, MEM_LEN, D_MODEL), 1.0),
        "positions": (jax.random.randint(ks[2], (BATCH, 1), 0, 4096) + jnp.arange(SEQ, dtype=jnp.int32)[None, :]).astype(jnp.int32),
        "pre_norm": 1.0 + nrm(ks[3], (DEPTH, D_MODEL), 0.02),
        "post_norm": 1.0 + nrm(ks[4], (DEPTH, D_MODEL), 0.02),
        "w_in": nrm(ks[5], (DEPTH, D_MODEL, D_IN), D_MODEL ** -0.5),
        "w_out": nrm(ks[6], (DEPTH, MIX_WIDTH, D_MODEL), MIX_WIDTH ** -0.5),
        "rel_bias": nrm(ks[7], (N_BUCKETS, N_BIAS_HEADS), 0.1),
        "a_cmp_pos": nrm(ks[8], (DEPTH, 2, CMP_LEN, HD), 0.02),
        "a_cmp_w1": nrm(ks[9], (DEPTH, 2, CMP_LEN * HD, CMP_HIDDEN), (CMP_LEN * HD) ** -0.5),
        "a_cmp_w2": nrm(ks[10], (DEPTH, 2, CMP_HIDDEN, HD), CMP_HIDDEN ** -0.5),
        "b_conv_w": nrm(ks[11], (DEPTH, B_CONV, B_CONV_DIM), B_CONV ** -0.5),
        "b_conv_b": nrm(ks[13], (DEPTH, B_CONV_DIM), 0.01),
        "b_dt_bias": dt + jnp.log(-jnp.expm1(-dt)),
        "b_a_log": jnp.log(jax.random.uniform(ks[14], (DEPTH, B_HEADS), f32, 1.0, 16.0)),
        "b_d": 1.0 + nrm(ks[15], (DEPTH, B_HEADS), 0.1),
        "b_norm": 1.0 + nrm(ks[16], (DEPTH, B_INNER), 0.02),
        "c_sinks": nrm(ks[17], (DEPTH, C_HEADS), 0.5),
        "m_norm": 1.0 + nrm(ks[18], (DEPTH, D_MODEL), 0.02),
        "m_w_kv": nrm(ks[19], (DEPTH, D_MODEL, 2 * M_HEADS * HD), D_MODEL ** -0.5),
    }


def reference(x, mem, positions, pre_norm, post_norm, w_in, w_out, rel_bias, a_cmp_pos, a_cmp_w1, a_cmp_w2,
              b_conv_w, b_conv_b, b_dt_bias, b_a_log, b_d, b_norm, c_sinks, m_norm, m_w_kv):
    b, s, _ = x.shape
    offsets = [int(o) for o in np.cumsum(IN_SPLITS)[:-1]]
    rel_a = rel_bias[:, :A_HEADS]
    rel_d = rel_bias[:, A_HEADS:]
    silu = jax.nn.silu
    for layer in range(DEPTH):
        h = rms_norm(x, pre_norm[layer])
        proj = jnp.einsum('bsd,de->bse', h, w_in[layer])
        (a_q, a_kv, a_gate, a_z, b_xbc, b_dt, b_z, c_q, c_kv, c_z,
         d_q, d_kv, d_z, m_q, m_z) = jnp.split(proj, offsets, axis=-1)
        a_kv = a_kv.reshape(b, s, 6, A_KV, HD)
        a_out = nsa_mixer(a_q.reshape(b, s, A_HEADS, HD), a_kv[:, :, 0], a_kv[:, :, 1], a_kv[:, :, 2],
                          a_kv[:, :, 3], a_kv[:, :, 4], a_kv[:, :, 5], a_gate.reshape(b, s, A_HEADS, 3),
                          a_cmp_pos[layer], a_cmp_w1[layer], a_cmp_w2[layer], rel_a)
        b_out = ssd_mixer(b_xbc, b_dt, b_z, b_conv_w[layer], b_conv_b[layer], b_dt_bias[layer],
                          b_a_log[layer], b_d[layer], b_norm[layer])
        c_kv = c_kv.reshape(b, s, 2, C_KV, HD)
        c_out, _ = banded_attention(rope(c_q.reshape(b, s, C_HEADS, HD), positions), rope(c_kv[:, :, 0], positions),
                                    c_kv[:, :, 1], C_WINDOW - 1, sinks=c_sinks[layer])
        d_kv = d_kv.reshape(b, s, 2, D_SLOTS, HD)
        d_out = dilated_mixer(d_q.reshape(b, s, D_NPAT, D_SLOTS, HD), d_kv[:, :, 0], d_kv[:, :, 1], rel_d)
        m_out = memory_attention(m_q.reshape(b, s, M_HEADS, HD), mem, m_norm[layer], m_w_kv[layer])
        mix = jnp.concatenate([
            a_out.reshape(b, s, -1) * silu(a_z),
            b_out,
            c_out.reshape(b, s, -1) * silu(c_z),
            d_out.reshape(b, s, -1) * silu(d_z),
            m_out.reshape(b, s, -1) * silu(m_z),
        ], axis=-1)
        x = x + rms_norm(jnp.einsum('bse,ed->bsd', mix, w_out[layer]), post_norm[layer])
    return x
```

```python
import functools
import math

import numpy as np
import jax
import jax.numpy as jnp
from jax import lax
from jax.experimental import pallas as pl
from jax.experimental.pallas import tpu as pltpu

F32 = jnp.float32
MXU_DTYPE = jnp.bfloat16

D_MODEL = 1024
HD = 64
BLK = 128
NEG_INF = -1e30
TINY = 1e-30
EPS = 1e-6

A_HEADS, A_KV = 8, 2
A_G = A_HEADS // A_KV
CMP_LEN, CMP_STRIDE, CMP_HIDDEN = 32, 16, 128
SEL_BLK, SEL_TOPK, A_WINDOW, SEL_FORCE = 64, 16, 512, 1e4
B_HEADS, B_HEADDIM, B_GROUPS, B_STATE, B_CONV, B_CHUNK = 8, 64, 2, 128, 4, 128
B_INNER = B_HEADS * B_HEADDIM
B_CONV_DIM = B_INNER + 2 * B_GROUPS * B_STATE
C_HEADS, C_KV, C_WINDOW = 8, 2, 128
ROPE_THETA = 150000.0
D_SLOTS = 8
D_PATTERNS = ((128, 1), (512, 4), (2048, 16))
D_NPAT = 3
M_HEADS = 4
N_BUCKETS, MAX_DIST = 32, 2048
MIX_WIDTH = A_HEADS * HD + B_INNER + C_HEADS * HD + D_SLOTS * HD + M_HEADS * HD

LANES = 128
VMEM_LIMIT = 56 * 1024 * 1024

_COLS = {}
_off = 0
for _name, _w in (("b_xbc", 1024), ("d_k", 512), ("d_v", 512), ("d_q", 1536), ("a_q", 512), ("a_z", 512),
                  ("b_z", 512), ("c_q", 512), ("c_z", 512), ("d_z", 512), ("a_kv", 768), ("c_k", 128),
                  ("c_v", 128), ("m_q", 256), ("m_z", 256), ("a_gate", 128), ("b_dt", 128)):
    _COLS[_name] = (_off, _w)
    _off += _w
N_PROJ = _off
_SRC = {"a_q": (0, 512), "a_kv": (512, 768), "a_gate": (1280, 24), "a_z": (1304, 512), "b_xbc": (1816, 1024),
        "b_dt": (2840, 8), "b_z": (2848, 512), "c_q": (3360, 512), "c_k": (3872, 128), "c_v": (4000, 128),
        "c_z": (4128, 512), "d_q": (4640, 1536), "d_k": (6176, 512), "d_v": (6688, 512), "d_z": (7200, 512),
        "m_q": (7712, 256), "m_z": (7968, 256)}


def _t5_thresholds():
    d = np.arange(0, 4 * MAX_DIST)
    exact = N_BUCKETS // 2
    rel = np.maximum(d, exact).astype(np.float64)
    large = exact + (np.log(rel / exact) / math.log(MAX_DIST / exact) * (N_BUCKETS - exact)).astype(np.int64)
    bucket = np.where(d < exact, d, np.minimum(large, N_BUCKETS - 1))
    return tuple(int(np.argmax(bucket >= b)) for b in range(1, N_BUCKETS))


_THR = _t5_thresholds()
FAR_DIST = _THR[-1]
N_OFF = -(-(FAR_DIST + BLK) // BLK) + 1
CMP_NEAR = -(-FAR_DIST // CMP_STRIDE) // 8 * 8 + 8
CMP_TAB = 2 * CMP_NEAR + 16
CMP_WIN = CMP_NEAR + 8


def _cparams(sem):
    return pltpu.CompilerParams(dimension_semantics=sem, vmem_limit_bytes=VMEM_LIMIT)


def _mx(x):
    return x.astype(MXU_DTYPE)


def _dot(a, b):
    return jnp.dot(a, b, preferred_element_type=F32)


def _dot_nt(a, b):
    return lax.dot_general(a, b, (((1,), (1,)), ((), ())), preferred_element_type=F32)


def _split3(a):
    hi = _mx(a)
    r1 = a - hi.astype(F32)
    mid = _mx(r1)
    lo = _mx(r1 - mid.astype(F32))
    return hi, mid, lo


def _dot_f32_rhs_exact(a, b):
    hi, mid, lo = _split3(a)
    return _dot(hi, b) + _dot(mid, b) + _dot(lo, b)


def _dot_f32_lhs_exact(a, b):
    hi, mid, lo = _split3(b)
    return _dot(a, hi) + _dot(a, mid) + _dot(a, lo)


def _silu(x):
    return x * jax.nn.sigmoid(x)


def _softmax_cols(s, mask, sink=None):
    s = jnp.where(mask, s, NEG_INF)
    m = jnp.max(s, axis=0, keepdims=True)
    if sink is not None:
        m = jnp.maximum(m, sink)
    p = jnp.where(mask, jnp.exp(s - m), 0.0)
    den = jnp.sum(p, axis=0, keepdims=True)
    if sink is not None:
        den = den + jnp.exp(sink - m)
    return p, den, m


def _bias_table_kernel(rel_ref, c0_ref, o_ref, *, a_row, b_lane, scale, rows):
    h = pl.program_id(0)
    t = pl.program_id(1)
    r = lax.broadcasted_iota(jnp.int32, (rows, LANES), 0)
    l = lax.broadcasted_iota(jnp.int32, (rows, LANES), 1)
    dist = (c0_ref[t] + a_row * r + b_lane * l) * scale
    out = jnp.full((rows, LANES), rel_ref[0, h], F32)
    for b in range(1, N_BUCKETS):
        out = jnp.where(dist >= _THR[b - 1], rel_ref[b, h], out)
    o_ref[0, 0] = out


def _bias_table(rel_cols, c0, *, a_row, b_lane, scale, rows):
    nh = rel_cols.shape[1]
    nt = c0.shape[0]
    return pl.pallas_call(
        functools.partial(_bias_table_kernel, a_row=a_row, b_lane=b_lane, scale=scale, rows=rows),
        grid_spec=pltpu.PrefetchScalarGridSpec(
            num_scalar_prefetch=2, grid=(nh, nt), in_specs=[],
            out_specs=pl.BlockSpec((1, 1, rows, LANES), lambda h, t, *_: (h, t, 0, 0))),
        out_shape=jax.ShapeDtypeStruct((nh, nt, rows, LANES), F32),
        compiler_params=_cparams(("arbitrary", "arbitrary")),
        name="bias_table",
    )(rel_cols, c0)


def _inproj_kernel(x_ref, nw_ref, w_ref, o_ref, h_scr):
    @pl.when(pl.program_id(1) == 0)
    def _():
        x = x_ref[...]
        y = x * lax.rsqrt(jnp.mean(x * x, axis=-1, keepdims=True) + EPS)
        h_scr[...] = _mx(y * nw_ref[...])

    o_ref[...] = _dot(h_scr[...], w_ref[...])


def _inproj(x2, norm_w, w):
    s = x2.shape[0]
    tm, tn = min(1024, s), 768
    return pl.pallas_call(
        _inproj_kernel,
        grid=(s // tm, N_PROJ // tn),
        in_specs=[pl.BlockSpec((tm, D_MODEL), lambda i, j: (i, 0)),
                  pl.BlockSpec((1, D_MODEL), lambda i, j: (0, 0)),
                  pl.BlockSpec((D_MODEL, tn), lambda i, j: (0, j))],
        out_specs=pl.BlockSpec((tm, tn), lambda i, j: (i, j)),
        out_shape=jax.ShapeDtypeStruct((s, N_PROJ), F32),
        scratch_shapes=[pltpu.VMEM((tm, D_MODEL), MXU_DTYPE)],
        compiler_params=_cparams(("arbitrary", "arbitrary")),
        name="inproj",
    )(x2, norm_w, w)


def _compress_kernel(u_ref, pe_ref, w1_ref, w2_ref, o_ref, h2_scr, *, ncmp):
    u = u_ref[0]
    first = _mx(u + pe_ref[0, 0:1, :])
    second = _mx(u + pe_ref[0, 1:2, :])
    h1 = _dot(first, w1_ref[0, 0])
    h2_scr[...] = _dot(second, w1_ref[0, 1])
    pre = h1[:ncmp] + h2_scr[pl.ds(1, ncmp), :]
    o_ref[0] = _dot(_mx(jax.nn.gelu(pre)), w2_ref[0])


def _compress(u, pe, w1, w2, ncmp):
    half = CMP_STRIDE * HD
    return pl.pallas_call(
        functools.partial(_compress_kernel, ncmp=ncmp),
        grid=(2 * A_KV,),
        in_specs=[pl.BlockSpec((1, ncmp + 8, half), lambda i: (i, 0, 0)),
                  pl.BlockSpec((1, 2, half), lambda i: (i // A_KV, 0, 0)),
                  pl.BlockSpec((1, 2, half, CMP_HIDDEN), lambda i: (i // A_KV, 0, 0, 0)),
                  pl.BlockSpec((1, CMP_HIDDEN, HD), lambda i: (i // A_KV, 0, 0))],
        out_specs=pl.BlockSpec((1, ncmp, HD), lambda i: (i, 0, 0)),
        out_shape=jax.ShapeDtypeStruct((2 * A_KV, ncmp, HD), F32),
        scratch_shapes=[pltpu.VMEM((ncmp + 8, CMP_HIDDEN), F32)],
        compiler_params=_cparams(("arbitrary",)),
        name="nsa_compress",
    )(u, pe, w1, w2)


def _nsa_kernel(q_ref, kc_ref, vct_ref, ks_ref, vst_ref, kw_ref, vwt_ref, g_ref, tb_ref, bc_ref, mm_ref,
                o_ref, s_scr, sel_scr, *, seq, tk):
    n = pl.program_id(1)
    ncmp = seq // CMP_STRIDE
    nsel = seq // SEL_BLK
    n_top = min(SEL_TOPK, nsel)
    r_all = A_G * BLK
    q = q_ref[0].reshape(r_all, HD)
    far_row = jnp.concatenate([tb_ref[g, N_OFF - 1, 0:1, :] for g in range(A_G)], axis=1)

    def tile_bias(o):
        return jnp.concatenate([tb_ref[g, o] for g in range(A_G)], axis=1)

    s_scr[...] = _dot_nt(kc_ref[0], q)
    st = jnp.maximum(8 * n - CMP_NEAR, 0)
    off = pl.multiple_of(st - 8 * n + CMP_NEAR, 8)
    st = pl.multiple_of(st, 8)
    near = jnp.concatenate([bc_ref[g, 0, pl.ds(off, CMP_WIN), :] for g in range(A_G)], axis=1)
    s_scr[pl.ds(st, CMP_WIN), :] = s_scr[pl.ds(st, CMP_WIN), :] + near
    jrow = lax.broadcasted_iota(jnp.int32, (ncmp, r_all), 0)
    qpos = n * BLK + (lax.broadcasted_iota(jnp.int32, (ncmp, r_all), 1) & (BLK - 1))
    in_win = (jrow >= st) & (jrow < st + CMP_WIN)
    s_c = s_scr[...] + jnp.where(in_win, 0.0, far_row)
    mask_c = jrow * CMP_STRIDE + (CMP_LEN - 1) <= qpos
    p, den, _ = _softmax_cols(s_c, mask_c)
    p_c = p / jnp.maximum(den, TINY)
    o_c = _dot(vct_ref[0], _mx(p_c))

    pcs = p_c[:, 0:BLK]
    for g in range(1, A_G):
        pcs = pcs + p_c[:, g * BLK:(g + 1) * BLK]
    imp = _dot_f32_lhs_exact(mm_ref[...], pcs)
    b_io = lax.broadcasted_iota(jnp.int32, (nsel, BLK), 0)
    b_f = b_io.astype(F32)
    pos = n * BLK + lax.broadcasted_iota(jnp.int32, (nsel, BLK), 1)
    forced = (b_io == pos // SEL_BLK) | (b_io == 0)
    valid = b_io * SEL_BLK <= pos
    imp = jnp.where(forced, SEL_FORCE, jnp.where(valid, imp, -1.0))

    def pick_one(_, carry):
        imp, sel = carry
        m = jnp.max(imp, axis=0, keepdims=True)
        first = jnp.min(jnp.where(imp == m, b_f, float(nsel)), axis=0, keepdims=True)
        hit = b_f == first
        sel = jnp.where(hit & (m >= 0.0), 1.0, sel)
        return jnp.where(hit, -2.0, imp), sel

    _, sel = lax.fori_loop(0, n_top, pick_one, (imp, jnp.zeros((nsel, BLK), F32)))
    sel_scr[...] = sel

    nblk_t = tk // BLK
    nsel_t = tk // SEL_BLK
    kq_row = lax.broadcasted_iota(jnp.int32, (tk, BLK), 0)
    kq_lane = lax.broadcasted_iota(jnp.int32, (tk, BLK), 1)

    def sel_tile(t, carry):
        m_i, l_i, acc = carry
        k0 = pl.multiple_of(t * tk, tk)
        s = _dot_nt(ks_ref[0, pl.ds(k0, tk), :], q)
        bias = jnp.concatenate(
            [tile_bias(jnp.clip(n - (nblk_t * t + i), 0, N_OFF - 1)) for i in range(nblk_t)], axis=0)
        chosen = jnp.concatenate(
            [jnp.broadcast_to(sel_scr[pl.ds(nsel_t * t + b, 1), :], (SEL_BLK, BLK)) for b in range(nsel_t)], axis=0)
        mk = (chosen > 0.0) & (k0 + kq_row <= n * BLK + kq_lane)
        mk = jnp.concatenate([mk] * A_G, axis=1)
        s = jnp.where(mk, s + bias, NEG_INF)
        m_new = jnp.maximum(m_i, jnp.max(s, axis=0, keepdims=True))
        alpha = jnp.exp(m_i - m_new)
        p = jnp.where(mk, jnp.exp(s - m_new), 0.0)
        l_new = alpha * l_i + jnp.sum(p, axis=0, keepdims=True)
        acc = acc * alpha + _dot(vst_ref[0, :, pl.ds(k0, tk)], _mx(p))
        return m_new, l_new, acc

    init = (jnp.full((1, r_all), NEG_INF, F32), jnp.zeros((1, r_all), F32), jnp.zeros((HD, r_all), F32))
    _, l_s, acc_s = lax.fori_loop(0, (n * BLK) // tk + 1, sel_tile, init)
    o_s = acc_s / jnp.maximum(l_s, TINY)

    nwb = A_WINDOW // BLK + 1
    kb0 = jnp.maximum(n - (nwb - 1), 0)
    k0 = pl.multiple_of(kb0 * BLK, BLK)
    s = _dot_nt(kw_ref[0, pl.ds(k0, nwb * BLK), :], q)
    bias = jnp.concatenate([tile_bias(jnp.clip(n - kb0 - i, 0, N_OFF - 1)) for i in range(nwb)], axis=0)
    w_row = lax.broadcasted_iota(jnp.int32, (nwb * BLK, r_all), 0)
    w_lane = lax.broadcasted_iota(jnp.int32, (nwb * BLK, r_all), 1) & (BLK - 1)
    dist = (n * BLK + w_lane) - (k0 + w_row)
    mask_w = (dist >= 0) & (dist < A_WINDOW)
    p, den, _ = _softmax_cols(s + bias, mask_w)
    o_w = _dot(vwt_ref[0, :, pl.ds(k0, nwb * BLK)], _mx(p / jnp.maximum(den, TINY)))

    gates = jax.nn.sigmoid(g_ref[0, 0])
    o_t = gates[0:1] * o_c + gates[1:2] * o_s + gates[2:3] * o_w
    for pair in range(A_G // 2):
        both = jnp.concatenate([o_t[:, (2 * pair) * BLK:(2 * pair + 1) * BLK],
                                o_t[:, (2 * pair + 1) * BLK:(2 * pair + 2) * BLK]], axis=0)
        o_ref[:, pair * LANES:(pair + 1) * LANES] = both.T


def _nsa(q, kc, vct, ks, vst, kw, vwt, gates, tb_a, bc, mm, seq):
    nb = seq // BLK
    ncmp = seq // CMP_STRIDE
    nsel = seq // SEL_BLK
    tk = min(512, seq)
    full = lambda h, n: (h, 0, 0)
    return pl.pallas_call(
        functools.partial(_nsa_kernel, seq=seq, tk=tk),
        grid=(A_KV, nb),
        in_specs=[pl.BlockSpec((1, A_G, BLK, HD), lambda h, n: (h, 0, n, 0)),
                  pl.BlockSpec((1, ncmp, HD), full),
                  pl.BlockSpec((1, HD, ncmp), full),
                  pl.BlockSpec((1, seq, HD), full),
                  pl.BlockSpec((1, HD, seq), full),
                  pl.BlockSpec((1, seq, HD), full),
                  pl.BlockSpec((1, HD, seq), full),
                  pl.BlockSpec((1, 1, 3, A_G * BLK), lambda h, n: (h, n, 0, 0)),
                  pl.BlockSpec((A_G, N_OFF, BLK, LANES), lambda h, n: (h, 0, 0, 0)),
                  pl.BlockSpec((A_G, 1, CMP_TAB, LANES), lambda h, n: (h, 0, 0, 0)),
                  pl.BlockSpec((nsel, ncmp), lambda h, n: (0, 0))],
        out_specs=pl.BlockSpec((BLK, A_G * HD), lambda h, n: (n, h)),
        out_shape=jax.ShapeDtypeStruct((seq, A_HEADS * HD), F32),
        scratch_shapes=[pltpu.VMEM((ncmp, A_G * BLK), F32), pltpu.VMEM((nsel, BLK), F32)],
        compiler_params=_cparams(("arbitrary", "arbitrary")),
        name="nsa",
    )(q, kc, vct, ks, vst, kw, vwt, gates, tb_a, bc, mm)


def _ssd_kernel(xc_ref, xp_ref, dt_ref, z_ref, cw_ref, cb_ref, dtb_ref, al_ref, d_ref, nw_ref, o_ref,
                xpad_scr, h_scr):
    c = pl.program_id(0)
    L = B_CHUNK

    @pl.when(c == 0)
    def _():
        h_scr[...] = jnp.zeros_like(h_scr)

    xpad_scr[0:8, :] = jnp.where(c > 0, xp_ref[L - 8:L, :], 0.0)
    xpad_scr[8:8 + L, :] = xc_ref[...]
    acc = cw_ref[0:1, :] * xpad_scr[pl.ds(8 - (B_CONV - 1), L), :]
    for i in range(1, B_CONV):
        acc = acc + cw_ref[i:i + 1, :] * xpad_scr[pl.ds(8 - (B_CONV - 1) + i, L), :]
    xa = _silu(acc + cb_ref[...])
    xs = xa[:, :B_INNER]
    bm = xa[:, B_INNER:B_INNER + B_GROUPS * B_STATE]
    cm = xa[:, B_INNER + B_GROUPS * B_STATE:]

    raw = dt_ref[...] + dtb_ref[...]
    dt = jnp.maximum(raw, 0.0) + jnp.log1p(jnp.exp(-jnp.abs(raw)))
    adt = dt * (-jnp.exp(al_ref[...]))
    ri = lax.broadcasted_iota(jnp.int32, (L, L), 0)
    ci = lax.broadcasted_iota(jnp.int32, (L, L), 1)
    tri = _mx(jnp.where(ri >= ci, 1.0, 0.0))
    acs = _dot_f32_lhs_exact(tri, adt)
    acs_t = acs.T
    er = lax.broadcasted_iota(jnp.int32, (LANES, B_INNER), 0)
    ec = lax.broadcasted_iota(jnp.int32, (LANES, B_INNER), 1)
    expand = _mx(jnp.where(ec // B_HEADDIM == er, 1.0, 0.0))
    dt_e = _dot_f32_rhs_exact(dt, expand)
    ea_e = _dot_f32_rhs_exact(jnp.exp(acs), expand)
    we_e = _dot_f32_rhs_exact(jnp.exp(acs[L - 1:L, :] - acs), expand)
    xdt = xs * dt_e
    xw = xdt * we_e
    lane = lax.broadcasted_iota(jnp.int32, (L, LANES), 1)
    hpg = B_HEADS // B_GROUPS
    gw = hpg * B_HEADDIM
    y_groups = []
    for g in range(B_GROUPS):
        b_g = bm[:, g * B_STATE:(g + 1) * B_STATE]
        c_g = _mx(cm[:, g * B_STATE:(g + 1) * B_STATE])
        cb = _dot_nt(c_g, _mx(b_g))
        pairs = []
        for pr in range(hpg // 2):
            x_pair = _mx(xdt[:, g * gw + pr * LANES:g * gw + (pr + 1) * LANES])
            halves = []
            for j in range(2):
                h = g * hpg + 2 * pr + j
                diff = acs[:, h:h + 1] - acs_t[h:h + 1, :]
                decay = jnp.exp(jnp.where(ri >= ci, diff, NEG_INF))
                halves.append(_dot(_mx(cb * decay), x_pair))
            pairs.append(jnp.where(lane < B_HEADDIM, halves[0], halves[1]))
        y_diag = jnp.concatenate(pairs, axis=1)
        h_prev = h_scr[:, g * gw:(g + 1) * gw]
        y_off = _dot(c_g, _mx(h_prev)) * ea_e[:, g * gw:(g + 1) * gw]
        st = _dot(_mx(b_g.T), _mx(xw[:, g * gw:(g + 1) * gw]))
        h_scr[:, g * gw:(g + 1) * gw] = h_prev * ea_e[L - 1:L, g * gw:(g + 1) * gw] + st
        y = y_diag + y_off + xs[:, g * gw:(g + 1) * gw] * d_ref[:, g * gw:(g + 1) * gw]
        yz = y * _silu(z_ref[:, g * gw:(g + 1) * gw])
        ms = jnp.mean(yz * yz, axis=-1, keepdims=True)
        y_groups.append(yz * lax.rsqrt(ms + EPS) * nw_ref[:, g * gw:(g + 1) * gw])
    o_ref[...] = jnp.concatenate(y_groups, axis=1)


def _ssd(proj, conv_w, conv_b, dtb_row, al_row, d_row, nw_row):
    s = proj.shape[0]
    L = B_CHUNK
    cx = _COLS["b_xbc"][0] // B_CONV_DIM
    cdt = _COLS["b_dt"][0] // LANES
    cz = _COLS["b_z"][0] // B_INNER
    row = lambda w: pl.BlockSpec((1, w), lambda c: (0, 0))
    return pl.pallas_call(
        _ssd_kernel,
        grid=(s // L,),
        in_specs=[pl.BlockSpec((L, B_CONV_DIM), lambda c: (c, cx)),
                  pl.BlockSpec((L, B_CONV_DIM), lambda c: (jnp.maximum(c - 1, 0), cx)),
                  pl.BlockSpec((L, LANES), lambda c: (c, cdt)),
                  pl.BlockSpec((L, B_INNER), lambda c: (c, cz)),
                  pl.BlockSpec((B_CONV, B_CONV_DIM), lambda c: (0, 0)),
                  row(B_CONV_DIM), row(LANES), row(LANES), row(B_INNER), row(B_INNER)],
        out_specs=pl.BlockSpec((L, B_INNER), lambda c: (c, 0)),
        out_shape=jax.ShapeDtypeStruct((s, B_INNER), F32),
        scratch_shapes=[pltpu.VMEM((L + 8, B_CONV_DIM), F32), pltpu.VMEM((B_STATE, B_INNER), F32)],
        compiler_params=_cparams(("arbitrary",)),
        name="ssd",
    )(proj, proj, proj, proj, conv_w, conv_b, dtb_row, al_row, d_row, nw_row)


def _rope_kernel(q_ref, k_ref, pos_ref, inv_ref, qo_ref, ko_ref):
    ang = pos_ref[...].astype(F32) * inv_ref[...]
    cos, sin = jnp.cos(ang), jnp.sin(ang)
    lane = lax.broadcasted_iota(jnp.int32, ang.shape, 1)
    low = (lane & (HD - 1)) < HD // 2

    def rot(t):
        partner = jnp.where(low, -pltpu.roll(t, LANES - HD // 2, 1), pltpu.roll(t, HD // 2, 1))
        return t * cos + partner * sin

    for j in range(q_ref.shape[1] // LANES):
        qo_ref[:, j * LANES:(j + 1) * LANES] = rot(q_ref[:, j * LANES:(j + 1) * LANES])
    ko_ref[...] = rot(k_ref[...])


def _rope(proj, pos_col, inv_row):
    s = proj.shape[0]
    tm = min(512, s)
    wq = C_HEADS * HD
    return pl.pallas_call(
        _rope_kernel,
        grid=(s // tm,),
        in_specs=[pl.BlockSpec((tm, wq), lambda i: (i, _COLS["c_q"][0] // wq)),
                  pl.BlockSpec((tm, LANES), lambda i: (i, _COLS["c_k"][0] // LANES)),
                  pl.BlockSpec((tm, 1), lambda i: (i, 0)),
                  pl.BlockSpec((1, LANES), lambda i: (0, 0))],
        out_specs=[pl.BlockSpec((tm, wq), lambda i: (i, 0)), pl.BlockSpec((tm, LANES), lambda i: (i, 0))],
        out_shape=[jax.ShapeDtypeStruct((s, wq), F32), jax.ShapeDtypeStruct((s, LANES), F32)],
        compiler_params=_cparams(("arbitrary",)),
        name="rope",
    )(proj, proj, pos_col, inv_row)


def _band_kernel(*refs, nq, nk, gb, max_dist, has_bias, has_sink, want_lse):
    refs = list(refs)
    q_ref, k_ref, kh_ref, vt_ref, vth_ref = refs[:5]
    pos = 5
    tb_ref = sk_ref = None
    if has_bias:
        tb_ref = refs[pos]
        pos += 1
    if has_sink:
        sk_ref = refs[pos]
        pos += 1
    o_ref = refs[pos]
    lse_ref = refs[pos + 1] if want_lse else None
    i = pl.program_id(1)
    krow = lax.broadcasted_iota(jnp.int32, (2 * BLK, BLK), 0)
    qlane = lax.broadcasted_iota(jnp.int32, (2 * BLK, BLK), 1)
    dist = BLK + qlane - krow
    band = (dist >= 0) & (dist <= max_dist)
    for jb in range(gb):
        outs, lses = [], []
        for a in range(nq):
            kk = a * nk // nq
            if jb == 0:
                k_prev, vt_prev = kh_ref[kk], vth_ref[kk]
                mask = band & (krow >= jnp.where(i > 0, 0, BLK))
            else:
                k_prev = k_ref[kk, (jb - 1) * BLK:jb * BLK, :]
                vt_prev = vt_ref[kk, :, (jb - 1) * BLK:jb * BLK]
                mask = band
            k_win = jnp.concatenate([k_prev, k_ref[kk, jb * BLK:(jb + 1) * BLK, :]], axis=0)
            vt_win = jnp.concatenate([vt_prev, vt_ref[kk, :, jb * BLK:(jb + 1) * BLK]], axis=1)
            s = _dot_nt(k_win, q_ref[a, jb * BLK:(jb + 1) * BLK, :])
            if has_bias:
                s = s + jnp.concatenate([tb_ref[a, 1], tb_ref[a, 0]], axis=0)
            sink = sk_ref[a] if has_sink else None
            p, den, m = _softmax_cols(s, mask, sink)
            outs.append(_dot(vt_win, _mx(p / den)))
            if want_lse:
                lses.append(jnp.broadcast_to(m + jnp.log(den), (HD, BLK)))
        for pr in range(nq // 2):
            o_ref[jb * BLK:(jb + 1) * BLK, pr * LANES:(pr + 1) * LANES] = \
                jnp.concatenate([outs[2 * pr], outs[2 * pr + 1]], axis=0).T
            if want_lse:
                lse_ref[jb * BLK:(jb + 1) * BLK, pr * LANES:(pr + 1) * LANES] = \
                    jnp.concatenate([lses[2 * pr], lses[2 * pr + 1]], axis=0).T


def _band(q, k, vt, *, nq, nk, max_dist, tb=None, sinks=None, want_lse=False):
    n_q, L, _ = q.shape
    gb = min(8, L // BLK)
    steps = n_q // nq
    halo = lambda s, i: (s, jnp.maximum(i * gb - 1, 0), 0)
    halo_t = lambda s, i: (s, 0, jnp.maximum(i * gb - 1, 0))
    in_specs = [pl.BlockSpec((nq, gb * BLK, HD), lambda s, i: (s, i, 0)),
                pl.BlockSpec((nk, gb * BLK, HD), lambda s, i: (s, i, 0)),
                pl.BlockSpec((nk, BLK, HD), halo),
                pl.BlockSpec((nk, HD, gb * BLK), lambda s, i: (s, 0, i)),
                pl.BlockSpec((nk, HD, BLK), halo_t)]
    args = [q, k, k, vt, vt]
    if tb is not None:
        nh = tb.shape[0] // nq
        in_specs.append(pl.BlockSpec((nq, 2, BLK, LANES), lambda s, i: (s % nh, 0, 0, 0)))
        args.append(tb)
    if sinks is not None:
        in_specs.append(pl.BlockSpec((nq, 1, LANES), lambda s, i: (s, 0, 0)))
        args.append(sinks)
    o_spec = pl.BlockSpec((gb * BLK, nq * HD), lambda s, i: (i, s))
    o_shape = jax.ShapeDtypeStruct((L, n_q * HD), F32)
    return pl.pallas_call(
        functools.partial(_band_kernel, nq=nq, nk=nk, gb=gb, max_dist=max_dist, has_bias=tb is not None,
                          has_sink=sinks is not None, want_lse=want_lse),
        grid=(steps, L // (gb * BLK)),
        in_specs=in_specs,
        out_specs=[o_spec, o_spec] if want_lse else o_spec,
        out_shape=[o_shape, o_shape] if want_lse else o_shape,
        compiler_params=_cparams(("arbitrary", "arbitrary")),
        name="band_attention",
    )(*args)


def _memkv_kernel(mem_ref, nw_ref, w_ref, o_ref):
    x = mem_ref[...]
    y = x * lax.rsqrt(jnp.mean(x * x, axis=-1, keepdims=True) + EPS) * nw_ref[...]
    o_ref[...] = _dot(_mx(y), w_ref[...])


def _memkv(mem2, norm_w, w):
    return pl.pallas_call(
        _memkv_kernel,
        out_shape=jax.ShapeDtypeStruct((mem2.shape[0], w.shape[1]), F32),
        compiler_params=pltpu.CompilerParams(vmem_limit_bytes=VMEM_LIMIT),
        name="mem_kv",
    )(mem2, norm_w, w)


def _memattn_kernel(q_ref, k_ref, vt_ref, o_ref):
    outs = []
    for h in range(M_HEADS):
        s = _dot_nt(k_ref[h], q_ref[h])
        m = jnp.max(s, axis=0, keepdims=True)
        p = jnp.exp(s - m)
        p = p / jnp.sum(p, axis=0, keepdims=True)
        outs.append(_dot(vt_ref[h], _mx(p)))
    for pr in range(M_HEADS // 2):
        o_ref[:, pr * LANES:(pr + 1) * LANES] = jnp.concatenate([outs[2 * pr], outs[2 * pr + 1]], axis=0).T


def _memattn(q, k, vt):
    _, s, _ = q.shape
    tq = min(512, s)
    ml = k.shape[1]
    return pl.pallas_call(
        _memattn_kernel,
        grid=(s // tq,),
        in_specs=[pl.BlockSpec((M_HEADS, tq, HD), lambda i: (0, i, 0)),
                  pl.BlockSpec((M_HEADS, ml, HD), lambda i: (0, 0, 0)),
                  pl.BlockSpec((M_HEADS, HD, ml), lambda i: (0, 0, 0))],
        out_specs=pl.BlockSpec((tq, M_HEADS * HD), lambda i: (i, 0)),
        out_shape=jax.ShapeDtypeStruct((s, M_HEADS * HD), F32),
        compiler_params=_cparams(("arbitrary",)),
        name="mem_attention",
    )(q, k, vt)


def _outproj_kernel(x_ref, a_ref, az_ref, b_ref, c_ref, cz_ref, d0_ref, d1_ref, d2_ref, l0_ref, l1_ref, l2_ref,
                    dz_ref, m_ref, mz_ref, w_ref, nw_ref, o_ref):
    l0, l1, l2 = l0_ref[...], l1_ref[...], l2_ref[...]
    mx = jnp.maximum(jnp.maximum(l0, l1), l2)
    e0, e1, e2 = jnp.exp(l0 - mx), jnp.exp(l1 - mx), jnp.exp(l2 - mx)
    inv = 1.0 / (e0 + e1 + e2)
    d = (e0 * inv) * d0_ref[...] + (e1 * inv) * d1_ref[...] + (e2 * inv) * d2_ref[...]
    pieces = (a_ref[...] * _silu(az_ref[...]), b_ref[...], c_ref[...] * _silu(cz_ref[...]),
              d * _silu(dz_ref[...]), m_ref[...] * _silu(mz_ref[...]))
    y = None
    row = 0
    for piece in pieces:
        w = piece.shape[1]
        part = _dot(_mx(piece), w_ref[row:row + w, :])
        y = part if y is None else y + part
        row += w
    y = y * lax.rsqrt(jnp.mean(y * y, axis=-1, keepdims=True) + EPS) * nw_ref[...]
    o_ref[...] = x_ref[...] + y


def _outproj(x2, proj, a_out, b_out, c_out, d_outs, d_lses, m_out, w, norm_w):
    s = x2.shape[0]
    tm = min(512, s)
    wide = lambda width, col: pl.BlockSpec((tm, width), lambda i: (i, col))
    w512 = A_HEADS * HD
    wm = M_HEADS * HD
    return pl.pallas_call(
        _outproj_kernel,
        grid=(s // tm,),
        in_specs=[wide(D_MODEL, 0), wide(w512, 0), wide(w512, _COLS["a_z"][0] // w512), wide(w512, 0),
                  wide(w512, 0), wide(w512, _COLS["c_z"][0] // w512),
                  wide(w512, 0), wide(w512, 0), wide(w512, 0), wide(w512, 0), wide(w512, 0), wide(w512, 0),
                  wide(w512, _COLS["d_z"][0] // w512), wide(wm, 0), wide(wm, _COLS["m_z"][0] // wm),
                  pl.BlockSpec((MIX_WIDTH, D_MODEL), lambda i: (0, 0)),
                  pl.BlockSpec((1, D_MODEL), lambda i: (0, 0))],
        out_specs=wide(D_MODEL, 0),
        out_shape=jax.ShapeDtypeStruct((s, D_MODEL), F32),
        compiler_params=_cparams(("arbitrary",)),
        name="outproj",
    )(x2, a_out, proj, b_out, c_out, proj, *d_outs, *d_lses, proj, m_out, proj, w, norm_w)


def _repack_w_in(w):
    out = []
    for name, (_, width) in _COLS.items():
        src, true_w = _SRC[name]
        piece = w[:, src:src + true_w]
        if true_w < width:
            piece = jnp.pad(piece, ((0, 0), (0, width - true_w)))
        out.append(piece)
    return jnp.concatenate(out, axis=1)


def _importance_matrix(nsel, ncmp):
    per = SEL_BLK // CMP_STRIDE
    m = np.zeros((nsel, ncmp), np.float32)
    for b in range(nsel):
        for i in range(per):
            for j in (per * b + i, per * b + i - 1):
                if 0 <= j < ncmp:
                    m[b, j] += 1.0
    return m


def _heads_first(t, nh):
    s = t.shape[0]
    return jnp.transpose(t.reshape(s, nh, HD), (1, 0, 2))


def _heads_first_t(t, nh):
    s = t.shape[0]
    return jnp.transpose(t.reshape(s, nh, HD), (1, 2, 0))


def _col(proj, name):
    off, width = _COLS[name]
    return proj[:, off:off + width]


def kernel(x, mem, positions, pre_norm, post_norm, w_in, w_out, rel_bias, a_cmp_pos, a_cmp_w1, a_cmp_w2,
           b_conv_w, b_conv_b, b_dt_bias, b_a_log, b_d, b_norm, c_sinks, m_norm, m_w_kv):
    b, s, _ = x.shape
    assert b == 1 and s % 2048 == 0, "sequence length must be a multiple of 2048 and batch 1"
    depth = w_in.shape[0]
    nb = s // BLK
    ncmp = s // CMP_STRIDE
    nsel = s // SEL_BLK
    scale = HD ** -0.5

    rel_a = rel_bias[:, :A_HEADS]
    rel_d = rel_bias[:, A_HEADS:]
    offs = jnp.arange(N_OFF, dtype=jnp.int32) * BLK
    tb_a = _bias_table(rel_a, offs, a_row=-1, b_lane=1, scale=1, rows=BLK)
    bc_a = _bias_table(rel_a, jnp.array([CMP_STRIDE * CMP_NEAR - (CMP_LEN - 1)], jnp.int32),
                       a_row=-CMP_STRIDE, b_lane=1, scale=1, rows=CMP_TAB)
    tb_d = [_bias_table(rel_d[:, p * D_SLOTS:(p + 1) * D_SLOTS], offs[:2], a_row=-1, b_lane=1, scale=dil, rows=BLK)
            for p, (_, dil) in enumerate(D_PATTERNS)]
    mm = jnp.asarray(_importance_matrix(nsel, ncmp), MXU_DTYPE)
    half = ROPE_THETA ** (-jnp.arange(HD // 2, dtype=F32) / (HD // 2))
    inv_row = jnp.tile(half, LANES // (HD // 2)).reshape(1, LANES)
    pos_col = positions.reshape(s, 1)
    mem2 = mem.reshape(mem.shape[1], D_MODEL)

    x2 = x.reshape(s, D_MODEL)
    for layer in range(depth):
        proj = _inproj(x2, pre_norm[layer].reshape(1, D_MODEL), _mx(_repack_w_in(w_in[layer])))

        a_q = _mx(_heads_first(_col(proj, "a_q") * scale, A_HEADS)).reshape(A_KV, A_G, s, HD)
        a_kv = _col(proj, "a_kv")
        part = lambda i: a_kv[:, i * A_KV * HD:(i + 1) * A_KV * HD]
        u = jnp.concatenate([_heads_first(part(0), A_KV), _heads_first(part(1), A_KV)], axis=0)
        u = jnp.pad(u.reshape(2 * A_KV, ncmp, CMP_STRIDE * HD), ((0, 0), (0, 8), (0, 0)))
        cmp_out = _compress(u, a_cmp_pos[layer].reshape(2, 2, CMP_STRIDE * HD),
                            _mx(a_cmp_w1[layer].reshape(2, 2, CMP_STRIDE * HD, CMP_HIDDEN)),
                            _mx(a_cmp_w2[layer]), ncmp)
        kc = _mx(cmp_out[:A_KV])
        vct = _mx(jnp.transpose(cmp_out[A_KV:], (0, 2, 1)))
        gates = jnp.transpose(_col(proj, "a_gate")[:, :3 * A_HEADS].reshape(nb, BLK, A_KV, A_G, 3),
                              (2, 0, 4, 3, 1)).reshape(A_KV, nb, 3, A_G * BLK)
        a_out = _nsa(a_q, kc, vct, _mx(_heads_first(part(2), A_KV)), _mx(_heads_first_t(part(3), A_KV)),
                     _mx(_heads_first(part(4), A_KV)), _mx(_heads_first_t(part(5), A_KV)), gates, tb_a, bc_a, mm, s)

        pad_row = lambda v: jnp.pad(v, (0, LANES - v.shape[0])).reshape(1, LANES)
        b_out = _ssd(proj, b_conv_w[layer], b_conv_b[layer].reshape(1, B_CONV_DIM), pad_row(b_dt_bias[layer]),
                     pad_row(b_a_log[layer]), jnp.repeat(b_d[layer], B_HEADDIM).reshape(1, B_INNER),
                     b_norm[layer].reshape(1, B_INNER))

        cq_rot, ck_rot = _rope(proj, pos_col, inv_row)
        sinks = jnp.broadcast_to(c_sinks[layer].reshape(C_HEADS, 1, 1), (C_HEADS, 1, LANES))
        c_out = _band(_mx(_heads_first(cq_rot * scale, C_HEADS)), _mx(_heads_first(ck_rot, C_KV)),
                      _mx(_heads_first_t(_col(proj, "c_v"), C_KV)), nq=C_HEADS // C_KV, nk=1,
                      max_dist=C_WINDOW - 1, sinks=sinks)

        d_q = _col(proj, "d_q")
        d_k = _col(proj, "d_k")
        d_v = _col(proj, "d_v")
        d_outs, d_lses = [], []
        for p, (window, dil) in enumerate(D_PATTERNS):
            L = s // dil
            to_seq = lambda t: jnp.transpose(t.reshape(L, dil * D_SLOTS, HD), (1, 0, 2))
            q_p = d_q[:, p * D_SLOTS * HD:(p + 1) * D_SLOTS * HD] * scale
            o_p, lse_p = _band(_mx(to_seq(q_p)), _mx(to_seq(d_k)), _mx(jnp.transpose(to_seq(d_v), (0, 2, 1))),
                               nq=2, nk=2, max_dist=window // dil, tb=tb_d[p], want_lse=True)
            d_outs.append(o_p.reshape(s, D_SLOTS * HD))
            d_lses.append(lse_p.reshape(s, D_SLOTS * HD))

        kv = _memkv(mem2, m_norm[layer].reshape(1, D_MODEL), _mx(m_w_kv[layer]))
        m_out = _memattn(_mx(_heads_first(_col(proj, "m_q") * scale, M_HEADS)),
                         _mx(_heads_first(kv[:, :M_HEADS * HD], M_HEADS)),
                         _mx(_heads_first_t(kv[:, M_HEADS * HD:], M_HEADS)))

        x2 = _outproj(x2, proj, a_out, b_out, c_out, d_outs, d_lses, m_out, _mx(w_out[layer]),
                      post_norm[layer].reshape(1, D_MODEL))
    return x2.reshape(b, s, D_MODEL)
```

```python
import functools
import math

import numpy as np
import jax
import jax.numpy as jnp
from jax import lax
from jax.experimental import pallas as pl
from jax.experimental.pallas import tpu as pltpu

F32 = jnp.float32
MXU_DTYPE = jnp.bfloat16

D_MODEL = 1024
HD = 64
BLK = 128
NEG_INF = -1e30
TINY = 1e-30
EPS = 1e-6

A_HEADS, A_KV = 8, 2
A_G = A_HEADS // A_KV
CMP_LEN, CMP_STRIDE, CMP_HIDDEN = 32, 16, 128
SEL_BLK, SEL_TOPK, A_WINDOW, SEL_FORCE = 64, 16, 512, 1e4
B_HEADS, B_HEADDIM, B_GROUPS, B_STATE, B_CONV, B_CHUNK = 8, 64, 2, 128, 4, 128
B_INNER = B_HEADS * B_HEADDIM
B_CONV_DIM = B_INNER + 2 * B_GROUPS * B_STATE
C_HEADS, C_KV, C_WINDOW = 8, 2, 128
ROPE_THETA = 150000.0
D_SLOTS = 8
D_PATTERNS = ((128, 1), (512, 4), (2048, 16))
D_NPAT = 3
M_HEADS = 4
N_BUCKETS, MAX_DIST = 32, 2048
MIX_WIDTH = A_HEADS * HD + B_INNER + C_HEADS * HD + D_SLOTS * HD + M_HEADS * HD

LANES = 128
VMEM_LIMIT = 56 * 1024 * 1024

_COLS = {}
_off = 0
for _name, _w in (("b_xbc", 1024), ("d_k", 512), ("d_v", 512), ("d_q", 1536), ("a_q", 512), ("a_z", 512),
                  ("b_z", 512), ("c_q", 512), ("c_z", 512), ("d_z", 512), ("a_kv", 768), ("c_k", 128),
                  ("c_v", 128), ("m_q", 256), ("m_z", 256), ("a_gate", 128), ("b_dt", 128)):
    _COLS[_name] = (_off, _w)
    _off += _w
N_PROJ = _off
_SRC = {"a_q": (0, 512), "a_kv": (512, 768), "a_gate": (1280, 24), "a_z": (1304, 512), "b_xbc": (1816, 1024),
        "b_dt": (2840, 8), "b_z": (2848, 512), "c_q": (3360, 512), "c_k": (3872, 128), "c_v": (4000, 128),
        "c_z": (4128, 512), "d_q": (4640, 1536), "d_k": (6176, 512), "d_v": (6688, 512), "d_z": (7200, 512),
        "m_q": (7712, 256), "m_z": (7968, 256)}


def _t5_thresholds():
    d = np.arange(0, 4 * MAX_DIST)
    exact = N_BUCKETS // 2
    rel = np.maximum(d, exact).astype(np.float64)
    large = exact + (np.log(rel / exact) / math.log(MAX_DIST / exact) * (N_BUCKETS - exact)).astype(np.int64)
    bucket = np.where(d < exact, d, np.minimum(large, N_BUCKETS - 1))
    return tuple(int(np.argmax(bucket >= b)) for b in range(1, N_BUCKETS))


_THR = _t5_thresholds()
FAR_DIST = _THR[-1]
N_OFF = -(-(FAR_DIST + BLK) // BLK) + 1
CMP_REACH = -(-FAR_DIST // CMP_STRIDE) // 8 * 8 + 8
CMP_WIN = 128
CMP_NEAR = 128
CMP_TAB = CMP_NEAR + CMP_WIN
SEL_TILE = 512
AUG_SEL = HD + 8
AUG_END = AUG_SEL + SEL_TILE // SEL_BLK
AUG_K = LANES
V_ROWS = HD + 16


def _cparams(sem):
    return pltpu.CompilerParams(dimension_semantics=sem, vmem_limit_bytes=VMEM_LIMIT)


def _mx(x):
    return x.astype(MXU_DTYPE)


def _dot(a, b):
    return jnp.dot(a, b, preferred_element_type=F32)


def _dot_nt(a, b):
    return lax.dot_general(a, b, (((1,), (1,)), ((), ())), preferred_element_type=F32)


def _split3(a):
    hi = _mx(a)
    r1 = a - hi.astype(F32)
    mid = _mx(r1)
    lo = _mx(r1 - mid.astype(F32))
    return hi, mid, lo


def _dot_f32_rhs_exact(a, b):
    hi, mid, lo = _split3(a)
    return _dot(hi, b) + _dot(mid, b) + _dot(lo, b)


def _dot_f32_lhs_exact(a, b):
    hi, mid, lo = _split3(b)
    return _dot(a, hi) + _dot(a, mid) + _dot(a, lo)


def _silu(x):
    return x * jax.nn.sigmoid(x)


def _softmax_cols(s, mask, sink=None):
    s = jnp.where(mask, s, NEG_INF)
    m = jnp.max(s, axis=0, keepdims=True)
    if sink is not None:
        m = jnp.maximum(m, sink)
    p = jnp.where(mask, jnp.exp(s - m), 0.0)
    den = jnp.sum(p, axis=0, keepdims=True)
    if sink is not None:
        den = den + jnp.exp(sink - m)
    return p, den, m


def _bias_table_kernel(rel_ref, c0_ref, o_ref, *, a_row, b_lane, scale, rows):
    h = pl.program_id(0)
    t = pl.program_id(1)
    r = lax.broadcasted_iota(jnp.int32, (rows, LANES), 0)
    l = lax.broadcasted_iota(jnp.int32, (rows, LANES), 1)
    dist = (c0_ref[t] + a_row * r + b_lane * l) * scale
    out = jnp.full((rows, LANES), rel_ref[0, h], F32)
    for b in range(1, N_BUCKETS):
        out = jnp.where(dist >= _THR[b - 1], rel_ref[b, h], out)
    o_ref[0, 0] = out


def _bias_table(rel_cols, c0, *, a_row, b_lane, scale, rows):
    nh = rel_cols.shape[1]
    nt = c0.shape[0]
    return pl.pallas_call(
        functools.partial(_bias_table_kernel, a_row=a_row, b_lane=b_lane, scale=scale, rows=rows),
        grid_spec=pltpu.PrefetchScalarGridSpec(
            num_scalar_prefetch=2, grid=(nh, nt), in_specs=[],
            out_specs=pl.BlockSpec((1, 1, rows, LANES), lambda h, t, *_: (h, t, 0, 0))),
        out_shape=jax.ShapeDtypeStruct((nh, nt, rows, LANES), F32),
        compiler_params=_cparams(("arbitrary", "arbitrary")),
        name="bias_table",
    )(rel_cols, c0)


def _inproj_kernel(x_ref, nw_ref, w_ref, o_ref, h_scr):
    @pl.when(pl.program_id(1) == 0)
    def _():
        x = x_ref[...]
        y = x * lax.rsqrt(jnp.mean(x * x, axis=-1, keepdims=True) + EPS)
        h_scr[...] = _mx(y * nw_ref[...])

    o_ref[...] = _dot(h_scr[...], w_ref[...])


def _inproj(x2, norm_w, w):
    s = x2.shape[0]
    tm, tn = min(1024, s), 768
    return pl.pallas_call(
        _inproj_kernel,
        grid=(s // tm, N_PROJ // tn),
        in_specs=[pl.BlockSpec((tm, D_MODEL), lambda i, j: (i, 0)),
                  pl.BlockSpec((1, D_MODEL), lambda i, j: (0, 0)),
                  pl.BlockSpec((D_MODEL, tn), lambda i, j: (0, j))],
        out_specs=pl.BlockSpec((tm, tn), lambda i, j: (i, j)),
        out_shape=jax.ShapeDtypeStruct((s, N_PROJ), F32),
        scratch_shapes=[pltpu.VMEM((tm, D_MODEL), MXU_DTYPE)],
        compiler_params=_cparams(("arbitrary", "arbitrary")),
        name="inproj",
    )(x2, norm_w, w)


def _compress_kernel(u_ref, pe_ref, w1_ref, w2_ref, o_ref, h2_scr, *, ncmp):
    u = u_ref[0]
    first = _mx(u + pe_ref[0, 0:1, :])
    second = _mx(u + pe_ref[0, 1:2, :])
    h1 = _dot(first, w1_ref[0, 0])
    h2_scr[...] = _dot(second, w1_ref[0, 1])
    pre = h1[:ncmp] + h2_scr[pl.ds(1, ncmp), :]
    o_ref[0] = _dot(_mx(jax.nn.gelu(pre)), w2_ref[0])


def _compress(u, pe, w1, w2, ncmp):
    half = CMP_STRIDE * HD
    return pl.pallas_call(
        functools.partial(_compress_kernel, ncmp=ncmp),
        grid=(2 * A_KV,),
        in_specs=[pl.BlockSpec((1, ncmp + 8, half), lambda i: (i, 0, 0)),
                  pl.BlockSpec((1, 2, half), lambda i: (i // A_KV, 0, 0)),
                  pl.BlockSpec((1, 2, half, CMP_HIDDEN), lambda i: (i // A_KV, 0, 0, 0)),
                  pl.BlockSpec((1, CMP_HIDDEN, HD), lambda i: (i // A_KV, 0, 0))],
        out_specs=pl.BlockSpec((1, ncmp, HD), lambda i: (i, 0, 0)),
        out_shape=jax.ShapeDtypeStruct((2 * A_KV, ncmp, HD), F32),
        scratch_shapes=[pltpu.VMEM((ncmp + 8, CMP_HIDDEN), F32)],
        compiler_params=_cparams(("arbitrary",)),
        name="nsa_compress",
    )(u, pe, w1, w2)


def _nsa_kernel(qt_ref, kc_ref, vct_ref, ks_ref, vst_ref, kw_ref, vwt_ref, g_ref, tb_ref, bc_ref,
                o_ref, s_scr, sel_scr, qa_scr, sa_scr, sb_scr, pcs_scr, *, seq, tk):
    n = pl.program_id(1)
    ncmp = seq // CMP_STRIDE
    nsel = seq // SEL_BLK
    n_top = min(SEL_TOPK, nsel)
    r_all = A_G * BLK
    q_t = qt_ref[0, 0]
    far_row = jnp.concatenate([tb_ref[g, N_OFF - 1, 0:1, :] for g in range(A_G)], axis=1)

    def tile_bias(o):
        return jnp.concatenate([tb_ref[g, o] for g in range(A_G)], axis=1)

    far_hi = _mx(far_row).astype(F32)
    bias_rows = jnp.concatenate([far_hi, far_row - far_hi, jnp.zeros((AUG_SEL - HD - 2, r_all), F32)], axis=0)
    qa_scr[0:HD, :] = q_t
    qa_scr[HD:AUG_END, :] = _mx(jnp.concatenate([bias_rows, jnp.zeros((AUG_END - AUG_SEL, r_all), F32)], axis=0))
    qa_scr[AUG_END:, :] = jnp.zeros((AUG_K - AUG_END, r_all), MXU_DTYPE)

    s_scr[...] = _dot(kc_ref[0], qa_scr[...])
    st = jnp.clip((8 * n - CMP_REACH) // 16 * 16, 0, ncmp - CMP_WIN)
    off = pl.multiple_of(st - 8 * n + CMP_NEAR, 8)
    st = pl.multiple_of(st, 16)
    near = jnp.concatenate([bc_ref[g, 0, pl.ds(off, CMP_WIN), :] for g in range(A_G)], axis=1)
    s_scr[pl.ds(st, CMP_WIN), :] = _dot(kc_ref[0, pl.ds(st, CMP_WIN), 0:HD], q_t) + near
    jrow = lax.broadcasted_iota(jnp.int32, (ncmp, BLK), 0)
    qpos = n * BLK + lax.broadcasted_iota(jnp.int32, (ncmp, BLK), 1)
    visible = jnp.where(jrow * CMP_STRIDE + (CMP_LEN - 1) <= qpos, 0.0, NEG_INF)
    s_c = s_scr[...] + jnp.concatenate([visible] * A_G, axis=1)
    m = jnp.max(s_c, axis=0, keepdims=True)
    e = jnp.exp(s_c - m)
    den = jnp.sum(e, axis=0, keepdims=True)
    inv = jnp.where(m > 0.5 * NEG_INF, 1.0 / jnp.maximum(den, TINY), 0.0)
    p_c = e * inv
    o_c = _dot(vct_ref[0], _mx(p_c))

    pcs = p_c[:, 0:BLK]
    for g in range(1, A_G):
        pcs = pcs + p_c[:, g * BLK:(g + 1) * BLK]
    per = SEL_BLK // CMP_STRIDE
    pcs_scr[0:8, :] = jnp.zeros((8, BLK), F32)
    pcs_scr[8:8 + ncmp, :] = pcs
    rows = [pcs_scr[pl.ds(8 + i, nsel, stride=per), :] for i in range(-1, per)]
    imp = rows[1] + rows[0]
    for i in range(1, per):
        imp = imp + (rows[i + 1] + rows[i])
    b_io = lax.broadcasted_iota(jnp.int32, (nsel, BLK), 0)
    b_f = b_io.astype(F32)
    pos = n * BLK + lax.broadcasted_iota(jnp.int32, (nsel, BLK), 1)
    forced = (b_io == pos // SEL_BLK) | (b_io == 0)
    valid = b_io * SEL_BLK <= pos
    imp = jnp.where(forced, SEL_FORCE, jnp.where(valid, imp, -1.0))

    def pick_one(_, carry):
        imp, sel = carry
        m = jnp.max(imp, axis=0, keepdims=True)
        first = jnp.min(jnp.where(imp == m, b_f, float(nsel)), axis=0, keepdims=True)
        hit = b_f == first
        sel = jnp.where(hit & (m >= 0.0), 1.0, sel)
        return jnp.where(hit, -2.0, imp), sel

    _, sel = lax.fori_loop(0, n_top, pick_one, (imp, jnp.zeros((nsel, BLK), F32)))
    sel_scr[...] = jnp.where(sel > 0.0, 0.0, NEG_INF)

    nblk_t = tk // BLK
    no_bias_rows = jnp.zeros((AUG_SEL - HD, r_all), F32)

    def scores(t, const_rows):
        k0 = pl.multiple_of(t * tk, tk)
        picked = sel_scr[pl.ds(pl.multiple_of((AUG_END - AUG_SEL) * t, 8), AUG_END - AUG_SEL), :]
        qa_scr[HD:AUG_END, :] = _mx(jnp.concatenate([const_rows, jnp.concatenate([picked] * A_G, axis=1)], axis=0))
        return _dot(ks_ref[0, pl.ds(k0, tk), :], qa_scr[...])

    kq_row = lax.broadcasted_iota(jnp.int32, (tk, BLK), 0)
    kq_lane = lax.broadcasted_iota(jnp.int32, (tk, BLK), 1)
    last = (n * BLK) // tk

    def tile_scores(t, near):
        if not near:
            return scores(t, bias_rows)
        bias = jnp.concatenate(
            [tile_bias(jnp.clip(n - (nblk_t * t + i), 0, N_OFF - 1)) for i in range(nblk_t)], axis=0)
        causal = jnp.where(t * tk + kq_row <= n * BLK + kq_lane, 0.0, NEG_INF)
        return scores(t, no_bias_rows) + bias + jnp.concatenate([causal] * A_G, axis=1)

    def accumulate(t, s, m_i, acc, m_tile):
        k0 = pl.multiple_of(t * tk, tk)
        m_new = jnp.maximum(m_i, m_tile)
        p = jnp.exp(s - m_new)
        acc = acc * jnp.exp(m_i - m_new) + _dot(vst_ref[0, :, pl.ds(k0, tk)], _mx(p))
        return m_new, acc

    def stage1(t, buf, near):
        s = tile_scores(t, near)
        buf[...] = s
        return jnp.max(s, axis=0, keepdims=True)

    def run_pairs(first, n_pairs, near, m_i, acc):
        def pair(u, carry):
            m_i, acc, m_a = carry
            t0 = first + 2 * u
            m_b = stage1(t0 + 1, sb_scr, near)
            m_i, acc = accumulate(t0, sa_scr[...], m_i, acc, m_a)
            m_a = stage1(jnp.minimum(t0 + 2, last), sa_scr, near)
            m_i, acc = accumulate(t0 + 1, sb_scr[...], m_i, acc, m_b)
            return m_i, acc, m_a

        m_i, acc, _ = lax.fori_loop(0, n_pairs, pair, (m_i, acc, stage1(first, sa_scr, near)))
        return m_i, acc

    def single(t, carry):
        s = tile_scores(t, True)
        return accumulate(t, s, *carry, jnp.max(s, axis=0, keepdims=True))

    far_pairs = jnp.maximum(n - (N_OFF - 2), 0) // (2 * nblk_t)
    near_pairs = (last + 1 - 2 * far_pairs) // 2
    m_i, acc_s = run_pairs(0, far_pairs, False, jnp.full((1, r_all), NEG_INF, F32), jnp.zeros((V_ROWS, r_all), F32))
    m_i, acc_s = run_pairs(2 * far_pairs, near_pairs, True, m_i, acc_s)
    _, acc_s = lax.fori_loop(2 * (far_pairs + near_pairs), last + 1, single, (m_i, acc_s))
    o_s = acc_s[0:HD] / jnp.maximum(acc_s[HD:HD + 1], TINY)

    nwb = A_WINDOW // BLK + 1
    kb0 = jnp.maximum(n - (nwb - 1), 0)
    k0 = pl.multiple_of(kb0 * BLK, BLK)
    s = _dot(kw_ref[0, pl.ds(k0, nwb * BLK), :], q_t)
    bias = jnp.concatenate([tile_bias(jnp.clip(n - kb0 - i, 0, N_OFF - 1)) for i in range(nwb)], axis=0)
    w_row = lax.broadcasted_iota(jnp.int32, (nwb * BLK, BLK), 0)
    w_lane = lax.broadcasted_iota(jnp.int32, (nwb * BLK, BLK), 1)
    dist = (n * BLK + w_lane) - (k0 + w_row)
    in_window = jnp.where((dist >= 0) & (dist < A_WINDOW), 0.0, NEG_INF)
    s = s + bias + jnp.concatenate([in_window] * A_G, axis=1)
    e = jnp.exp(s - jnp.max(s, axis=0, keepdims=True))
    o_w = _dot(vwt_ref[0, :, pl.ds(k0, nwb * BLK)], _mx(e)) / jnp.sum(e, axis=0, keepdims=True)

    gates = jax.nn.sigmoid(g_ref[0, 0])
    o_t = gates[0:1] * o_c + gates[1:2] * o_s + gates[2:3] * o_w
    for pair in range(A_G // 2):
        both = jnp.concatenate([o_t[:, (2 * pair) * BLK:(2 * pair + 1) * BLK],
                                o_t[:, (2 * pair + 1) * BLK:(2 * pair + 2) * BLK]], axis=0)
        o_ref[:, pair * LANES:(pair + 1) * LANES] = both.T


def _nsa(q, kc, vct, ks, vst, kw, vwt, gates, tb_a, bc, seq):
    nb = seq // BLK
    ncmp = seq // CMP_STRIDE
    nsel = seq // SEL_BLK
    tk = SEL_TILE
    full = lambda h, n: (h, 0, 0)
    return pl.pallas_call(
        functools.partial(_nsa_kernel, seq=seq, tk=tk),
        grid=(A_KV, nb),
        in_specs=[pl.BlockSpec((1, 1, HD, A_G * BLK), lambda h, n: (h, n, 0, 0)),
                  pl.BlockSpec((1, ncmp, AUG_K), full),
                  pl.BlockSpec((1, HD, ncmp), full),
                  pl.BlockSpec((1, seq, AUG_K), full),
                  pl.BlockSpec((1, V_ROWS, seq), full),
                  pl.BlockSpec((1, seq, HD), full),
                  pl.BlockSpec((1, HD, seq), full),
                  pl.BlockSpec((1, 1, 3, A_G * BLK), lambda h, n: (h, n, 0, 0)),
                  pl.BlockSpec((A_G, N_OFF, BLK, LANES), lambda h, n: (h, 0, 0, 0)),
                  pl.BlockSpec((A_G, 1, CMP_TAB, LANES), lambda h, n: (h, 0, 0, 0))],
        out_specs=pl.BlockSpec((BLK, A_G * HD), lambda h, n: (n, h)),
        out_shape=jax.ShapeDtypeStruct((seq, A_HEADS * HD), F32),
        scratch_shapes=[pltpu.VMEM((ncmp, A_G * BLK), F32), pltpu.VMEM((nsel, BLK), F32),
                        pltpu.VMEM((AUG_K, A_G * BLK), MXU_DTYPE),
                        pltpu.VMEM((tk, A_G * BLK), F32), pltpu.VMEM((tk, A_G * BLK), F32),
                        pltpu.VMEM((ncmp + 8, BLK), F32)],
        compiler_params=_cparams(("arbitrary", "arbitrary")),
        name="nsa",
    )(q, kc, vct, ks, vst, kw, vwt, gates, tb_a, bc)


def _ssd_kernel(xc_ref, xp_ref, dt_ref, z_ref, cw_ref, cb_ref, dtb_ref, al_ref, d_ref, nw_ref, o_ref,
                xpad_scr, h_scr):
    c = pl.program_id(0)
    L = B_CHUNK

    @pl.when(c == 0)
    def _():
        h_scr[...] = jnp.zeros_like(h_scr)

    xpad_scr[0:8, :] = jnp.where(c > 0, xp_ref[L - 8:L, :], 0.0)
    xpad_scr[8:8 + L, :] = xc_ref[...]
    acc = cw_ref[0:1, :] * xpad_scr[pl.ds(8 - (B_CONV - 1), L), :]
    for i in range(1, B_CONV):
        acc = acc + cw_ref[i:i + 1, :] * xpad_scr[pl.ds(8 - (B_CONV - 1) + i, L), :]
    xa = _silu(acc + cb_ref[...])
    xs = xa[:, :B_INNER]
    bm = xa[:, B_INNER:B_INNER + B_GROUPS * B_STATE]
    cm = xa[:, B_INNER + B_GROUPS * B_STATE:]

    raw = dt_ref[...] + dtb_ref[...]
    dt = jnp.maximum(raw, 0.0) + jnp.log1p(jnp.exp(-jnp.abs(raw)))
    adt = dt * (-jnp.exp(al_ref[...]))
    ri = lax.broadcasted_iota(jnp.int32, (L, L), 0)
    ci = lax.broadcasted_iota(jnp.int32, (L, L), 1)
    tri = _mx(jnp.where(ri >= ci, 1.0, 0.0))
    acs = _dot_f32_lhs_exact(tri, adt)
    acs_t = acs.T
    er = lax.broadcasted_iota(jnp.int32, (LANES, B_INNER), 0)
    ec = lax.broadcasted_iota(jnp.int32, (LANES, B_INNER), 1)
    expand = _mx(jnp.where(ec // B_HEADDIM == er, 1.0, 0.0))
    dt_e = _dot_f32_rhs_exact(dt, expand)
    ea_e = _dot_f32_rhs_exact(jnp.exp(acs), expand)
    we_e = _dot_f32_rhs_exact(jnp.exp(acs[L - 1:L, :] - acs), expand)
    xdt = xs * dt_e
    xw = xdt * we_e
    lane = lax.broadcasted_iota(jnp.int32, (L, LANES), 1)
    hpg = B_HEADS // B_GROUPS
    gw = hpg * B_HEADDIM
    y_groups = []
    for g in range(B_GROUPS):
        b_g = bm[:, g * B_STATE:(g + 1) * B_STATE]
        c_g = _mx(cm[:, g * B_STATE:(g + 1) * B_STATE])
        cb = _dot_nt(c_g, _mx(b_g))
        pairs = []
        for pr in range(hpg // 2):
            x_pair = _mx(xdt[:, g * gw + pr * LANES:g * gw + (pr + 1) * LANES])
            halves = []
            for j in range(2):
                h = g * hpg + 2 * pr + j
                diff = acs[:, h:h + 1] - acs_t[h:h + 1, :]
                decay = jnp.exp(jnp.where(ri >= ci, diff, NEG_INF))
                halves.append(_dot(_mx(cb * decay), x_pair))
            pairs.append(jnp.where(lane < B_HEADDIM, halves[0], halves[1]))
        y_diag = jnp.concatenate(pairs, axis=1)
        h_prev = h_scr[:, g * gw:(g + 1) * gw]
        y_off = _dot(c_g, _mx(h_prev)) * ea_e[:, g * gw:(g + 1) * gw]
        st = _dot(_mx(b_g.T), _mx(xw[:, g * gw:(g + 1) * gw]))
        h_scr[:, g * gw:(g + 1) * gw] = h_prev * ea_e[L - 1:L, g * gw:(g + 1) * gw] + st
        y = y_diag + y_off + xs[:, g * gw:(g + 1) * gw] * d_ref[:, g * gw:(g + 1) * gw]
        yz = y * _silu(z_ref[:, g * gw:(g + 1) * gw])
        ms = jnp.mean(yz * yz, axis=-1, keepdims=True)
        y_groups.append(yz * lax.rsqrt(ms + EPS) * nw_ref[:, g * gw:(g + 1) * gw])
    o_ref[...] = jnp.concatenate(y_groups, axis=1)


def _ssd(proj, conv_w, conv_b, dtb_row, al_row, d_row, nw_row):
    s = proj.shape[0]
    L = B_CHUNK
    cx = _COLS["b_xbc"][0] // B_CONV_DIM
    cdt = _COLS["b_dt"][0] // LANES
    cz = _COLS["b_z"][0] // B_INNER
    row = lambda w: pl.BlockSpec((1, w), lambda c: (0, 0))
    return pl.pallas_call(
        _ssd_kernel,
        grid=(s // L,),
        in_specs=[pl.BlockSpec((L, B_CONV_DIM), lambda c: (c, cx)),
                  pl.BlockSpec((L, B_CONV_DIM), lambda c: (jnp.maximum(c - 1, 0), cx)),
                  pl.BlockSpec((L, LANES), lambda c: (c, cdt)),
                  pl.BlockSpec((L, B_INNER), lambda c: (c, cz)),
                  pl.BlockSpec((B_CONV, B_CONV_DIM), lambda c: (0, 0)),
                  row(B_CONV_DIM), row(LANES), row(LANES), row(B_INNER), row(B_INNER)],
        out_specs=pl.BlockSpec((L, B_INNER), lambda c: (c, 0)),
        out_shape=jax.ShapeDtypeStruct((s, B_INNER), F32),
        scratch_shapes=[pltpu.VMEM((L + 8, B_CONV_DIM), F32), pltpu.VMEM((B_STATE, B_INNER), F32)],
        compiler_params=_cparams(("arbitrary",)),
        name="ssd",
    )(proj, proj, proj, proj, conv_w, conv_b, dtb_row, al_row, d_row, nw_row)


def _rope_kernel(q_ref, k_ref, pos_ref, inv_ref, qo_ref, ko_ref):
    ang = pos_ref[...].astype(F32) * inv_ref[...]
    cos, sin = jnp.cos(ang), jnp.sin(ang)
    lane = lax.broadcasted_iota(jnp.int32, ang.shape, 1)
    low = (lane & (HD - 1)) < HD // 2

    def rot(t):
        partner = jnp.where(low, -pltpu.roll(t, LANES - HD // 2, 1), pltpu.roll(t, HD // 2, 1))
        return t * cos + partner * sin

    for j in range(q_ref.shape[1] // LANES):
        qo_ref[:, j * LANES:(j + 1) * LANES] = rot(q_ref[:, j * LANES:(j + 1) * LANES])
    ko_ref[...] = rot(k_ref[...])


def _rope(proj, pos_col, inv_row):
    s = proj.shape[0]
    tm = min(512, s)
    wq = C_HEADS * HD
    return pl.pallas_call(
        _rope_kernel,
        grid=(s // tm,),
        in_specs=[pl.BlockSpec((tm, wq), lambda i: (i, _COLS["c_q"][0] // wq)),
                  pl.BlockSpec((tm, LANES), lambda i: (i, _COLS["c_k"][0] // LANES)),
                  pl.BlockSpec((tm, 1), lambda i: (i, 0)),
                  pl.BlockSpec((1, LANES), lambda i: (0, 0))],
        out_specs=[pl.BlockSpec((tm, wq), lambda i: (i, 0)), pl.BlockSpec((tm, LANES), lambda i: (i, 0))],
        out_shape=[jax.ShapeDtypeStruct((s, wq), F32), jax.ShapeDtypeStruct((s, LANES), F32)],
        compiler_params=_cparams(("arbitrary",)),
        name="rope",
    )(proj, proj, pos_col, inv_row)


def _band_kernel(*refs, nq, nk, gb, max_dist, has_bias, has_sink, want_lse):
    refs = list(refs)
    q_ref, k_ref, kh_ref, vt_ref, vth_ref = refs[:5]
    pos = 5
    tb_ref = sk_ref = None
    if has_bias:
        tb_ref = refs[pos]
        pos += 1
    if has_sink:
        sk_ref = refs[pos]
        pos += 1
    o_ref = refs[pos]
    lse_ref = refs[pos + 1] if want_lse else None
    i = pl.program_id(1)
    krow = lax.broadcasted_iota(jnp.int32, (2 * BLK, BLK), 0)
    qlane = lax.broadcasted_iota(jnp.int32, (2 * BLK, BLK), 1)
    dist = BLK + qlane - krow
    band = (dist >= 0) & (dist <= max_dist)
    for jb in range(gb):
        outs, lses = [], []
        for a in range(nq):
            kk = a * nk // nq
            if jb == 0:
                k_prev, vt_prev = kh_ref[kk], vth_ref[kk]
                mask = band & (krow >= jnp.where(i > 0, 0, BLK))
            else:
                k_prev = k_ref[kk, (jb - 1) * BLK:jb * BLK, :]
                vt_prev = vt_ref[kk, :, (jb - 1) * BLK:jb * BLK]
                mask = band
            k_win = jnp.concatenate([k_prev, k_ref[kk, jb * BLK:(jb + 1) * BLK, :]], axis=0)
            vt_win = jnp.concatenate([vt_prev, vt_ref[kk, :, jb * BLK:(jb + 1) * BLK]], axis=1)
            s = _dot_nt(k_win, q_ref[a, jb * BLK:(jb + 1) * BLK, :])
            if has_bias:
                s = s + jnp.concatenate([tb_ref[a, 1], tb_ref[a, 0]], axis=0)
            sink = sk_ref[a] if has_sink else None
            p, den, m = _softmax_cols(s, mask, sink)
            outs.append(_dot(vt_win, _mx(p / den)))
            if want_lse:
                lses.append(jnp.broadcast_to(m + jnp.log(den), (HD, BLK)))
        for pr in range(nq // 2):
            o_ref[jb * BLK:(jb + 1) * BLK, pr * LANES:(pr + 1) * LANES] = \
                jnp.concatenate([outs[2 * pr], outs[2 * pr + 1]], axis=0).T
            if want_lse:
                lse_ref[jb * BLK:(jb + 1) * BLK, pr * LANES:(pr + 1) * LANES] = \
                    jnp.concatenate([lses[2 * pr], lses[2 * pr + 1]], axis=0).T


def _band(q, k, vt, *, nq, nk, max_dist, tb=None, sinks=None, want_lse=False):
    n_q, L, _ = q.shape
    gb = min(8, L // BLK)
    steps = n_q // nq
    halo = lambda s, i: (s, jnp.maximum(i * gb - 1, 0), 0)
    halo_t = lambda s, i: (s, 0, jnp.maximum(i * gb - 1, 0))
    in_specs = [pl.BlockSpec((nq, gb * BLK, HD), lambda s, i: (s, i, 0)),
                pl.BlockSpec((nk, gb * BLK, HD), lambda s, i: (s, i, 0)),
                pl.BlockSpec((nk, BLK, HD), halo),
                pl.BlockSpec((nk, HD, gb * BLK), lambda s, i: (s, 0, i)),
                pl.BlockSpec((nk, HD, BLK), halo_t)]
    args = [q, k, k, vt, vt]
    if tb is not None:
        nh = tb.shape[0] // nq
        in_specs.append(pl.BlockSpec((nq, 2, BLK, LANES), lambda s, i: (s % nh, 0, 0, 0)))
        args.append(tb)
    if sinks is not None:
        in_specs.append(pl.BlockSpec((nq, 1, LANES), lambda s, i: (s, 0, 0)))
        args.append(sinks)
    o_spec = pl.BlockSpec((gb * BLK, nq * HD), lambda s, i: (i, s))
    o_shape = jax.ShapeDtypeStruct((L, n_q * HD), F32)
    return pl.pallas_call(
        functools.partial(_band_kernel, nq=nq, nk=nk, gb=gb, max_dist=max_dist, has_bias=tb is not None,
                          has_sink=sinks is not None, want_lse=want_lse),
        grid=(steps, L // (gb * BLK)),
        in_specs=in_specs,
        out_specs=[o_spec, o_spec] if want_lse else o_spec,
        out_shape=[o_shape, o_shape] if want_lse else o_shape,
        compiler_params=_cparams(("arbitrary", "arbitrary")),
        name="band_attention",
    )(*args)


def _memkv_kernel(mem_ref, nw_ref, w_ref, o_ref):
    x = mem_ref[...]
    y = x * lax.rsqrt(jnp.mean(x * x, axis=-1, keepdims=True) + EPS) * nw_ref[...]
    o_ref[...] = _dot(_mx(y), w_ref[...])


def _memkv(mem2, norm_w, w):
    return pl.pallas_call(
        _memkv_kernel,
        out_shape=jax.ShapeDtypeStruct((mem2.shape[0], w.shape[1]), F32),
        compiler_params=pltpu.CompilerParams(vmem_limit_bytes=VMEM_LIMIT),
        name="mem_kv",
    )(mem2, norm_w, w)


def _memattn_kernel(q_ref, k_ref, vt_ref, o_ref):
    outs = []
    for h in range(M_HEADS):
        s = _dot_nt(k_ref[h], q_ref[h])
        m = jnp.max(s, axis=0, keepdims=True)
        p = jnp.exp(s - m)
        p = p / jnp.sum(p, axis=0, keepdims=True)
        outs.append(_dot(vt_ref[h], _mx(p)))
    for pr in range(M_HEADS // 2):
        o_ref[:, pr * LANES:(pr + 1) * LANES] = jnp.concatenate([outs[2 * pr], outs[2 * pr + 1]], axis=0).T


def _memattn(q, k, vt):
    _, s, _ = q.shape
    tq = min(512, s)
    ml = k.shape[1]
    return pl.pallas_call(
        _memattn_kernel,
        grid=(s // tq,),
        in_specs=[pl.BlockSpec((M_HEADS, tq, HD), lambda i: (0, i, 0)),
                  pl.BlockSpec((M_HEADS, ml, HD), lambda i: (0, 0, 0)),
                  pl.BlockSpec((M_HEADS, HD, ml), lambda i: (0, 0, 0))],
        out_specs=pl.BlockSpec((tq, M_HEADS * HD), lambda i: (i, 0)),
        out_shape=jax.ShapeDtypeStruct((s, M_HEADS * HD), F32),
        compiler_params=_cparams(("arbitrary",)),
        name="mem_attention",
    )(q, k, vt)


def _outproj_kernel(x_ref, a_ref, az_ref, b_ref, c_ref, cz_ref, d0_ref, d1_ref, d2_ref, l0_ref, l1_ref, l2_ref,
                    dz_ref, m_ref, mz_ref, w_ref, nw_ref, o_ref):
    l0, l1, l2 = l0_ref[...], l1_ref[...], l2_ref[...]
    mx = jnp.maximum(jnp.maximum(l0, l1), l2)
    e0, e1, e2 = jnp.exp(l0 - mx), jnp.exp(l1 - mx), jnp.exp(l2 - mx)
    inv = 1.0 / (e0 + e1 + e2)
    d = (e0 * inv) * d0_ref[...] + (e1 * inv) * d1_ref[...] + (e2 * inv) * d2_ref[...]
    pieces = (a_ref[...] * _silu(az_ref[...]), b_ref[...], c_ref[...] * _silu(cz_ref[...]),
              d * _silu(dz_ref[...]), m_ref[...] * _silu(mz_ref[...]))
    y = None
    row = 0
    for piece in pieces:
        w = piece.shape[1]
        part = _dot(_mx(piece), w_ref[row:row + w, :])
        y = part if y is None else y + part
        row += w
    y = y * lax.rsqrt(jnp.mean(y * y, axis=-1, keepdims=True) + EPS) * nw_ref[...]
    o_ref[...] = x_ref[...] + y


def _outproj(x2, proj, a_out, b_out, c_out, d_outs, d_lses, m_out, w, norm_w):
    s = x2.shape[0]
    tm = min(512, s)
    wide = lambda width, col: pl.BlockSpec((tm, width), lambda i: (i, col))
    w512 = A_HEADS * HD
    wm = M_HEADS * HD
    return pl.pallas_call(
        _outproj_kernel,
        grid=(s // tm,),
        in_specs=[wide(D_MODEL, 0), wide(w512, 0), wide(w512, _COLS["a_z"][0] // w512), wide(w512, 0),
                  wide(w512, 0), wide(w512, _COLS["c_z"][0] // w512),
                  wide(w512, 0), wide(w512, 0), wide(w512, 0), wide(w512, 0), wide(w512, 0), wide(w512, 0),
                  wide(w512, _COLS["d_z"][0] // w512), wide(wm, 0), wide(wm, _COLS["m_z"][0] // wm),
                  pl.BlockSpec((MIX_WIDTH, D_MODEL), lambda i: (0, 0)),
                  pl.BlockSpec((1, D_MODEL), lambda i: (0, 0))],
        out_specs=wide(D_MODEL, 0),
        out_shape=jax.ShapeDtypeStruct((s, D_MODEL), F32),
        compiler_params=_cparams(("arbitrary",)),
        name="outproj",
    )(x2, a_out, proj, b_out, c_out, proj, *d_outs, *d_lses, proj, m_out, proj, w, norm_w)


def _repack_w_in(w):
    out = []
    for name, (_, width) in _COLS.items():
        src, true_w = _SRC[name]
        piece = w[:, src:src + true_w]
        if true_w < width:
            piece = jnp.pad(piece, ((0, 0), (0, width - true_w)))
        out.append(piece)
    return jnp.concatenate(out, axis=1)


def _heads_first(t, nh):
    s = t.shape[0]
    return jnp.transpose(t.reshape(s, nh, HD), (1, 0, 2))


def _heads_first_t(t, nh):
    s = t.shape[0]
    return jnp.transpose(t.reshape(s, nh, HD), (1, 2, 0))


def _col(proj, name):
    off, width = _COLS[name]
    return proj[:, off:off + width]


def kernel(x, mem, positions, pre_norm, post_norm, w_in, w_out, rel_bias, a_cmp_pos, a_cmp_w1, a_cmp_w2,
           b_conv_w, b_conv_b, b_dt_bias, b_a_log, b_d, b_norm, c_sinks, m_norm, m_w_kv):
    b, s, _ = x.shape
    assert b == 1 and s % 2048 == 0, "sequence length must be a multiple of 2048 and batch 1"
    depth = w_in.shape[0]
    nb = s // BLK
    ncmp = s // CMP_STRIDE
    nsel = s // SEL_BLK
    scale = HD ** -0.5

    rel_a = rel_bias[:, :A_HEADS]
    rel_d = rel_bias[:, A_HEADS:]
    offs = jnp.arange(N_OFF, dtype=jnp.int32) * BLK
    tb_a = _bias_table(rel_a, offs, a_row=-1, b_lane=1, scale=1, rows=BLK)
    bc_a = _bias_table(rel_a, jnp.array([CMP_STRIDE * CMP_NEAR - (CMP_LEN - 1)], jnp.int32),
                       a_row=-CMP_STRIDE, b_lane=1, scale=1, rows=CMP_TAB)
    tb_d = [_bias_table(rel_d[:, p * D_SLOTS:(p + 1) * D_SLOTS], offs[:2], a_row=-1, b_lane=1, scale=dil, rows=BLK)
            for p, (_, dil) in enumerate(D_PATTERNS)]
    key_pos = np.arange(s)
    key_aug_np = np.zeros((s, AUG_K - HD), np.float32)
    key_aug_np[:, 0:2] = 1.0
    key_aug_np[key_pos, AUG_SEL - HD + (key_pos // SEL_BLK) % (SEL_TILE // SEL_BLK)] = 1.0
    key_aug = jnp.asarray(key_aug_np, MXU_DTYPE)[None]
    val_aug_np = np.zeros((V_ROWS - HD, s), np.float32)
    val_aug_np[0] = 1.0
    val_aug = jnp.asarray(val_aug_np, MXU_DTYPE)[None]
    half = ROPE_THETA ** (-jnp.arange(HD // 2, dtype=F32) / (HD // 2))
    inv_row = jnp.tile(half, LANES // (HD // 2)).reshape(1, LANES)
    pos_col = positions.reshape(s, 1)
    mem2 = mem.reshape(mem.shape[1], D_MODEL)

    x2 = x.reshape(s, D_MODEL)
    for layer in range(depth):
        proj = _inproj(x2, pre_norm[layer].reshape(1, D_MODEL), _mx(_repack_w_in(w_in[layer])))

        a_qt = _mx(jnp.transpose((_col(proj, "a_q") * scale).reshape(nb, BLK, A_KV, A_G, HD),
                                 (2, 0, 4, 3, 1)).reshape(A_KV, nb, HD, A_G * BLK))
        a_kv = _col(proj, "a_kv")
        part = lambda i: a_kv[:, i * A_KV * HD:(i + 1) * A_KV * HD]
        u = jnp.concatenate([_heads_first(part(0), A_KV), _heads_first(part(1), A_KV)], axis=0)
        u = jnp.pad(u.reshape(2 * A_KV, ncmp, CMP_STRIDE * HD), ((0, 0), (0, 8), (0, 0)))
        cmp_out = _compress(u, a_cmp_pos[layer].reshape(2, 2, CMP_STRIDE * HD),
                            _mx(a_cmp_w1[layer].reshape(2, 2, CMP_STRIDE * HD, CMP_HIDDEN)),
                            _mx(a_cmp_w2[layer]), ncmp)
        kc = jnp.concatenate([_mx(cmp_out[:A_KV]), jnp.broadcast_to(key_aug[:, :ncmp, :2], (A_KV, ncmp, 2)),
                              jnp.zeros((A_KV, ncmp, AUG_K - HD - 2), MXU_DTYPE)], axis=2)
        vct = _mx(jnp.transpose(cmp_out[A_KV:], (0, 2, 1)))
        gates = jnp.transpose(_col(proj, "a_gate")[:, :3 * A_HEADS].reshape(nb, BLK, A_KV, A_G, 3),
                              (2, 0, 4, 3, 1)).reshape(A_KV, nb, 3, A_G * BLK)
        ks_aug = jnp.concatenate([_mx(_heads_first(part(2), A_KV)),
                                  jnp.broadcast_to(key_aug, (A_KV, s, AUG_K - HD))], axis=2)
        vst_aug = jnp.concatenate([_mx(_heads_first_t(part(3), A_KV)),
                                   jnp.broadcast_to(val_aug, (A_KV, V_ROWS - HD, s))], axis=1)
        a_out = _nsa(a_qt, kc, vct, ks_aug, vst_aug, _mx(_heads_first(part(4), A_KV)),
                     _mx(_heads_first_t(part(5), A_KV)), gates, tb_a, bc_a, s)

        pad_row = lambda v: jnp.pad(v, (0, LANES - v.shape[0])).reshape(1, LANES)
        b_out = _ssd(proj, b_conv_w[layer], b_conv_b[layer].reshape(1, B_CONV_DIM), pad_row(b_dt_bias[layer]),
                     pad_row(b_a_log[layer]), jnp.repeat(b_d[layer], B_HEADDIM).reshape(1, B_INNER),
                     b_norm[layer].reshape(1, B_INNER))

        cq_rot, ck_rot = _rope(proj, pos_col, inv_row)
        sinks = jnp.broadcast_to(c_sinks[layer].reshape(C_HEADS, 1, 1), (C_HEADS, 1, LANES))
        c_out = _band(_mx(_heads_first(cq_rot * scale, C_HEADS)), _mx(_heads_first(ck_rot, C_KV)),
                      _mx(_heads_first_t(_col(proj, "c_v"), C_KV)), nq=C_HEADS // C_KV, nk=1,
                      max_dist=C_WINDOW - 1, sinks=sinks)

        d_q = _col(proj, "d_q")
        d_k = _col(proj, "d_k")
        d_v = _col(proj, "d_v")
        d_outs, d_lses = [], []
        for p, (window, dil) in enumerate(D_PATTERNS):
            L = s // dil
            to_seq = lambda t: jnp.transpose(t.reshape(L, dil * D_SLOTS, HD), (1, 0, 2))
            q_p = d_q[:, p * D_SLOTS * HD:(p + 1) * D_SLOTS * HD] * scale
            o_p, lse_p = _band(_mx(to_seq(q_p)), _mx(to_seq(d_k)), _mx(jnp.transpose(to_seq(d_v), (0, 2, 1))),
                               nq=2, nk=2, max_dist=window // dil, tb=tb_d[p], want_lse=True)
            d_outs.append(o_p.reshape(s, D_SLOTS * HD))
            d_lses.append(lse_p.reshape(s, D_SLOTS * HD))

        kv = _memkv(mem2, m_norm[layer].reshape(1, D_MODEL), _mx(m_w_kv[layer]))
        m_out = _memattn(_mx(_heads_first(_col(proj, "m_q") * scale, M_HEADS)),
                         _mx(_heads_first(kv[:, :M_HEADS * HD], M_HEADS)),
                         _mx(_heads_first_t(kv[:, M_HEADS * HD:], M_HEADS)))

        x2 = _outproj(x2, proj, a_out, b_out, c_out, d_outs, d_lses, m_out, _mx(w_out[layer]),
                      post_norm[layer].reshape(1, D_MODEL))
    return x2.reshape(b, s, D_MODEL)
```

```python
import functools
import math

import numpy as np
import jax
import jax.numpy as jnp
from jax import lax
from jax.experimental import pallas as pl
from jax.experimental.pallas import tpu as pltpu

F32 = jnp.float32
MXU_DTYPE = jnp.bfloat16

D_MODEL = 1024
HD = 64
BLK = 128
NEG_INF = -1e30
TINY = 1e-30
EPS = 1e-6

A_HEADS, A_KV = 8, 2
A_G = A_HEADS // A_KV
CMP_LEN, CMP_STRIDE, CMP_HIDDEN = 32, 16, 128
SEL_BLK, SEL_TOPK, A_WINDOW, SEL_FORCE = 64, 16, 512, 1e4
B_HEADS, B_HEADDIM, B_GROUPS, B_STATE, B_CONV, B_CHUNK = 8, 64, 2, 128, 4, 128
B_INNER = B_HEADS * B_HEADDIM
B_CONV_DIM = B_INNER + 2 * B_GROUPS * B_STATE
C_HEADS, C_KV, C_WINDOW = 8, 2, 128
ROPE_THETA = 150000.0
D_SLOTS = 8
D_PATTERNS = ((128, 1), (512, 4), (2048, 16))
D_NPAT = 3
M_HEADS = 4
N_BUCKETS, MAX_DIST = 32, 2048
MIX_WIDTH = A_HEADS * HD + B_INNER + C_HEADS * HD + D_SLOTS * HD + M_HEADS * HD

LANES = 128
VMEM_LIMIT = 56 * 1024 * 1024

_SRC = {"a_q": (0, 512, True), "a_kc": (512, 128, False), "a_vc": (640, 128, False), "a_ks": (768, 128, False),
        "a_vs": (896, 128, False), "a_kw": (1024, 128, False), "a_vw": (1152, 128, False),
        "a_gate": (1280, 24, False), "a_z": (1304, 512, False), "b_xbc": (1816, 1024, False),
        "b_dt": (2840, 8, False), "b_z": (2848, 512, False), "c_q": (3360, 512, True), "c_k": (3872, 128, False),
        "c_v": (4000, 128, False), "c_z": (4128, 512, False), "d_q": (4640, 1536, True), "d_k": (6176, 512, False),
        "d_v": (6688, 512, False), "d_z": (7200, 512, False), "m_q": (7712, 256, True), "m_z": (7968, 256, False)}


def _layout(pieces):
    cols, off = {}, 0
    for name, width in pieces:
        cols[name] = (off, width)
        off += width
    return cols, off


_FCOLS, N_F32 = _layout((("b_xbc", 1024), ("a_z", 512), ("b_z", 512), ("c_z", 512), ("d_z", 512), ("c_q", 512),
                         ("d_q", 1536), ("d_k", 512), ("d_v", 512), ("m_z", 256), ("a_kc", 128), ("a_vc", 128),
                         ("c_k", 128), ("a_gate", 128), ("b_dt", 128)))
_BCOLS, N_BF16 = _layout((("a_q", 512), ("m_q", 256), ("a_ks", 128), ("a_vs", 128), ("a_kw", 128),
                          ("a_vw", 128), ("c_v", 128)))
PROJ_TN = 1408
_C_ORDER = tuple(h for j in range(C_HEADS // C_KV) for h in (j, j + C_HEADS // C_KV))


def _t5_thresholds():
    d = np.arange(0, 4 * MAX_DIST)
    exact = N_BUCKETS // 2
    rel = np.maximum(d, exact).astype(np.float64)
    large = exact + (np.log(rel / exact) / math.log(MAX_DIST / exact) * (N_BUCKETS - exact)).astype(np.int64)
    bucket = np.where(d < exact, d, np.minimum(large, N_BUCKETS - 1))
    return tuple(int(np.argmax(bucket >= b)) for b in range(1, N_BUCKETS))


_THR = _t5_thresholds()
FAR_DIST = _THR[-1]
N_OFF = -(-(FAR_DIST + BLK) // BLK) + 1
CMP_REACH = -(-FAR_DIST // CMP_STRIDE) // 8 * 8 + 8
CMP_WIN = 128
CMP_NEAR = 128
CMP_TAB = CMP_NEAR + CMP_WIN
SEL_TILE = 512
AUG_BASE = LANES
AUG_SEL = AUG_BASE + 8
AUG_END = AUG_SEL + SEL_TILE // SEL_BLK
AUG_K = 2 * LANES
V_ROWS = HD + 16


def _cparams(sem):
    return pltpu.CompilerParams(dimension_semantics=sem, vmem_limit_bytes=VMEM_LIMIT)


def _mx(x):
    return x.astype(MXU_DTYPE)


def _dot(a, b):
    return jnp.dot(a, b, preferred_element_type=F32)


def _dot_nt(a, b):
    return lax.dot_general(a, b, (((1,), (1,)), ((), ())), preferred_element_type=F32)


def _split3(a):
    hi = _mx(a)
    r1 = a - hi.astype(F32)
    mid = _mx(r1)
    lo = _mx(r1 - mid.astype(F32))
    return hi, mid, lo


def _dot_f32_rhs_exact(a, b):
    hi, mid, lo = _split3(a)
    return _dot(hi, b) + _dot(mid, b) + _dot(lo, b)


def _dot_f32_lhs_exact(a, b):
    hi, mid, lo = _split3(b)
    return _dot(a, hi) + _dot(a, mid) + _dot(a, lo)


def _silu(x):
    return x * jax.nn.sigmoid(x)


def _softmax_cols(s, mask, sink=None):
    s = jnp.where(mask, s, NEG_INF)
    m = jnp.max(s, axis=0, keepdims=True)
    if sink is not None:
        m = jnp.maximum(m, sink)
    p = jnp.where(mask, jnp.exp(s - m), 0.0)
    den = jnp.sum(p, axis=0, keepdims=True)
    if sink is not None:
        den = den + jnp.exp(sink - m)
    return p, den, m


def _bias_table_kernel(rel_ref, c0_ref, o_ref, *, a_row, b_lane, scale, rows):
    h = pl.program_id(0)
    t = pl.program_id(1)
    r = lax.broadcasted_iota(jnp.int32, (rows, LANES), 0)
    l = lax.broadcasted_iota(jnp.int32, (rows, LANES), 1)
    dist = (c0_ref[t] + a_row * r + b_lane * l) * scale
    out = jnp.full((rows, LANES), rel_ref[0, h], F32)
    for b in range(1, N_BUCKETS):
        out = jnp.where(dist >= _THR[b - 1], rel_ref[b, h], out)
    o_ref[0, 0] = out


def _bias_table(rel_cols, c0, *, a_row, b_lane, scale, rows):
    nh = rel_cols.shape[1]
    nt = c0.shape[0]
    return pl.pallas_call(
        functools.partial(_bias_table_kernel, a_row=a_row, b_lane=b_lane, scale=scale, rows=rows),
        grid_spec=pltpu.PrefetchScalarGridSpec(
            num_scalar_prefetch=2, grid=(nh, nt), in_specs=[],
            out_specs=pl.BlockSpec((1, 1, rows, LANES), lambda h, t, *_: (h, t, 0, 0))),
        out_shape=jax.ShapeDtypeStruct((nh, nt, rows, LANES), F32),
        compiler_params=_cparams(("arbitrary", "arbitrary")),
        name="bias_table",
    )(rel_cols, c0)


def _inproj_kernel(x_ref, nw_ref, w_ref, o_ref, h_scr):
    @pl.when(pl.program_id(1) == 0)
    def _():
        x = x_ref[...]
        y = x * lax.rsqrt(jnp.mean(x * x, axis=-1, keepdims=True) + EPS)
        h_scr[...] = _mx(y * nw_ref[...])

    o_ref[...] = _dot(h_scr[...], w_ref[...]).astype(o_ref.dtype)


def _inproj(x2, norm_w, w, out_dtype):
    s = x2.shape[0]
    tm, tn = min(1024, s), PROJ_TN
    return pl.pallas_call(
        _inproj_kernel,
        grid=(s // tm, w.shape[1] // tn),
        in_specs=[pl.BlockSpec((tm, D_MODEL), lambda i, j: (i, 0)),
                  pl.BlockSpec((1, D_MODEL), lambda i, j: (0, 0)),
                  pl.BlockSpec((D_MODEL, tn), lambda i, j: (0, j))],
        out_specs=pl.BlockSpec((tm, tn), lambda i, j: (i, j)),
        out_shape=jax.ShapeDtypeStruct((s, w.shape[1]), out_dtype),
        scratch_shapes=[pltpu.VMEM((tm, D_MODEL), MXU_DTYPE)],
        compiler_params=_cparams(("arbitrary", "arbitrary")),
        name="inproj",
    )(x2, norm_w, w)


def _compress_kernel(u_ref, pe_ref, w1_ref, w2_ref, o_ref, h2_scr, *, ncmp):
    u = u_ref[0]
    pe_first, pe_second = pe_ref[0, 0:1, :], pe_ref[0, 1:2, :]
    h1 = _dot(_mx(u + pe_first), w1_ref[0, 0])
    h2_scr[0:ncmp, :] = _dot(_mx(u + pe_second), w1_ref[0, 1])
    h2_scr[ncmp:ncmp + 8, :] = _dot(_mx(jnp.broadcast_to(pe_second, (8, pe_second.shape[1]))), w1_ref[0, 1])
    pre = h1 + h2_scr[pl.ds(1, ncmp), :]
    o_ref[0] = _dot(_mx(jax.nn.gelu(pre)), w2_ref[0])


def _compress(u, pe, w1, w2, ncmp):
    wide = CMP_STRIDE * A_KV * HD
    return pl.pallas_call(
        functools.partial(_compress_kernel, ncmp=ncmp),
        grid=(2 * A_KV,),
        in_specs=[pl.BlockSpec((1, ncmp, wide), lambda i: (i // A_KV, 0, 0)),
                  pl.BlockSpec((1, 2, wide), lambda i: (i, 0, 0)),
                  pl.BlockSpec((1, 2, wide, CMP_HIDDEN), lambda i: (i, 0, 0, 0)),
                  pl.BlockSpec((1, CMP_HIDDEN, HD), lambda i: (i // A_KV, 0, 0))],
        out_specs=pl.BlockSpec((1, ncmp, HD), lambda i: (i, 0, 0)),
        out_shape=jax.ShapeDtypeStruct((2 * A_KV, ncmp, HD), F32),
        scratch_shapes=[pltpu.VMEM((ncmp + 8, CMP_HIDDEN), F32)],
        compiler_params=_cparams(("arbitrary",)),
        name="nsa_compress",
    )(u, pe, w1, w2)


def _nsa_kernel(q_ref, kc_ref, vct_ref, ks_ref, ka_ref, vst_ref, kw_ref, vwt_ref, g_ref, tb_ref, bc_ref,
                o_ref, s_scr, sel_scr, qa_scr, sa_scr, sb_scr, pcs_scr, *, seq, tk):
    hk = pl.program_id(0)
    n = pl.program_id(1)
    ncmp = seq // CMP_STRIDE
    nsel = seq // SEL_BLK
    n_top = min(SEL_TOPK, nsel)
    r_all = A_G * BLK
    q_rows = q_ref[...].astype(F32)
    halves = [q_rows[:, p * LANES:(p + 1) * LANES].T for p in range(A_G // 2)]
    q_t = _mx(jnp.concatenate([h[r * HD:(r + 1) * HD] for h in halves for r in range(2)], axis=1))
    far_row = jnp.concatenate([tb_ref[g, N_OFF - 1, 0:1, :] for g in range(A_G)], axis=1)

    def tile_bias(o):
        return jnp.concatenate([tb_ref[g, o] for g in range(A_G)], axis=1)

    far_hi = _mx(far_row).astype(F32)
    bias_rows = jnp.concatenate([far_hi, far_row - far_hi, jnp.zeros((AUG_SEL - AUG_BASE - 2, r_all), F32)], axis=0)
    qa_scr[0:AUG_BASE, :] = jnp.zeros((AUG_BASE, r_all), MXU_DTYPE)
    qa_scr[pl.ds(pl.multiple_of(hk * HD, HD), HD), :] = q_t
    qa_scr[AUG_BASE:AUG_END, :] = _mx(
        jnp.concatenate([bias_rows, jnp.zeros((AUG_END - AUG_SEL, r_all), F32)], axis=0))
    qa_scr[AUG_END:, :] = jnp.zeros((AUG_K - AUG_END, r_all), MXU_DTYPE)
    q_own = qa_scr[0:AUG_BASE, :]

    s_scr[...] = _dot(kc_ref[0], qa_scr[...])
    st = jnp.clip((8 * n - CMP_REACH) // 16 * 16, 0, ncmp - CMP_WIN)
    off = pl.multiple_of(st - 8 * n + CMP_NEAR, 8)
    st = pl.multiple_of(st, 16)
    near = jnp.concatenate([bc_ref[g, 0, pl.ds(off, CMP_WIN), :] for g in range(A_G)], axis=1)
    s_scr[pl.ds(st, CMP_WIN), :] = _dot(kc_ref[0, pl.ds(st, CMP_WIN), 0:AUG_BASE], q_own) + near
    jrow = lax.broadcasted_iota(jnp.int32, (ncmp, BLK), 0)
    qpos = n * BLK + lax.broadcasted_iota(jnp.int32, (ncmp, BLK), 1)
    visible = jnp.where(jrow * CMP_STRIDE + (CMP_LEN - 1) <= qpos, 0.0, NEG_INF)
    s_c = s_scr[...] + jnp.concatenate([visible] * A_G, axis=1)
    m = jnp.max(s_c, axis=0, keepdims=True)
    e = jnp.exp(s_c - m)
    den = jnp.sum(e, axis=0, keepdims=True)
    inv = jnp.where(m > 0.5 * NEG_INF, 1.0 / jnp.maximum(den, TINY), 0.0)
    p_c = e * inv
    o_c = _dot(vct_ref[0], _mx(p_c))

    pcs = p_c[:, 0:BLK]
    for g in range(1, A_G):
        pcs = pcs + p_c[:, g * BLK:(g + 1) * BLK]
    per = SEL_BLK // CMP_STRIDE
    pcs_scr[0:8, :] = jnp.zeros((8, BLK), F32)
    pcs_scr[8:8 + ncmp, :] = pcs
    rows = [pcs_scr[pl.ds(8 + i, nsel, stride=per), :] for i in range(-1, per)]
    imp = rows[1] + rows[0]
    for i in range(1, per):
        imp = imp + (rows[i + 1] + rows[i])
    b_io = lax.broadcasted_iota(jnp.int32, (nsel, BLK), 0)
    b_f = b_io.astype(F32)
    pos = n * BLK + lax.broadcasted_iota(jnp.int32, (nsel, BLK), 1)
    forced = (b_io == pos // SEL_BLK) | (b_io == 0)
    valid = b_io * SEL_BLK <= pos
    imp = jnp.where(forced, SEL_FORCE, jnp.where(valid, imp, -1.0))

    def pick_one(_, carry):
        imp, sel = carry
        m = jnp.max(imp, axis=0, keepdims=True)
        first = jnp.min(jnp.where(imp == m, b_f, float(nsel)), axis=0, keepdims=True)
        hit = b_f == first
        sel = jnp.where(hit & (m >= 0.0), 1.0, sel)
        return jnp.where(hit, -2.0, imp), sel

    _, sel = lax.fori_loop(0, n_top, pick_one, (imp, jnp.zeros((nsel, BLK), F32)))
    sel_scr[...] = jnp.where(sel > 0.0, 0.0, NEG_INF)

    nblk_t = tk // BLK
    no_bias_rows = jnp.zeros((AUG_SEL - AUG_BASE, r_all), F32)

    def scores(t, const_rows):
        k0 = pl.multiple_of(t * tk, tk)
        picked = sel_scr[pl.ds(pl.multiple_of((AUG_END - AUG_SEL) * t, 8), AUG_END - AUG_SEL), :]
        qa_scr[AUG_BASE:AUG_END, :] = _mx(
            jnp.concatenate([const_rows, jnp.concatenate([picked] * A_G, axis=1)], axis=0))
        keys = jnp.concatenate([ks_ref[pl.ds(k0, tk), :], ka_ref[pl.ds(k0, tk), :]], axis=1)
        return _dot(keys, qa_scr[...])

    kq_row = lax.broadcasted_iota(jnp.int32, (tk, BLK), 0)
    kq_lane = lax.broadcasted_iota(jnp.int32, (tk, BLK), 1)
    last = (n * BLK) // tk

    def tile_scores(t, near):
        if not near:
            return scores(t, bias_rows)
        bias = jnp.concatenate(
            [tile_bias(jnp.clip(n - (nblk_t * t + i), 0, N_OFF - 1)) for i in range(nblk_t)], axis=0)
        causal = jnp.where(t * tk + kq_row <= n * BLK + kq_lane, 0.0, NEG_INF)
        return scores(t, no_bias_rows) + bias + jnp.concatenate([causal] * A_G, axis=1)

    def accumulate(t, s, m_i, acc, m_tile):
        k0 = pl.multiple_of(t * tk, tk)
        m_new = jnp.maximum(m_i, m_tile)
        p = jnp.exp(s - m_new)
        acc = acc * jnp.exp(m_i - m_new) + _dot(vst_ref[0, :, pl.ds(k0, tk)], _mx(p))
        return m_new, acc

    def stage1(t, buf, near):
        s = tile_scores(t, near)
        buf[...] = s
        return jnp.max(s, axis=0, keepdims=True)

    def run_pairs(first, n_pairs, near, m_i, acc):
        def pair(u, carry):
            m_i, acc, m_a = carry
            t0 = first + 2 * u
            m_b = stage1(t0 + 1, sb_scr, near)
            m_i, acc = accumulate(t0, sa_scr[...], m_i, acc, m_a)
            m_a = stage1(jnp.minimum(t0 + 2, last), sa_scr, near)
            m_i, acc = accumulate(t0 + 1, sb_scr[...], m_i, acc, m_b)
            return m_i, acc, m_a

        m_i, acc, _ = lax.fori_loop(0, n_pairs, pair, (m_i, acc, stage1(first, sa_scr, near)))
        return m_i, acc

    def single(t, carry):
        s = tile_scores(t, True)
        return accumulate(t, s, *carry, jnp.max(s, axis=0, keepdims=True))

    far_pairs = jnp.maximum(n - (N_OFF - 2), 0) // (2 * nblk_t)
    near_pairs = (last + 1 - 2 * far_pairs) // 2
    m_i, acc_s = run_pairs(0, far_pairs, False, jnp.full((1, r_all), NEG_INF, F32), jnp.zeros((V_ROWS, r_all), F32))
    m_i, acc_s = run_pairs(2 * far_pairs, near_pairs, True, m_i, acc_s)
    _, acc_s = lax.fori_loop(2 * (far_pairs + near_pairs), last + 1, single, (m_i, acc_s))
    o_s = acc_s[0:HD] / jnp.maximum(acc_s[HD:HD + 1], TINY)

    nwb = A_WINDOW // BLK + 1
    kb0 = jnp.maximum(n - (nwb - 1), 0)
    k0 = pl.multiple_of(kb0 * BLK, BLK)
    s = _dot(kw_ref[pl.ds(k0, nwb * BLK), :], q_own)
    bias = jnp.concatenate([tile_bias(jnp.clip(n - kb0 - i, 0, N_OFF - 1)) for i in range(nwb)], axis=0)
    w_row = lax.broadcasted_iota(jnp.int32, (nwb * BLK, BLK), 0)
    w_lane = lax.broadcasted_iota(jnp.int32, (nwb * BLK, BLK), 1)
    dist = (n * BLK + w_lane) - (k0 + w_row)
    in_window = jnp.where((dist >= 0) & (dist < A_WINDOW), 0.0, NEG_INF)
    s = s + bias + jnp.concatenate([in_window] * A_G, axis=1)
    e = jnp.exp(s - jnp.max(s, axis=0, keepdims=True))
    o_w = _dot(vwt_ref[0, :, pl.ds(k0, nwb * BLK)], _mx(e)) / jnp.sum(e, axis=0, keepdims=True)

    gates = jax.nn.sigmoid(g_ref[0, 0])
    o_t = gates[0:1] * o_c + gates[1:2] * o_s + gates[2:3] * o_w
    for pair in range(A_G // 2):
        both = jnp.concatenate([o_t[:, (2 * pair) * BLK:(2 * pair + 1) * BLK],
                                o_t[:, (2 * pair + 1) * BLK:(2 * pair + 2) * BLK]], axis=0)
        o_ref[:, pair * LANES:(pair + 1) * LANES] = both.T


def _nsa(proj_b, kc, vct, key_aug, vst, vwt, gates, tb_a, bc, seq):
    nb = seq // BLK
    ncmp = seq // CMP_STRIDE
    nsel = seq // SEL_BLK
    tk = SEL_TILE
    full = lambda h, n: (h, 0, 0)
    wq = A_G * HD
    resident = lambda name: pl.BlockSpec((seq, LANES), lambda h, n: (0, _BCOLS[name][0] // LANES))
    return pl.pallas_call(
        functools.partial(_nsa_kernel, seq=seq, tk=tk),
        grid=(A_KV, nb),
        in_specs=[pl.BlockSpec((BLK, wq), lambda h, n: (n, _BCOLS["a_q"][0] // wq + h)),
                  pl.BlockSpec((1, ncmp, AUG_K), full),
                  pl.BlockSpec((1, HD, ncmp), full),
                  resident("a_ks"),
                  pl.BlockSpec((seq, LANES), lambda h, n: (0, 0)),
                  pl.BlockSpec((1, V_ROWS, seq), full),
                  resident("a_kw"),
                  pl.BlockSpec((1, HD, seq), full),
                  pl.BlockSpec((1, 1, 3, A_G * BLK), lambda h, n: (h, n, 0, 0)),
                  pl.BlockSpec((A_G, N_OFF, BLK, LANES), lambda h, n: (h, 0, 0, 0)),
                  pl.BlockSpec((A_G, 1, CMP_TAB, LANES), lambda h, n: (h, 0, 0, 0))],
        out_specs=pl.BlockSpec((BLK, A_G * HD), lambda h, n: (n, h)),
        out_shape=jax.ShapeDtypeStruct((seq, A_HEADS * HD), F32),
        scratch_shapes=[pltpu.VMEM((ncmp, A_G * BLK), F32), pltpu.VMEM((nsel, BLK), F32),
                        pltpu.VMEM((AUG_K, A_G * BLK), MXU_DTYPE),
                        pltpu.VMEM((tk, A_G * BLK), F32), pltpu.VMEM((tk, A_G * BLK), F32),
                        pltpu.VMEM((ncmp + 8, BLK), F32)],
        compiler_params=_cparams(("arbitrary", "arbitrary")),
        name="nsa",
    )(proj_b, kc, vct, proj_b, key_aug, vst, proj_b, vwt, gates, tb_a, bc)


def _ssd_kernel(xc_ref, xp_ref, dt_ref, z_ref, cw_ref, cb_ref, dtb_ref, al_ref, d_ref, nw_ref, o_ref,
                xpad_scr, h_scr):
    c = pl.program_id(0)
    L = B_CHUNK

    @pl.when(c == 0)
    def _():
        h_scr[...] = jnp.zeros_like(h_scr)

    xpad_scr[0:8, :] = jnp.where(c > 0, xp_ref[L - 8:L, :], 0.0)
    xpad_scr[8:8 + L, :] = xc_ref[...]
    acc = cw_ref[0:1, :] * xpad_scr[pl.ds(8 - (B_CONV - 1), L), :]
    for i in range(1, B_CONV):
        acc = acc + cw_ref[i:i + 1, :] * xpad_scr[pl.ds(8 - (B_CONV - 1) + i, L), :]
    xa = _silu(acc + cb_ref[...])
    xs = xa[:, :B_INNER]
    bm = xa[:, B_INNER:B_INNER + B_GROUPS * B_STATE]
    cm = xa[:, B_INNER + B_GROUPS * B_STATE:]

    raw = dt_ref[...] + dtb_ref[...]
    dt = jnp.maximum(raw, 0.0) + jnp.log1p(jnp.exp(-jnp.abs(raw)))
    adt = dt * (-jnp.exp(al_ref[...]))
    ri = lax.broadcasted_iota(jnp.int32, (L, L), 0)
    ci = lax.broadcasted_iota(jnp.int32, (L, L), 1)
    tri = _mx(jnp.where(ri >= ci, 1.0, 0.0))
    acs = _dot_f32_lhs_exact(tri, adt)
    acs_t = acs.T
    er = lax.broadcasted_iota(jnp.int32, (LANES, B_INNER), 0)
    ec = lax.broadcasted_iota(jnp.int32, (LANES, B_INNER), 1)
    expand = _mx(jnp.where(ec // B_HEADDIM == er, 1.0, 0.0))
    dt_e = _dot_f32_rhs_exact(dt, expand)
    ea_e = _dot_f32_rhs_exact(jnp.exp(acs), expand)
    we_e = _dot_f32_rhs_exact(jnp.exp(acs[L - 1:L, :] - acs), expand)
    xdt = xs * dt_e
    xw = xdt * we_e
    lane = lax.broadcasted_iota(jnp.int32, (L, LANES), 1)
    hpg = B_HEADS // B_GROUPS
    gw = hpg * B_HEADDIM
    y_groups = []
    for g in range(B_GROUPS):
        b_g = bm[:, g * B_STATE:(g + 1) * B_STATE]
        c_g = _mx(cm[:, g * B_STATE:(g + 1) * B_STATE])
        cb = _dot_nt(c_g, _mx(b_g))
        pairs = []
        for pr in range(hpg // 2):
            x_pair = _mx(xdt[:, g * gw + pr * LANES:g * gw + (pr + 1) * LANES])
            halves = []
            for j in range(2):
                h = g * hpg + 2 * pr + j
                diff = acs[:, h:h + 1] - acs_t[h:h + 1, :]
                decay = jnp.exp(jnp.where(ri >= ci, diff, NEG_INF))
                halves.append(_dot(_mx(cb * decay), x_pair))
            pairs.append(jnp.where(lane < B_HEADDIM, halves[0], halves[1]))
        y_diag = jnp.concatenate(pairs, axis=1)
        h_prev = h_scr[:, g * gw:(g + 1) * gw]
        y_off = _dot(c_g, _mx(h_prev)) * ea_e[:, g * gw:(g + 1) * gw]
        st = _dot(_mx(b_g.T), _mx(xw[:, g * gw:(g + 1) * gw]))
        h_scr[:, g * gw:(g + 1) * gw] = h_prev * ea_e[L - 1:L, g * gw:(g + 1) * gw] + st
        y = y_diag + y_off + xs[:, g * gw:(g + 1) * gw] * d_ref[:, g * gw:(g + 1) * gw]
        yz = y * _silu(z_ref[:, g * gw:(g + 1) * gw])
        ms = jnp.mean(yz * yz, axis=-1, keepdims=True)
        y_groups.append(yz * lax.rsqrt(ms + EPS) * nw_ref[:, g * gw:(g + 1) * gw])
    o_ref[...] = jnp.concatenate(y_groups, axis=1)


def _ssd(proj, conv_w, conv_b, dtb_row, al_row, d_row, nw_row):
    s = proj.shape[0]
    L = B_CHUNK
    cx = _FCOLS["b_xbc"][0] // B_CONV_DIM
    cdt = _FCOLS["b_dt"][0] // LANES
    cz = _FCOLS["b_z"][0] // B_INNER
    row = lambda w: pl.BlockSpec((1, w), lambda c: (0, 0))
    return pl.pallas_call(
        _ssd_kernel,
        grid=(s // L,),
        in_specs=[pl.BlockSpec((L, B_CONV_DIM), lambda c: (c, cx)),
                  pl.BlockSpec((L, B_CONV_DIM), lambda c: (jnp.maximum(c - 1, 0), cx)),
                  pl.BlockSpec((L, LANES), lambda c: (c, cdt)),
                  pl.BlockSpec((L, B_INNER), lambda c: (c, cz)),
                  pl.BlockSpec((B_CONV, B_CONV_DIM), lambda c: (0, 0)),
                  row(B_CONV_DIM), row(LANES), row(LANES), row(B_INNER), row(B_INNER)],
        out_specs=pl.BlockSpec((L, B_INNER), lambda c: (c, 0)),
        out_shape=jax.ShapeDtypeStruct((s, B_INNER), F32),
        scratch_shapes=[pltpu.VMEM((L + 8, B_CONV_DIM), F32), pltpu.VMEM((B_STATE, B_INNER), F32)],
        compiler_params=_cparams(("arbitrary",)),
        name="ssd",
    )(proj, proj, proj, proj, conv_w, conv_b, dtb_row, al_row, d_row, nw_row)


def _rope_kernel(q_ref, k_ref, pos_ref, inv_ref, qo_ref, ko_ref):
    ang = pos_ref[...].astype(F32) * inv_ref[...]
    cos, sin = jnp.cos(ang), jnp.sin(ang)
    lane = lax.broadcasted_iota(jnp.int32, ang.shape, 1)
    low = (lane & (HD - 1)) < HD // 2

    def rot(t):
        partner = jnp.where(low, -pltpu.roll(t, LANES - HD // 2, 1), pltpu.roll(t, HD // 2, 1))
        return t * cos + partner * sin

    for j in range(q_ref.shape[1] // LANES):
        qo_ref[:, j * LANES:(j + 1) * LANES] = _mx(rot(q_ref[:, j * LANES:(j + 1) * LANES]))
    ko_ref[...] = _mx(rot(k_ref[...]))


def _rope(proj, pos_col, inv_row):
    s = proj.shape[0]
    tm = min(512, s)
    wq = C_HEADS * HD
    return pl.pallas_call(
        _rope_kernel,
        grid=(s // tm,),
        in_specs=[pl.BlockSpec((tm, wq), lambda i: (i, _FCOLS["c_q"][0] // wq)),
                  pl.BlockSpec((tm, LANES), lambda i: (i, _FCOLS["c_k"][0] // LANES)),
                  pl.BlockSpec((tm, 1), lambda i: (i, 0)),
                  pl.BlockSpec((1, LANES), lambda i: (0, 0))],
        out_specs=[pl.BlockSpec((tm, wq), lambda i: (i, 0)), pl.BlockSpec((tm, LANES), lambda i: (i, 0))],
        out_shape=[jax.ShapeDtypeStruct((s, wq), MXU_DTYPE), jax.ShapeDtypeStruct((s, LANES), MXU_DTYPE)],
        compiler_params=_cparams(("arbitrary",)),
        name="rope",
    )(proj, proj, pos_col, inv_row)


def _pair_rows(q2, low):
    zero = jnp.zeros_like(q2)
    return jnp.concatenate([jnp.where(low, q2, zero), jnp.where(low, zero, q2)], axis=0)


def _band_kernel(*refs, dil, rows, max_dist, has_bias, has_sink, want_lse):
    refs = list(refs)
    q_ref, k_ref, kh_ref, v_ref, vh_ref = refs[:5]
    pos = 5
    tb_ref = sk_ref = None
    if has_bias:
        tb_ref = refs[pos]
        pos += 1
    if has_sink:
        sk_ref = refs[pos]
        pos += 1
    o_ref = refs[pos]
    lse_ref = refs[pos + 1] if want_lse else None
    j = pl.program_id(0)
    i = pl.program_id(1)
    span = BLK * dil
    qi = lax.broadcasted_iota(jnp.int32, (BLK, 2 * BLK), 0)
    kj = lax.broadcasted_iota(jnp.int32, (BLK, 2 * BLK), 1)
    dist = BLK + qi - kj
    band = jnp.where((dist >= 0) & (dist <= max_dist), 0.0, NEG_INF)
    first = jnp.where(i > 0, band, jnp.where(kj >= BLK, band, NEG_INF))
    low = lax.broadcasted_iota(jnp.int32, (BLK, LANES), 1) < HD
    bias = None
    if has_bias:
        bias = jnp.concatenate([jnp.concatenate([tb_ref[a, 0], tb_ref[a, 1]], axis=1) for a in range(2)], axis=0)
    sink = None
    if has_sink:
        sink = jnp.concatenate([jnp.full((BLK, 1), sk_ref[2 * j + a], F32) for a in range(2)], axis=0)

    def take(ref, start):
        return ref[pl.ds(start, BLK, stride=dil), :] if dil > 1 else ref[start:start + BLK, :]

    def put(ref, start, val):
        if dil > 1:
            ref[pl.ds(start, BLK, stride=dil), :] = val
        else:
            ref[start:start + BLK, :] = val

    for sb in range(rows // span):
        for r in range(dil):
            base = sb * span + r
            qs = _mx(_pair_rows(take(q_ref, base), low))
            if sb == 0:
                k_prev, v_prev, mask = take(kh_ref, r), take(vh_ref, r), first
            else:
                k_prev, v_prev, mask = take(k_ref, base - span), take(v_ref, base - span), band
            k_win = _mx(jnp.concatenate([k_prev, take(k_ref, base)], axis=0))
            v_win = _mx(jnp.concatenate([v_prev, take(v_ref, base)], axis=0))
            s = _dot_nt(qs, k_win) + jnp.concatenate([mask, mask], axis=0)
            if has_bias:
                s = s + bias
            m = jnp.max(s, axis=-1, keepdims=True)
            if has_sink:
                m = jnp.maximum(m, sink)
            e = jnp.exp(s - m)
            den = jnp.sum(e, axis=-1, keepdims=True)
            if has_sink:
                den = den + jnp.exp(sink - m)
            o = _dot(_mx(e), v_win) / den
            put(o_ref, base, jnp.where(low, o[0:BLK], o[BLK:]))
            if want_lse:
                lse = jnp.broadcast_to(m + jnp.log(den), (2 * BLK, LANES))
                put(lse_ref, base, jnp.where(low, lse[0:BLK], lse[BLK:]))


def _band(q_arr, k_arr, v_arr, *, q_col, k_col, v_col, shared_kv, dil, max_dist, tb=None, sinks=None,
          want_lse=False):
    s = q_arr.shape[0]
    span = BLK * dil
    rows = max(min(2048, s), span)
    per = rows // span
    npair = A_HEADS * HD // LANES
    kv = (lambda j: 0) if shared_kv else (lambda j: j)
    in_specs = [pl.BlockSpec((rows, LANES), lambda j, i: (i, q_col + j)),
                pl.BlockSpec((rows, LANES), lambda j, i: (i, k_col + kv(j))),
                pl.BlockSpec((span, LANES), lambda j, i: (jnp.maximum(i * per - 1, 0), k_col + kv(j))),
                pl.BlockSpec((rows, LANES), lambda j, i: (i, v_col + kv(j))),
                pl.BlockSpec((span, LANES), lambda j, i: (jnp.maximum(i * per - 1, 0), v_col + kv(j)))]
    args = [q_arr, k_arr, k_arr, v_arr, v_arr]
    if tb is not None:
        in_specs.append(pl.BlockSpec((2, 2, BLK, LANES), lambda j, i: (j, 0, 0, 0)))
        args.append(tb)
    if sinks is not None:
        in_specs.append(pl.BlockSpec(memory_space=pltpu.SMEM))
        args.append(sinks)
    o_spec = pl.BlockSpec((rows, LANES), lambda j, i: (i, j))
    o_shape = jax.ShapeDtypeStruct((s, npair * LANES), F32)
    return pl.pallas_call(
        functools.partial(_band_kernel, dil=dil, rows=rows, max_dist=max_dist, has_bias=tb is not None,
                          has_sink=sinks is not None, want_lse=want_lse),
        grid=(npair, s // rows),
        in_specs=in_specs,
        out_specs=[o_spec, o_spec] if want_lse else o_spec,
        out_shape=[o_shape, o_shape] if want_lse else o_shape,
        compiler_params=_cparams(("arbitrary", "arbitrary")),
        name="band_attention",
    )(*args)


def _memkv_kernel(mem_ref, nw_ref, w_ref, o_ref):
    x = mem_ref[...]
    y = x * lax.rsqrt(jnp.mean(x * x, axis=-1, keepdims=True) + EPS) * nw_ref[...]
    o_ref[...] = _dot(_mx(y), w_ref[...])


def _memkv(mem2, norm_w, w):
    return pl.pallas_call(
        _memkv_kernel,
        out_shape=jax.ShapeDtypeStruct((mem2.shape[0], w.shape[1]), F32),
        compiler_params=pltpu.CompilerParams(vmem_limit_bytes=VMEM_LIMIT),
        name="mem_kv",
    )(mem2, norm_w, w)


def _memattn_kernel(q_ref, k_ref, v_ref, o_ref):
    tq = q_ref.shape[0]
    low = lax.broadcasted_iota(jnp.int32, (tq, LANES), 1) < HD
    for pr in range(M_HEADS // 2):
        cols = slice(pr * LANES, (pr + 1) * LANES)
        s = _dot_nt(_pair_rows(q_ref[:, cols], low), _mx(k_ref[:, cols]))
        e = jnp.exp(s - jnp.max(s, axis=-1, keepdims=True))
        o = _dot(_mx(e), _mx(v_ref[:, cols])) / jnp.sum(e, axis=-1, keepdims=True)
        o_ref[:, cols] = jnp.where(low, o[0:tq], o[tq:])


def _memattn(proj_b, kv):
    s = proj_b.shape[0]
    tq = min(256, s)
    ml = kv.shape[0]
    wm = M_HEADS * HD
    return pl.pallas_call(
        _memattn_kernel,
        grid=(s // tq,),
        in_specs=[pl.BlockSpec((tq, wm), lambda i: (i, _BCOLS["m_q"][0] // wm)),
                  pl.BlockSpec((ml, wm), lambda i: (0, 0)),
                  pl.BlockSpec((ml, wm), lambda i: (0, 1))],
        out_specs=pl.BlockSpec((tq, wm), lambda i: (i, 0)),
        out_shape=jax.ShapeDtypeStruct((s, wm), F32),
        compiler_params=_cparams(("arbitrary",)),
        name="mem_attention",
    )(proj_b, kv, kv)


def _outproj_kernel(x_ref, a_ref, az_ref, b_ref, c_ref, cz_ref, d0_ref, d1_ref, d2_ref, l0_ref, l1_ref, l2_ref,
                    dz_ref, m_ref, mz_ref, w_ref, nw_ref, o_ref):
    l0, l1, l2 = l0_ref[...], l1_ref[...], l2_ref[...]
    mx = jnp.maximum(jnp.maximum(l0, l1), l2)
    e0, e1, e2 = jnp.exp(l0 - mx), jnp.exp(l1 - mx), jnp.exp(l2 - mx)
    inv = 1.0 / (e0 + e1 + e2)
    d = (e0 * inv) * d0_ref[...] + (e1 * inv) * d1_ref[...] + (e2 * inv) * d2_ref[...]
    pieces = (a_ref[...] * _silu(az_ref[...]), b_ref[...], c_ref[...] * _silu(cz_ref[...]),
              d * _silu(dz_ref[...]), m_ref[...] * _silu(mz_ref[...]))
    y = None
    row = 0
    for piece in pieces:
        w = piece.shape[1]
        part = _dot(_mx(piece), w_ref[row:row + w, :])
        y = part if y is None else y + part
        row += w
    y = y * lax.rsqrt(jnp.mean(y * y, axis=-1, keepdims=True) + EPS) * nw_ref[...]
    o_ref[...] = x_ref[...] + y


def _outproj(x2, proj, a_out, b_out, c_out, d_outs, d_lses, m_out, w, norm_w):
    s = x2.shape[0]
    tm = min(512, s)
    wide = lambda width, col: pl.BlockSpec((tm, width), lambda i: (i, col))
    w512 = A_HEADS * HD
    wm = M_HEADS * HD
    return pl.pallas_call(
        _outproj_kernel,
        grid=(s // tm,),
        in_specs=[wide(D_MODEL, 0), wide(w512, 0), wide(w512, _FCOLS["a_z"][0] // w512), wide(w512, 0),
                  wide(w512, 0), wide(w512, _FCOLS["c_z"][0] // w512),
                  wide(w512, 0), wide(w512, 0), wide(w512, 0), wide(w512, 0), wide(w512, 0), wide(w512, 0),
                  wide(w512, _FCOLS["d_z"][0] // w512), wide(wm, 0), wide(wm, _FCOLS["m_z"][0] // wm),
                  pl.BlockSpec((MIX_WIDTH, D_MODEL), lambda i: (0, 0)),
                  pl.BlockSpec((1, D_MODEL), lambda i: (0, 0))],
        out_specs=wide(D_MODEL, 0),
        out_shape=jax.ShapeDtypeStruct((s, D_MODEL), F32),
        compiler_params=_cparams(("arbitrary",)),
        name="outproj",
    )(x2, a_out, proj, b_out, c_out, proj, *d_outs, *d_lses, proj, m_out, proj, w, norm_w)


def _c_heads(t, axis):
    shape = t.shape
    t = t.reshape(shape[:axis] + (C_HEADS, HD) + shape[axis + 1:])
    return jnp.take(t, jnp.array(_C_ORDER), axis=axis).reshape(shape)


def _repack_w_in(w, cols):
    out = []
    for name, (_, width) in cols.items():
        src, true_w, is_query = _SRC[name]
        piece = w[:, src:src + true_w]
        if is_query:
            piece = piece * HD ** -0.5
        if name in ("c_q", "c_z"):
            piece = _c_heads(piece, 1)
        if true_w < width:
            piece = jnp.pad(piece, ((0, 0), (0, width - true_w)))
        out.append(piece)
    return _mx(jnp.concatenate(out, axis=1))


def _heads_first_t(t, nh):
    s = t.shape[0]
    return jnp.transpose(t.reshape(s, nh, HD), (1, 2, 0))


def kernel(x, mem, positions, pre_norm, post_norm, w_in, w_out, rel_bias, a_cmp_pos, a_cmp_w1, a_cmp_w2,
           b_conv_w, b_conv_b, b_dt_bias, b_a_log, b_d, b_norm, c_sinks, m_norm, m_w_kv):
    b, s, _ = x.shape
    assert b == 1 and s % 2048 == 0, "sequence length must be a multiple of 2048 and batch 1"
    depth = w_in.shape[0]
    nb = s // BLK
    ncmp = s // CMP_STRIDE

    rel_a = rel_bias[:, :A_HEADS]
    rel_d = rel_bias[:, A_HEADS:]
    offs = jnp.arange(N_OFF, dtype=jnp.int32) * BLK
    tb_a = _bias_table(rel_a, offs, a_row=-1, b_lane=1, scale=1, rows=BLK)
    bc_a = _bias_table(rel_a, jnp.array([CMP_STRIDE * CMP_NEAR - (CMP_LEN - 1)], jnp.int32),
                       a_row=-CMP_STRIDE, b_lane=1, scale=1, rows=CMP_TAB)
    tb_d = [_bias_table(rel_d[:, p * D_SLOTS:(p + 1) * D_SLOTS], jnp.array([BLK, 0], jnp.int32), a_row=1,
                        b_lane=-1, scale=dil, rows=BLK) for p, (_, dil) in enumerate(D_PATTERNS)]
    key_pos = np.arange(s)
    key_aug_np = np.zeros((s, AUG_K - AUG_BASE), np.float32)
    key_aug_np[:, 0:2] = 1.0
    key_aug_np[key_pos, AUG_SEL - AUG_BASE + (key_pos // SEL_BLK) % (SEL_TILE // SEL_BLK)] = 1.0
    key_aug = jnp.asarray(key_aug_np, MXU_DTYPE)
    val_aug_np = np.zeros((V_ROWS - HD, s), np.float32)
    val_aug_np[0] = 1.0
    val_aug = jnp.asarray(val_aug_np, MXU_DTYPE)[None]
    head_slot = jnp.eye(A_KV, dtype=F32)
    half = ROPE_THETA ** (-jnp.arange(HD // 2, dtype=F32) / (HD // 2))
    inv_row = jnp.tile(half, LANES // (HD // 2)).reshape(1, LANES)
    pos_col = positions.reshape(s, 1)
    mem2 = mem.reshape(mem.shape[1], D_MODEL)

    x2 = x.reshape(s, D_MODEL)
    for layer in range(depth):
        norm_w = pre_norm[layer].reshape(1, D_MODEL)
        proj = _inproj(x2, norm_w, _repack_w_in(w_in[layer], _FCOLS), F32)
        proj_b = _inproj(x2, norm_w, _repack_w_in(w_in[layer], _BCOLS), MXU_DTYPE)
        fcol = lambda name: proj[:, _FCOLS[name][0]:_FCOLS[name][0] + _FCOLS[name][1]]
        bcol = lambda name: proj_b[:, _BCOLS[name][0]:_BCOLS[name][0] + _BCOLS[name][1]]

        wide = CMP_STRIDE * A_KV * HD
        u = jnp.stack([fcol("a_kc").reshape(ncmp, wide), fcol("a_vc").reshape(ncmp, wide)])
        pe = (a_cmp_pos[layer].reshape(2, 1, 2, CMP_STRIDE, 1, HD)
              * head_slot.reshape(1, A_KV, 1, 1, A_KV, 1)).reshape(2 * A_KV, 2, wide)
        w1 = _mx(a_cmp_w1[layer].reshape(2, 1, 2, CMP_STRIDE, 1, HD, CMP_HIDDEN)
                 * head_slot.reshape(1, A_KV, 1, 1, A_KV, 1, 1)).reshape(2 * A_KV, 2, wide, CMP_HIDDEN)
        cmp_out = _compress(u, pe, w1, _mx(a_cmp_w2[layer]), ncmp)
        kc_own = (cmp_out[:A_KV, :, None, :] * head_slot[:, None, :, None]).reshape(A_KV, ncmp, AUG_BASE)
        kc = _mx(jnp.concatenate([kc_own, jnp.ones((A_KV, ncmp, 2), F32),
                                  jnp.zeros((A_KV, ncmp, AUG_K - AUG_BASE - 2), F32)], axis=2))
        vct = _mx(jnp.transpose(cmp_out[A_KV:], (0, 2, 1)))
        gates = jnp.transpose(fcol("a_gate")[:, :3 * A_HEADS].reshape(nb, BLK, A_KV, A_G, 3),
                              (2, 0, 4, 3, 1)).reshape(A_KV, nb, 3, A_G * BLK)
        vst_aug = jnp.concatenate([_heads_first_t(bcol("a_vs"), A_KV),
                                   jnp.broadcast_to(val_aug, (A_KV, V_ROWS - HD, s))], axis=1)
        a_out = _nsa(proj_b, kc, vct, key_aug, vst_aug, _heads_first_t(bcol("a_vw"), A_KV), gates, tb_a, bc_a, s)

        pad_row = lambda v: jnp.pad(v, (0, LANES - v.shape[0])).reshape(1, LANES)
        b_out = _ssd(proj, b_conv_w[layer], b_conv_b[layer].reshape(1, B_CONV_DIM), pad_row(b_dt_bias[layer]),
                     pad_row(b_a_log[layer]), jnp.repeat(b_d[layer], B_HEADDIM).reshape(1, B_INNER),
                     b_norm[layer].reshape(1, B_INNER))

        cq_rot, ck_rot = _rope(proj, pos_col, inv_row)
        c_out = _band(cq_rot, ck_rot, proj_b, q_col=0, k_col=0, v_col=_BCOLS["c_v"][0] // LANES, shared_kv=True,
                      dil=1, max_dist=C_WINDOW - 1, sinks=c_sinks[layer][jnp.array(_C_ORDER)])

        d_outs, d_lses = [], []
        for p, (window, dil) in enumerate(D_PATTERNS):
            o_p, lse_p = _band(proj, proj, proj, q_col=_FCOLS["d_q"][0] // LANES + p * D_SLOTS * HD // LANES,
                               k_col=_FCOLS["d_k"][0] // LANES, v_col=_FCOLS["d_v"][0] // LANES, shared_kv=False,
                               dil=dil, max_dist=window // dil, tb=tb_d[p], want_lse=True)
            d_outs.append(o_p)
            d_lses.append(lse_p)

        kv = _memkv(mem2, m_norm[layer].reshape(1, D_MODEL), _mx(m_w_kv[layer]))
        m_out = _memattn(proj_b, kv)

        c_rows = slice(A_HEADS * HD + B_INNER, A_HEADS * HD + B_INNER + C_HEADS * HD)
        w_o = w_out[layer].at[c_rows].set(_c_heads(w_out[layer][c_rows], 0))
        x2 = _outproj(x2, proj, a_out, b_out, c_out, d_outs, d_lses, m_out, _mx(w_o),
                      post_norm[layer].reshape(1, D_MODEL))
    return x2.reshape(b, s, D_MODEL)
```

```python
import functools
import math

import numpy as np
import jax
import jax.numpy as jnp
from jax import lax
from jax.experimental import pallas as pl
from jax.experimental.pallas import tpu as pltpu

F32 = jnp.float32
MXU_DTYPE = jnp.bfloat16

D_MODEL = 1024
HD = 64
BLK = 128
NEG_INF = -1e30
TINY = 1e-30
EPS = 1e-6

A_HEADS, A_KV = 8, 2
A_G = A_HEADS // A_KV
CMP_LEN, CMP_STRIDE, CMP_HIDDEN = 32, 16, 128
SEL_BLK, SEL_TOPK, A_WINDOW, SEL_FORCE = 64, 16, 512, 1e4
B_HEADS, B_HEADDIM, B_GROUPS, B_STATE, B_CONV, B_CHUNK = 8, 64, 2, 128, 4, 128
B_INNER = B_HEADS * B_HEADDIM
B_CONV_DIM = B_INNER + 2 * B_GROUPS * B_STATE
C_HEADS, C_KV, C_WINDOW = 8, 2, 128
ROPE_THETA = 150000.0
D_SLOTS = 8
D_PATTERNS = ((128, 1), (512, 4), (2048, 16))
D_NPAT = 3
M_HEADS = 4
N_BUCKETS, MAX_DIST = 32, 2048
MIX_WIDTH = A_HEADS * HD + B_INNER + C_HEADS * HD + D_SLOTS * HD + M_HEADS * HD

LANES = 128
VMEM_LIMIT = 56 * 1024 * 1024

_SRC = {"a_q": (0, 512, True), "a_kc": (512, 128, False), "a_vc": (640, 128, False), "a_ks": (768, 128, False),
        "a_vs": (896, 128, False), "a_kw": (1024, 128, False), "a_vw": (1152, 128, False),
        "a_gate": (1280, 24, False), "a_z": (1304, 512, False), "b_xbc": (1816, 1024, False),
        "b_dt": (2840, 8, False), "b_z": (2848, 512, False), "c_q": (3360, 512, True), "c_k": (3872, 128, False),
        "c_v": (4000, 128, False), "c_z": (4128, 512, False), "d_q": (4640, 1536, True), "d_k": (6176, 512, False),
        "d_v": (6688, 512, False), "d_z": (7200, 512, False), "m_q": (7712, 256, True), "m_z": (7968, 256, False)}


def _layout(pieces):
    cols, off = {}, 0
    for name, width in pieces:
        cols[name] = (off, width)
        off += width
    return cols, off


_FCOLS, N_F32 = _layout((("b_xbc", 1024), ("a_z", 512), ("b_z", 512), ("c_z", 512), ("d_z", 512), ("c_q", 512),
                         ("d_q", 1536), ("d_k", 512), ("d_v", 512), ("m_z", 256), ("a_kc", 128), ("a_vc", 128),
                         ("c_k", 128), ("a_gate", 128), ("b_dt", 128)))
_BCOLS, N_BF16 = _layout((("a_q", 512), ("m_q", 256), ("a_ks", 128), ("a_vs", 128), ("a_kw", 128),
                          ("a_vw", 128), ("c_v", 128)))
PROJ_TN = 1408
_C_ORDER = tuple(h for j in range(C_HEADS // C_KV) for h in (j, j + C_HEADS // C_KV))


def _t5_thresholds():
    d = np.arange(0, 4 * MAX_DIST)
    exact = N_BUCKETS // 2
    rel = np.maximum(d, exact).astype(np.float64)
    large = exact + (np.log(rel / exact) / math.log(MAX_DIST / exact) * (N_BUCKETS - exact)).astype(np.int64)
    bucket = np.where(d < exact, d, np.minimum(large, N_BUCKETS - 1))
    return tuple(int(np.argmax(bucket >= b)) for b in range(1, N_BUCKETS))


_THR = _t5_thresholds()
FAR_DIST = _THR[-1]
N_OFF = -(-(FAR_DIST + BLK) // BLK) + 1
TAB_OFF0 = 1
TAB_EDGE = TAB_OFF0 + N_OFF
N_TAB = TAB_EDGE + 1
FAR_WIDTH = 4
CMP_REACH = -(-FAR_DIST // CMP_STRIDE) // 8 * 8 + 8
CMP_WIN = 128
CMP_NEAR = 128
CMP_TAB = CMP_NEAR + CMP_WIN
SEL_TILE = 512
AUG_BASE = LANES
AUG_SEL = AUG_BASE + 8
AUG_END = AUG_SEL + SEL_TILE // SEL_BLK
AUG_K = 2 * LANES
V_ROWS = HD + 16


def _cparams(sem):
    return pltpu.CompilerParams(dimension_semantics=sem, vmem_limit_bytes=VMEM_LIMIT)


def _mx(x):
    return x.astype(MXU_DTYPE)


def _dot(a, b):
    return jnp.dot(a, b, preferred_element_type=F32)


def _dot_nt(a, b):
    return lax.dot_general(a, b, (((1,), (1,)), ((), ())), preferred_element_type=F32)


def _split3(a):
    hi = _mx(a)
    r1 = a - hi.astype(F32)
    mid = _mx(r1)
    lo = _mx(r1 - mid.astype(F32))
    return hi, mid, lo


def _dot_f32_rhs_exact(a, b):
    hi, mid, lo = _split3(a)
    return _dot(hi, b) + _dot(mid, b) + _dot(lo, b)


def _dot_f32_lhs_exact(a, b):
    hi, mid, lo = _split3(b)
    return _dot(a, hi) + _dot(a, mid) + _dot(a, lo)


def _silu(x):
    return x * jax.nn.sigmoid(x)


def _softmax_cols(s, mask, sink=None):
    s = jnp.where(mask, s, NEG_INF)
    m = jnp.max(s, axis=0, keepdims=True)
    if sink is not None:
        m = jnp.maximum(m, sink)
    p = jnp.where(mask, jnp.exp(s - m), 0.0)
    den = jnp.sum(p, axis=0, keepdims=True)
    if sink is not None:
        den = den + jnp.exp(sink - m)
    return p, den, m


def _bias_table_kernel(rel_ref, c0_ref, hi_ref, o_ref, *, a_row, b_lane, scale, rows, masked):
    h = pl.program_id(0)
    t = pl.program_id(1)
    r = lax.broadcasted_iota(jnp.int32, (rows, LANES), 0)
    l = lax.broadcasted_iota(jnp.int32, (rows, LANES), 1)
    dist = (c0_ref[t] + a_row * r + b_lane * l) * scale
    out = jnp.full((rows, LANES), rel_ref[0, h], F32)
    for b in range(1, N_BUCKETS):
        out = jnp.where(dist >= _THR[b - 1], rel_ref[b, h], out)
    if masked:
        out = jnp.where(dist < 0, NEG_INF, jnp.where(dist >= hi_ref[t], NEG_INF, out))
    o_ref[0, 0] = out


def _bias_table(rel_cols, c0, *, a_row, b_lane, scale, rows, hi=None):
    nh = rel_cols.shape[1]
    nt = c0.shape[0]
    return pl.pallas_call(
        functools.partial(_bias_table_kernel, a_row=a_row, b_lane=b_lane, scale=scale, rows=rows,
                          masked=hi is not None),
        grid_spec=pltpu.PrefetchScalarGridSpec(
            num_scalar_prefetch=3, grid=(nh, nt), in_specs=[],
            out_specs=pl.BlockSpec((1, 1, rows, LANES), lambda h, t, *_: (h, t, 0, 0))),
        out_shape=jax.ShapeDtypeStruct((nh, nt, rows, LANES), F32),
        compiler_params=_cparams(("arbitrary", "arbitrary")),
        name="bias_table",
    )(rel_cols, c0, c0 if hi is None else hi)


def _inproj_kernel(x_ref, nw_ref, w_ref, o_ref, h_scr):
    @pl.when(pl.program_id(1) == 0)
    def _():
        x = x_ref[...]
        y = x * lax.rsqrt(jnp.mean(x * x, axis=-1, keepdims=True) + EPS)
        h_scr[...] = _mx(y * nw_ref[...])

    o_ref[...] = _dot(h_scr[...], w_ref[...]).astype(o_ref.dtype)


def _inproj(x2, norm_w, w, out_dtype):
    s = x2.shape[0]
    tm, tn = min(1024, s), PROJ_TN
    return pl.pallas_call(
        _inproj_kernel,
        grid=(s // tm, w.shape[1] // tn),
        in_specs=[pl.BlockSpec((tm, D_MODEL), lambda i, j: (i, 0)),
                  pl.BlockSpec((1, D_MODEL), lambda i, j: (0, 0)),
                  pl.BlockSpec((D_MODEL, tn), lambda i, j: (0, j))],
        out_specs=pl.BlockSpec((tm, tn), lambda i, j: (i, j)),
        out_shape=jax.ShapeDtypeStruct((s, w.shape[1]), out_dtype),
        scratch_shapes=[pltpu.VMEM((tm, D_MODEL), MXU_DTYPE)],
        compiler_params=_cparams(("arbitrary", "arbitrary")),
        name="inproj",
    )(x2, norm_w, w)


def _compress_kernel(u_ref, pe_ref, w1_ref, w2_ref, o_ref, h2_scr, *, ncmp):
    u = u_ref[0]
    pe_first, pe_second = pe_ref[0, 0:1, :], pe_ref[0, 1:2, :]
    h1 = _dot(_mx(u + pe_first), w1_ref[0, 0])
    h2_scr[0:ncmp, :] = _dot(_mx(u + pe_second), w1_ref[0, 1])
    h2_scr[ncmp:ncmp + 8, :] = _dot(_mx(jnp.broadcast_to(pe_second, (8, pe_second.shape[1]))), w1_ref[0, 1])
    pre = h1 + h2_scr[pl.ds(1, ncmp), :]
    o_ref[0] = _dot(_mx(jax.nn.gelu(pre)), w2_ref[0])


def _compress(u, pe, w1, w2, ncmp):
    wide = CMP_STRIDE * A_KV * HD
    return pl.pallas_call(
        functools.partial(_compress_kernel, ncmp=ncmp),
        grid=(2 * A_KV,),
        in_specs=[pl.BlockSpec((1, ncmp, wide), lambda i: (i // A_KV, 0, 0)),
                  pl.BlockSpec((1, 2, wide), lambda i: (i, 0, 0)),
                  pl.BlockSpec((1, 2, wide, CMP_HIDDEN), lambda i: (i, 0, 0, 0)),
                  pl.BlockSpec((1, CMP_HIDDEN, HD), lambda i: (i // A_KV, 0, 0))],
        out_specs=pl.BlockSpec((1, ncmp, HD), lambda i: (i, 0, 0)),
        out_shape=jax.ShapeDtypeStruct((2 * A_KV, ncmp, HD), F32),
        scratch_shapes=[pltpu.VMEM((ncmp + 8, CMP_HIDDEN), F32)],
        compiler_params=_cparams(("arbitrary",)),
        name="nsa_compress",
    )(u, pe, w1, w2)


def _nsa_kernel(q_ref, kc_ref, vct_ref, ks_ref, ka_ref, vst_ref, kw_ref, vwt_ref, g_ref, tb_ref, bc_ref,
                o_ref, s_scr, sel_scr, qa_scr, sa_scr, sb_scr, sc_scr, sd_scr, pcs_scr, *, seq, tk):
    hk = pl.program_id(0)
    n = pl.program_id(1)
    ncmp = seq // CMP_STRIDE
    nsel = seq // SEL_BLK
    n_top = min(SEL_TOPK, nsel)
    r_all = A_G * BLK
    q_rows = q_ref[...].astype(F32)
    halves = [q_rows[:, p * LANES:(p + 1) * LANES].T for p in range(A_G // 2)]
    q_t = _mx(jnp.concatenate([h[r * HD:(r + 1) * HD] for h in halves for r in range(2)], axis=1))
    far_row = jnp.concatenate([tb_ref[g, TAB_OFF0 + N_OFF - 1, 0:1, :] for g in range(A_G)], axis=1)

    def tile_bias(o, edge=None):
        idx = jnp.clip(o, -1, N_OFF - 1) + TAB_OFF0
        if edge is not None:
            idx = jnp.where(o == edge, TAB_EDGE, idx)
        return jnp.concatenate([tb_ref[g, idx] for g in range(A_G)], axis=1)

    far_hi = _mx(far_row).astype(F32)
    bias_rows = jnp.concatenate([far_hi, far_row - far_hi, jnp.zeros((AUG_SEL - AUG_BASE - 2, r_all), F32)], axis=0)
    qa_scr[0:AUG_BASE, :] = jnp.zeros((AUG_BASE, r_all), MXU_DTYPE)
    qa_scr[pl.ds(pl.multiple_of(hk * HD, HD), HD), :] = q_t
    qa_scr[AUG_BASE:AUG_END, :] = _mx(
        jnp.concatenate([bias_rows, jnp.zeros((AUG_END - AUG_SEL, r_all), F32)], axis=0))
    qa_scr[AUG_END:, :] = jnp.zeros((AUG_K - AUG_END, r_all), MXU_DTYPE)
    q_own = qa_scr[0:AUG_BASE, :]

    s_scr[...] = _dot(kc_ref[0], qa_scr[...])
    st = jnp.clip((8 * n - CMP_REACH) // 16 * 16, 0, ncmp - CMP_WIN)
    off = pl.multiple_of(st - 8 * n + CMP_NEAR, 8)
    st = pl.multiple_of(st, 16)
    near = jnp.concatenate([bc_ref[g, 0, pl.ds(off, CMP_WIN), :] for g in range(A_G)], axis=1)
    s_scr[pl.ds(st, CMP_WIN), :] = _dot(kc_ref[0, pl.ds(st, CMP_WIN), 0:AUG_BASE], q_own) + near

    nwb = A_WINDOW // BLK + 1
    kb0 = jnp.maximum(n - (nwb - 1), 0)
    k0 = pl.multiple_of(kb0 * BLK, BLK)
    s = _dot(kw_ref[pl.ds(k0, nwb * BLK), :], q_own)
    s = s + jnp.concatenate([tile_bias(n - kb0 - i, edge=nwb - 1) for i in range(nwb)], axis=0)
    e = jnp.exp(s - jnp.max(s, axis=0, keepdims=True))
    o_w = _dot(vwt_ref[0, :, pl.ds(k0, nwb * BLK)], _mx(e)) / jnp.sum(e, axis=0, keepdims=True)

    jrow = lax.broadcasted_iota(jnp.int32, (ncmp, BLK), 0)
    qpos = n * BLK + lax.broadcasted_iota(jnp.int32, (ncmp, BLK), 1)
    visible = jnp.where(jrow * CMP_STRIDE + (CMP_LEN - 1) <= qpos, 0.0, NEG_INF)
    s_c = s_scr[...] + jnp.concatenate([visible] * A_G, axis=1)
    m = jnp.max(s_c, axis=0, keepdims=True)
    e = jnp.exp(s_c - m)
    den = jnp.sum(e, axis=0, keepdims=True)
    inv = jnp.where(m > 0.5 * NEG_INF, 1.0 / jnp.maximum(den, TINY), 0.0)
    p_c = e * inv
    o_c = _dot(vct_ref[0], _mx(p_c))

    pcs = p_c[:, 0:BLK]
    for g in range(1, A_G):
        pcs = pcs + p_c[:, g * BLK:(g + 1) * BLK]
    per = SEL_BLK // CMP_STRIDE
    pcs_scr[0:8, :] = jnp.zeros((8, BLK), F32)
    pcs_scr[8:8 + ncmp, :] = pcs
    rows = [pcs_scr[pl.ds(8 + i, nsel, stride=per), :] for i in range(-1, per)]
    imp = rows[1] + rows[0]
    for i in range(1, per):
        imp = imp + (rows[i + 1] + rows[i])
    b_io = lax.broadcasted_iota(jnp.int32, (nsel, BLK), 0)
    b_f = b_io.astype(F32)
    pos = n * BLK + lax.broadcasted_iota(jnp.int32, (nsel, BLK), 1)
    forced = (b_io == pos // SEL_BLK) | (b_io == 0)
    valid = b_io * SEL_BLK <= pos
    imp = jnp.where(forced, -2.0, jnp.where(valid, imp, -1.0))

    def pick_one(_, carry):
        imp, sel = carry
        vals = [imp[8 * g:8 * (g + 1)] for g in range(nsel // 8)]
        idxs = [b_f[8 * g:8 * (g + 1)] for g in range(nsel // 8)]
        while len(vals) > 1:
            later = [vals[a + 1] > vals[a] for a in range(0, len(vals), 2)]
            idxs = [jnp.where(c, idxs[2 * a + 1], idxs[2 * a]) for a, c in enumerate(later)]
            vals = [jnp.where(c, vals[2 * a + 1], vals[2 * a]) for a, c in enumerate(later)]
        v, ix = vals[0], idxs[0]
        top = jnp.max(v, axis=0, keepdims=True)
        first = jnp.min(jnp.where(v == top, ix, float(nsel)), axis=0, keepdims=True)
        hit = b_f == first
        sel = jnp.where(hit, jnp.where(top >= 0.0, 0.0, NEG_INF), sel)
        return jnp.where(hit, -2.0, imp), sel

    _, sel = lax.fori_loop(0, n_top - 2, pick_one, (imp, jnp.where(forced, 0.0, NEG_INF)))
    sel_scr[...] = sel

    nblk_t = tk // BLK
    no_bias_rows = jnp.zeros((AUG_SEL - AUG_BASE, r_all), F32)

    def scores(t, const_rows):
        k0 = pl.multiple_of(t * tk, tk)
        picked = sel_scr[pl.ds(pl.multiple_of((AUG_END - AUG_SEL) * t, 8), AUG_END - AUG_SEL), :]
        aug = _mx(jnp.concatenate([const_rows, jnp.concatenate([picked] * A_G, axis=1)], axis=0))
        q_aug = jnp.concatenate([q_own, aug, jnp.zeros((AUG_K - AUG_END, r_all), MXU_DTYPE)], axis=0)
        keys = jnp.concatenate([ks_ref[pl.ds(k0, tk), :], ka_ref[pl.ds(k0, tk), :]], axis=1)
        return _dot(keys, q_aug)

    last = (n * BLK) // tk

    def tile_scores(t, near):
        if not near:
            return scores(t, bias_rows)
        bias = jnp.concatenate([tile_bias(n - (nblk_t * t + i)) for i in range(nblk_t)], axis=0)
        return scores(t, no_bias_rows) + bias

    def accumulate(t, s, m_i, acc, m_tile):
        k0 = pl.multiple_of(t * tk, tk)
        m_new = jnp.maximum(m_i, m_tile)
        p = jnp.exp(s - m_new)
        acc = acc * jnp.exp(m_i - m_new) + _dot(vst_ref[0, :, pl.ds(k0, tk)], _mx(p))
        return m_new, acc

    def stage1(t, buf, near):
        s = tile_scores(t, near)
        buf[...] = s
        return jnp.max(s, axis=0, keepdims=True)

    def run_pipelined(first, trips, width, near, m_i, acc):
        bufs = (sa_scr, sb_scr, sc_scr, sd_scr)[:width]

        def trip(u, carry):
            m_i, acc, m_cur = carry
            t0 = first + width * u
            for k in range(width):
                m_next = stage1(t0 + k + 1, bufs[(k + 1) % width], near)
                m_i, acc = accumulate(t0 + k, bufs[k][...], m_i, acc, m_cur)
                m_cur = m_next
            return m_i, acc, m_cur

        return lax.fori_loop(0, trips, trip, (m_i, acc, stage1(first, sa_scr, near)))

    def single(t, carry):
        s = tile_scores(t, True)
        return accumulate(t, s, *carry, jnp.max(s, axis=0, keepdims=True))

    far_tiles = jnp.maximum(n - (N_OFF - 2), 0) // nblk_t
    far_trips = jnp.maximum(far_tiles - 1, 0) // FAR_WIDTH
    m_i, acc_s, m_t = run_pipelined(0, far_trips, FAR_WIDTH, False, jnp.full((1, r_all), NEG_INF, F32),
                                    jnp.zeros((V_ROWS, r_all), F32))
    far_done = FAR_WIDTH * far_trips
    has_far = jnp.minimum(far_tiles, 1)
    m_i, acc_s = lax.fori_loop(0, has_far, lambda _, c: accumulate(far_done, sa_scr[...], *c, m_t), (m_i, acc_s))
    start = far_done + has_far
    near_pairs = (last - start) // 2
    m_i, acc_s, m_t = run_pipelined(start, near_pairs, 2, True, m_i, acc_s)
    m_i, acc_s = accumulate(start + 2 * near_pairs, sa_scr[...], m_i, acc_s, m_t)
    _, acc_s = lax.fori_loop(start + 2 * near_pairs + 1, last + 1, single, (m_i, acc_s))
    o_s = acc_s[0:HD] / jnp.maximum(acc_s[HD:HD + 1], TINY)

    gates = jax.nn.sigmoid(g_ref[0, 0])
    o_t = gates[0:1] * o_c + gates[1:2] * o_s + gates[2:3] * o_w
    for pair in range(A_G // 2):
        both = jnp.concatenate([o_t[:, (2 * pair) * BLK:(2 * pair + 1) * BLK],
                                o_t[:, (2 * pair + 1) * BLK:(2 * pair + 2) * BLK]], axis=0)
        o_ref[:, pair * LANES:(pair + 1) * LANES] = both.T


def _nsa(proj_b, kc, vct, key_aug, vst, vwt, gates, tb_a, bc, seq):
    nb = seq // BLK
    ncmp = seq // CMP_STRIDE
    nsel = seq // SEL_BLK
    tk = SEL_TILE
    full = lambda h, n: (h, 0, 0)
    wq = A_G * HD
    resident = lambda name: pl.BlockSpec((seq, LANES), lambda h, n: (0, _BCOLS[name][0] // LANES))
    return pl.pallas_call(
        functools.partial(_nsa_kernel, seq=seq, tk=tk),
        grid=(A_KV, nb),
        in_specs=[pl.BlockSpec((BLK, wq), lambda h, n: (n, _BCOLS["a_q"][0] // wq + h)),
                  pl.BlockSpec((1, ncmp, AUG_K), full),
                  pl.BlockSpec((1, HD, ncmp), full),
                  resident("a_ks"),
                  pl.BlockSpec((seq, LANES), lambda h, n: (0, 0)),
                  pl.BlockSpec((1, V_ROWS, seq), full),
                  resident("a_kw"),
                  pl.BlockSpec((1, HD, seq), full),
                  pl.BlockSpec((1, 1, 3, A_G * BLK), lambda h, n: (h, n, 0, 0)),
                  pl.BlockSpec((A_G, N_TAB, BLK, LANES), lambda h, n: (h, 0, 0, 0)),
                  pl.BlockSpec((A_G, 1, CMP_TAB, LANES), lambda h, n: (h, 0, 0, 0))],
        out_specs=pl.BlockSpec((BLK, A_G * HD), lambda h, n: (n, h)),
        out_shape=jax.ShapeDtypeStruct((seq, A_HEADS * HD), F32),
        scratch_shapes=[pltpu.VMEM((ncmp, A_G * BLK), F32), pltpu.VMEM((nsel, BLK), F32),
                        pltpu.VMEM((AUG_K, A_G * BLK), MXU_DTYPE),
                        *[pltpu.VMEM((tk, A_G * BLK), F32) for _ in range(FAR_WIDTH)],
                        pltpu.VMEM((ncmp + 8, BLK), F32)],
        compiler_params=_cparams(("arbitrary", "arbitrary")),
        name="nsa",
    )(proj_b, kc, vct, proj_b, key_aug, vst, proj_b, vwt, gates, tb_a, bc)


def _ssd_kernel(xc_ref, xp_ref, dt_ref, z_ref, cw_ref, cb_ref, dtb_ref, al_ref, d_ref, nw_ref, o_ref,
                xpad_scr, h_scr):
    c = pl.program_id(0)
    L = B_CHUNK

    @pl.when(c == 0)
    def _():
        h_scr[...] = jnp.zeros_like(h_scr)

    xpad_scr[0:8, :] = jnp.where(c > 0, xp_ref[L - 8:L, :], 0.0)
    xpad_scr[8:8 + L, :] = xc_ref[...]
    acc = cw_ref[0:1, :] * xpad_scr[pl.ds(8 - (B_CONV - 1), L), :]
    for i in range(1, B_CONV):
        acc = acc + cw_ref[i:i + 1, :] * xpad_scr[pl.ds(8 - (B_CONV - 1) + i, L), :]
    xa = _silu(acc + cb_ref[...])
    xs = xa[:, :B_INNER]
    bm = xa[:, B_INNER:B_INNER + B_GROUPS * B_STATE]
    cm = xa[:, B_INNER + B_GROUPS * B_STATE:]

    raw = dt_ref[...] + dtb_ref[...]
    dt = jnp.maximum(raw, 0.0) + jnp.log1p(jnp.exp(-jnp.abs(raw)))
    adt = dt * (-jnp.exp(al_ref[...]))
    ri = lax.broadcasted_iota(jnp.int32, (L, L), 0)
    ci = lax.broadcasted_iota(jnp.int32, (L, L), 1)
    tri = _mx(jnp.where(ri >= ci, 1.0, 0.0))
    acs = _dot_f32_lhs_exact(tri, adt)
    acs_t = acs.T
    er = lax.broadcasted_iota(jnp.int32, (LANES, B_INNER), 0)
    ec = lax.broadcasted_iota(jnp.int32, (LANES, B_INNER), 1)
    expand = _mx(jnp.where(ec // B_HEADDIM == er, 1.0, 0.0))
    dt_e = _dot_f32_rhs_exact(dt, expand)
    ea_e = _dot_f32_rhs_exact(jnp.exp(acs), expand)
    we_e = _dot_f32_rhs_exact(jnp.exp(acs[L - 1:L, :] - acs), expand)
    xdt = xs * dt_e
    xw = xdt * we_e
    lane = lax.broadcasted_iota(jnp.int32, (L, LANES), 1)
    hpg = B_HEADS // B_GROUPS
    gw = hpg * B_HEADDIM
    y_groups = []
    for g in range(B_GROUPS):
        b_g = bm[:, g * B_STATE:(g + 1) * B_STATE]
        c_g = _mx(cm[:, g * B_STATE:(g + 1) * B_STATE])
        cb = _dot_nt(c_g, _mx(b_g))
        pairs = []
        for pr in range(hpg // 2):
            x_pair = _mx(xdt[:, g * gw + pr * LANES:g * gw + (pr + 1) * LANES])
            halves = []
            for j in range(2):
                h = g * hpg + 2 * pr + j
                diff = acs[:, h:h + 1] - acs_t[h:h + 1, :]
                decay = jnp.exp(jnp.where(ri >= ci, diff, NEG_INF))
                halves.append(_dot(_mx(cb * decay), x_pair))
            pairs.append(jnp.where(lane < B_HEADDIM, halves[0], halves[1]))
        y_diag = jnp.concatenate(pairs, axis=1)
        h_prev = h_scr[:, g * gw:(g + 1) * gw]
        y_off = _dot(c_g, _mx(h_prev)) * ea_e[:, g * gw:(g + 1) * gw]
        st = _dot(_mx(b_g.T), _mx(xw[:, g * gw:(g + 1) * gw]))
        h_scr[:, g * gw:(g + 1) * gw] = h_prev * ea_e[L - 1:L, g * gw:(g + 1) * gw] + st
        y = y_diag + y_off + xs[:, g * gw:(g + 1) * gw] * d_ref[:, g * gw:(g + 1) * gw]
        yz = y * _silu(z_ref[:, g * gw:(g + 1) * gw])
        ms = jnp.mean(yz * yz, axis=-1, keepdims=True)
        y_groups.append(yz * lax.rsqrt(ms + EPS) * nw_ref[:, g * gw:(g + 1) * gw])
    o_ref[...] = jnp.concatenate(y_groups, axis=1)


def _ssd(proj, conv_w, conv_b, dtb_row, al_row, d_row, nw_row):
    s = proj.shape[0]
    L = B_CHUNK
    cx = _FCOLS["b_xbc"][0] // B_CONV_DIM
    cdt = _FCOLS["b_dt"][0] // LANES
    cz = _FCOLS["b_z"][0] // B_INNER
    row = lambda w: pl.BlockSpec((1, w), lambda c: (0, 0))
    return pl.pallas_call(
        _ssd_kernel,
        grid=(s // L,),
        in_specs=[pl.BlockSpec((L, B_CONV_DIM), lambda c: (c, cx)),
                  pl.BlockSpec((L, B_CONV_DIM), lambda c: (jnp.maximum(c - 1, 0), cx)),
                  pl.BlockSpec((L, LANES), lambda c: (c, cdt)),
                  pl.BlockSpec((L, B_INNER), lambda c: (c, cz)),
                  pl.BlockSpec((B_CONV, B_CONV_DIM), lambda c: (0, 0)),
                  row(B_CONV_DIM), row(LANES), row(LANES), row(B_INNER), row(B_INNER)],
        out_specs=pl.BlockSpec((L, B_INNER), lambda c: (c, 0)),
        out_shape=jax.ShapeDtypeStruct((s, B_INNER), F32),
        scratch_shapes=[pltpu.VMEM((L + 8, B_CONV_DIM), F32), pltpu.VMEM((B_STATE, B_INNER), F32)],
        compiler_params=_cparams(("arbitrary",)),
        name="ssd",
    )(proj, proj, proj, proj, conv_w, conv_b, dtb_row, al_row, d_row, nw_row)


def _rope_kernel(q_ref, k_ref, pos_ref, inv_ref, qo_ref, ko_ref):
    ang = pos_ref[...].astype(F32) * inv_ref[...]
    cos, sin = jnp.cos(ang), jnp.sin(ang)
    lane = lax.broadcasted_iota(jnp.int32, ang.shape, 1)
    low = (lane & (HD - 1)) < HD // 2

    def rot(t):
        partner = jnp.where(low, -pltpu.roll(t, LANES - HD // 2, 1), pltpu.roll(t, HD // 2, 1))
        return t * cos + partner * sin

    for j in range(q_ref.shape[1] // LANES):
        qo_ref[:, j * LANES:(j + 1) * LANES] = _mx(rot(q_ref[:, j * LANES:(j + 1) * LANES]))
    ko_ref[...] = _mx(rot(k_ref[...]))


def _rope(proj, pos_col, inv_row):
    s = proj.shape[0]
    tm = min(512, s)
    wq = C_HEADS * HD
    return pl.pallas_call(
        _rope_kernel,
        grid=(s // tm,),
        in_specs=[pl.BlockSpec((tm, wq), lambda i: (i, _FCOLS["c_q"][0] // wq)),
                  pl.BlockSpec((tm, LANES), lambda i: (i, _FCOLS["c_k"][0] // LANES)),
                  pl.BlockSpec((tm, 1), lambda i: (i, 0)),
                  pl.BlockSpec((1, LANES), lambda i: (0, 0))],
        out_specs=[pl.BlockSpec((tm, wq), lambda i: (i, 0)), pl.BlockSpec((tm, LANES), lambda i: (i, 0))],
        out_shape=[jax.ShapeDtypeStruct((s, wq), MXU_DTYPE), jax.ShapeDtypeStruct((s, LANES), MXU_DTYPE)],
        compiler_params=_cparams(("arbitrary",)),
        name="rope",
    )(proj, proj, pos_col, inv_row)


def _pair_rows(q2, low):
    zero = jnp.zeros_like(q2)
    return jnp.concatenate([jnp.where(low, q2, zero), jnp.where(low, zero, q2)], axis=0)


def _band_kernel(*refs, dil, rows, max_dist, has_bias, has_sink, want_lse):
    refs = list(refs)
    q_ref, k_ref, kh_ref, v_ref, vh_ref = refs[:5]
    pos = 5
    tb_ref = sk_ref = None
    if has_bias:
        tb_ref = refs[pos]
        pos += 1
    if has_sink:
        sk_ref = refs[pos]
        pos += 1
    o_ref = refs[pos]
    lse_ref = refs[pos + 1] if want_lse else None
    j = pl.program_id(0)
    i = pl.program_id(1)
    span = BLK * dil
    qi = lax.broadcasted_iota(jnp.int32, (BLK, 2 * BLK), 0)
    kj = lax.broadcasted_iota(jnp.int32, (BLK, 2 * BLK), 1)
    dist = BLK + qi - kj
    band = jnp.where((dist >= 0) & (dist <= max_dist), 0.0, NEG_INF)
    first = jnp.where(i > 0, band, jnp.where(kj >= BLK, band, NEG_INF))
    low = lax.broadcasted_iota(jnp.int32, (BLK, LANES), 1) < HD
    bias = None
    if has_bias:
        bias = jnp.concatenate([jnp.concatenate([tb_ref[a, 0], tb_ref[a, 1]], axis=1) for a in range(2)], axis=0)
    sink = None
    if has_sink:
        sink = jnp.concatenate([jnp.full((BLK, 1), sk_ref[2 * j + a], F32) for a in range(2)], axis=0)

    def take(ref, start):
        return ref[pl.ds(start, BLK, stride=dil), :] if dil > 1 else ref[start:start + BLK, :]

    def put(ref, start, val):
        if dil > 1:
            ref[pl.ds(start, BLK, stride=dil), :] = val
        else:
            ref[start:start + BLK, :] = val

    for sb in range(rows // span):
        for r in range(dil):
            base = sb * span + r
            qs = _mx(_pair_rows(take(q_ref, base).astype(F32), low))
            if sb == 0:
                k_prev, v_prev, mask = take(kh_ref, r), take(vh_ref, r), first
            else:
                k_prev, v_prev, mask = take(k_ref, base - span), take(v_ref, base - span), band
            k_win = _mx(jnp.concatenate([k_prev, take(k_ref, base)], axis=0))
            v_win = _mx(jnp.concatenate([v_prev, take(v_ref, base)], axis=0))
            s = _dot_nt(qs, k_win) + jnp.concatenate([mask, mask], axis=0)
            if has_bias:
                s = s + bias
            m = jnp.max(s, axis=-1, keepdims=True)
            if has_sink:
                m = jnp.maximum(m, sink)
            e = jnp.exp(s - m)
            den = jnp.sum(e, axis=-1, keepdims=True)
            if has_sink:
                den = den + jnp.exp(sink - m)
            o = _dot(_mx(e), v_win) / den
            put(o_ref, base, jnp.where(low, o[0:BLK], o[BLK:]))
            if want_lse:
                lse = jnp.broadcast_to(m + jnp.log(den), (2 * BLK, LANES))
                put(lse_ref, base, jnp.where(low, lse[0:BLK], lse[BLK:]))


def _band(q_arr, k_arr, v_arr, *, q_col, k_col, v_col, shared_kv, dil, max_dist, tb=None, sinks=None,
          want_lse=False):
    s = q_arr.shape[0]
    span = BLK * dil
    rows = max(min(2048, s), span)
    per = rows // span
    npair = A_HEADS * HD // LANES
    kv = (lambda j: 0) if shared_kv else (lambda j: j)
    in_specs = [pl.BlockSpec((rows, LANES), lambda j, i: (i, q_col + j)),
                pl.BlockSpec((rows, LANES), lambda j, i: (i, k_col + kv(j))),
                pl.BlockSpec((span, LANES), lambda j, i: (jnp.maximum(i * per - 1, 0), k_col + kv(j))),
                pl.BlockSpec((rows, LANES), lambda j, i: (i, v_col + kv(j))),
                pl.BlockSpec((span, LANES), lambda j, i: (jnp.maximum(i * per - 1, 0), v_col + kv(j)))]
    args = [q_arr, k_arr, k_arr, v_arr, v_arr]
    if tb is not None:
        in_specs.append(pl.BlockSpec((2, 2, BLK, LANES), lambda j, i: (j, 0, 0, 0)))
        args.append(tb)
    if sinks is not None:
        in_specs.append(pl.BlockSpec(memory_space=pltpu.SMEM))
        args.append(sinks)
    o_spec = pl.BlockSpec((rows, LANES), lambda j, i: (i, j))
    o_shape = jax.ShapeDtypeStruct((s, npair * LANES), F32)
    return pl.pallas_call(
        functools.partial(_band_kernel, dil=dil, rows=rows, max_dist=max_dist, has_bias=tb is not None,
                          has_sink=sinks is not None, want_lse=want_lse),
        grid=(npair, s // rows),
        in_specs=in_specs,
        out_specs=[o_spec, o_spec] if want_lse else o_spec,
        out_shape=[o_shape, o_shape] if want_lse else o_shape,
        compiler_params=_cparams(("arbitrary", "arbitrary")),
        name="band_attention",
    )(*args)


def _memkv_kernel(mem_ref, nw_ref, w_ref, o_ref):
    x = mem_ref[...]
    y = x * lax.rsqrt(jnp.mean(x * x, axis=-1, keepdims=True) + EPS) * nw_ref[...]
    o_ref[...] = _dot(_mx(y), w_ref[...])


def _memkv(mem2, norm_w, w):
    return pl.pallas_call(
        _memkv_kernel,
        out_shape=jax.ShapeDtypeStruct((mem2.shape[0], w.shape[1]), F32),
        compiler_params=pltpu.CompilerParams(vmem_limit_bytes=VMEM_LIMIT),
        name="mem_kv",
    )(mem2, norm_w, w)


def _memattn_kernel(q_ref, k_ref, v_ref, o_ref):
    tq = q_ref.shape[0]
    low = lax.broadcasted_iota(jnp.int32, (tq, LANES), 1) < HD
    for pr in range(M_HEADS // 2):
        cols = slice(pr * LANES, (pr + 1) * LANES)
        s = _dot_nt(_pair_rows(q_ref[:, cols], low), _mx(k_ref[:, cols]))
        e = jnp.exp(s - jnp.max(s, axis=-1, keepdims=True))
        o = _dot(_mx(e), _mx(v_ref[:, cols])) / jnp.sum(e, axis=-1, keepdims=True)
        o_ref[:, cols] = jnp.where(low, o[0:tq], o[tq:])


def _memattn(proj_b, kv):
    s = proj_b.shape[0]
    tq = min(256, s)
    ml = kv.shape[0]
    wm = M_HEADS * HD
    return pl.pallas_call(
        _memattn_kernel,
        grid=(s // tq,),
        in_specs=[pl.BlockSpec((tq, wm), lambda i: (i, _BCOLS["m_q"][0] // wm)),
                  pl.BlockSpec((ml, wm), lambda i: (0, 0)),
                  pl.BlockSpec((ml, wm), lambda i: (0, 1))],
        out_specs=pl.BlockSpec((tq, wm), lambda i: (i, 0)),
        out_shape=jax.ShapeDtypeStruct((s, wm), F32),
        compiler_params=_cparams(("arbitrary",)),
        name="mem_attention",
    )(proj_b, kv, kv)


def _outproj_kernel(x_ref, a_ref, az_ref, b_ref, c_ref, cz_ref, d0_ref, d1_ref, d2_ref, l0_ref, l1_ref, l2_ref,
                    dz_ref, m_ref, mz_ref, w_ref, nw_ref, o_ref):
    l0, l1, l2 = l0_ref[...], l1_ref[...], l2_ref[...]
    mx = jnp.maximum(jnp.maximum(l0, l1), l2)
    e0, e1, e2 = jnp.exp(l0 - mx), jnp.exp(l1 - mx), jnp.exp(l2 - mx)
    inv = 1.0 / (e0 + e1 + e2)
    d = (e0 * inv) * d0_ref[...] + (e1 * inv) * d1_ref[...] + (e2 * inv) * d2_ref[...]
    pieces = (a_ref[...] * _silu(az_ref[...]), b_ref[...], c_ref[...] * _silu(cz_ref[...]),
              d * _silu(dz_ref[...]), m_ref[...] * _silu(mz_ref[...]))
    y = None
    row = 0
    for piece in pieces:
        w = piece.shape[1]
        part = _dot(_mx(piece), w_ref[row:row + w, :])
        y = part if y is None else y + part
        row += w
    y = y * lax.rsqrt(jnp.mean(y * y, axis=-1, keepdims=True) + EPS) * nw_ref[...]
    o_ref[...] = x_ref[...] + y


def _outproj(x2, proj, a_out, b_out, c_out, d_outs, d_lses, m_out, w, norm_w):
    s = x2.shape[0]
    tm = min(512, s)
    wide = lambda width, col: pl.BlockSpec((tm, width), lambda i: (i, col))
    w512 = A_HEADS * HD
    wm = M_HEADS * HD
    return pl.pallas_call(
        _outproj_kernel,
        grid=(s // tm,),
        in_specs=[wide(D_MODEL, 0), wide(w512, 0), wide(w512, _FCOLS["a_z"][0] // w512), wide(w512, 0),
                  wide(w512, 0), wide(w512, _FCOLS["c_z"][0] // w512),
                  wide(w512, 0), wide(w512, 0), wide(w512, 0), wide(w512, 0), wide(w512, 0), wide(w512, 0),
                  wide(w512, _FCOLS["d_z"][0] // w512), wide(wm, 0), wide(wm, _FCOLS["m_z"][0] // wm),
                  pl.BlockSpec((MIX_WIDTH, D_MODEL), lambda i: (0, 0)),
                  pl.BlockSpec((1, D_MODEL), lambda i: (0, 0))],
        out_specs=wide(D_MODEL, 0),
        out_shape=jax.ShapeDtypeStruct((s, D_MODEL), F32),
        compiler_params=_cparams(("arbitrary",)),
        name="outproj",
    )(x2, a_out, proj, b_out, c_out, proj, *d_outs, *d_lses, proj, m_out, proj, w, norm_w)


def _c_heads(t, axis):
    shape = t.shape
    t = t.reshape(shape[:axis] + (C_HEADS, HD) + shape[axis + 1:])
    return jnp.take(t, jnp.array(_C_ORDER), axis=axis).reshape(shape)


def _repack_w_in(w, cols):
    out = []
    for name, (_, width) in cols.items():
        src, true_w, is_query = _SRC[name]
        piece = w[:, src:src + true_w]
        if is_query:
            piece = piece * HD ** -0.5
        if name in ("c_q", "c_z"):
            piece = _c_heads(piece, 1)
        if true_w < width:
            piece = jnp.pad(piece, ((0, 0), (0, width - true_w)))
        out.append(piece)
    return _mx(jnp.concatenate(out, axis=1))


def _heads_first_t(t, nh):
    s = t.shape[0]
    return jnp.transpose(t.reshape(s, nh, HD), (1, 2, 0))


def kernel(x, mem, positions, pre_norm, post_norm, w_in, w_out, rel_bias, a_cmp_pos, a_cmp_w1, a_cmp_w2,
           b_conv_w, b_conv_b, b_dt_bias, b_a_log, b_d, b_norm, c_sinks, m_norm, m_w_kv):
    b, s, _ = x.shape
    assert b == 1 and s % 2048 == 0, "sequence length must be a multiple of 2048 and batch 1"
    depth = w_in.shape[0]
    nb = s // BLK
    ncmp = s // CMP_STRIDE

    rel_a = rel_bias[:, :A_HEADS]
    rel_d = rel_bias[:, A_HEADS:]
    no_limit = 1 << 30
    c0_a = np.array([-2 * BLK] + [o * BLK for o in range(N_OFF)] + [A_WINDOW], np.int32)
    hi_a = np.array([no_limit] * (N_TAB - 1) + [A_WINDOW], np.int32)
    tb_a = _bias_table(rel_a, jnp.asarray(c0_a), a_row=-1, b_lane=1, scale=1, rows=BLK,
                       hi=jnp.asarray(hi_a))
    bc_a = _bias_table(rel_a, jnp.array([CMP_STRIDE * CMP_NEAR - (CMP_LEN - 1)], jnp.int32),
                       a_row=-CMP_STRIDE, b_lane=1, scale=1, rows=CMP_TAB)
    tb_d = [_bias_table(rel_d[:, p * D_SLOTS:(p + 1) * D_SLOTS], jnp.array([BLK, 0], jnp.int32), a_row=1,
                        b_lane=-1, scale=dil, rows=BLK) for p, (_, dil) in enumerate(D_PATTERNS)]
    key_pos = np.arange(s)
    key_aug_np = np.zeros((s, AUG_K - AUG_BASE), np.float32)
    key_aug_np[:, 0:2] = 1.0
    key_aug_np[key_pos, AUG_SEL - AUG_BASE + (key_pos // SEL_BLK) % (SEL_TILE // SEL_BLK)] = 1.0
    key_aug = jnp.asarray(key_aug_np, MXU_DTYPE)
    val_aug_np = np.zeros((V_ROWS - HD, s), np.float32)
    val_aug_np[0] = 1.0
    val_aug = jnp.asarray(val_aug_np, MXU_DTYPE)[None]
    head_slot = jnp.eye(A_KV, dtype=F32)
    half = ROPE_THETA ** (-jnp.arange(HD // 2, dtype=F32) / (HD // 2))
    inv_row = jnp.tile(half, LANES // (HD // 2)).reshape(1, LANES)
    pos_col = positions.reshape(s, 1)
    mem2 = mem.reshape(mem.shape[1], D_MODEL)

    x2 = x.reshape(s, D_MODEL)
    for layer in range(depth):
        norm_w = pre_norm[layer].reshape(1, D_MODEL)
        proj = _inproj(x2, norm_w, _repack_w_in(w_in[layer], _FCOLS), F32)
        proj_b = _inproj(x2, norm_w, _repack_w_in(w_in[layer], _BCOLS), MXU_DTYPE)
        fcol = lambda name: proj[:, _FCOLS[name][0]:_FCOLS[name][0] + _FCOLS[name][1]]
        bcol = lambda name: proj_b[:, _BCOLS[name][0]:_BCOLS[name][0] + _BCOLS[name][1]]

        wide = CMP_STRIDE * A_KV * HD
        u = jnp.stack([fcol("a_kc").reshape(ncmp, wide), fcol("a_vc").reshape(ncmp, wide)])
        pe = (a_cmp_pos[layer].reshape(2, 1, 2, CMP_STRIDE, 1, HD)
              * head_slot.reshape(1, A_KV, 1, 1, A_KV, 1)).reshape(2 * A_KV, 2, wide)
        w1 = _mx(a_cmp_w1[layer].reshape(2, 1, 2, CMP_STRIDE, 1, HD, CMP_HIDDEN)
                 * head_slot.reshape(1, A_KV, 1, 1, A_KV, 1, 1)).reshape(2 * A_KV, 2, wide, CMP_HIDDEN)
        cmp_out = _compress(u, pe, w1, _mx(a_cmp_w2[layer]), ncmp)
        kc_own = (cmp_out[:A_KV, :, None, :] * head_slot[:, None, :, None]).reshape(A_KV, ncmp, AUG_BASE)
        kc = _mx(jnp.concatenate([kc_own, jnp.ones((A_KV, ncmp, 2), F32),
                                  jnp.zeros((A_KV, ncmp, AUG_K - AUG_BASE - 2), F32)], axis=2))
        vct = _mx(jnp.transpose(cmp_out[A_KV:], (0, 2, 1)))
        gates = jnp.transpose(fcol("a_gate")[:, :3 * A_HEADS].reshape(nb, BLK, A_KV, A_G, 3),
                              (2, 0, 4, 3, 1)).reshape(A_KV, nb, 3, A_G * BLK)
        vst_aug = jnp.concatenate([_heads_first_t(bcol("a_vs"), A_KV),
                                   jnp.broadcast_to(val_aug, (A_KV, V_ROWS - HD, s))], axis=1)
        a_out = _nsa(proj_b, kc, vct, key_aug, vst_aug, _heads_first_t(bcol("a_vw"), A_KV), gates, tb_a, bc_a, s)

        pad_row = lambda v: jnp.pad(v, (0, LANES - v.shape[0])).reshape(1, LANES)
        b_out = _ssd(proj, b_conv_w[layer], b_conv_b[layer].reshape(1, B_CONV_DIM), pad_row(b_dt_bias[layer]),
                     pad_row(b_a_log[layer]), jnp.repeat(b_d[layer], B_HEADDIM).reshape(1, B_INNER),
                     b_norm[layer].reshape(1, B_INNER))

        cq_rot, ck_rot = _rope(proj, pos_col, inv_row)
        c_out = _band(cq_rot, ck_rot, proj_b, q_col=0, k_col=0, v_col=_BCOLS["c_v"][0] // LANES, shared_kv=True,
                      dil=1, max_dist=C_WINDOW - 1, sinks=c_sinks[layer][jnp.array(_C_ORDER)])

        d_outs, d_lses = [], []
        for p, (window, dil) in enumerate(D_PATTERNS):
            o_p, lse_p = _band(proj, proj, proj, q_col=_FCOLS["d_q"][0] // LANES + p * D_SLOTS * HD // LANES,
                               k_col=_FCOLS["d_k"][0] // LANES, v_col=_FCOLS["d_v"][0] // LANES, shared_kv=False,
                               dil=dil, max_dist=window // dil, tb=tb_d[p], want_lse=True)
            d_outs.append(o_p)
            d_lses.append(lse_p)

        kv = _memkv(mem2, m_norm[layer].reshape(1, D_MODEL), _mx(m_w_kv[layer]))
        m_out = _memattn(proj_b, kv)

        c_rows = slice(A_HEADS * HD + B_INNER, A_HEADS * HD + B_INNER + C_HEADS * HD)
        w_o = w_out[layer].at[c_rows].set(_c_heads(w_out[layer][c_rows], 0))
        x2 = _outproj(x2, proj, a_out, b_out, c_out, d_outs, d_lses, m_out, _mx(w_o),
                      post_norm[layer].reshape(1, D_MODEL))
    return x2.reshape(b, s, D_MODEL)
```

```python
import functools
import math

import numpy as np
import jax
import jax.numpy as jnp
from jax import lax
from jax.experimental import pallas as pl
from jax.experimental.pallas import tpu as pltpu

F32 = jnp.float32
MXU_DTYPE = jnp.bfloat16

D_MODEL = 1024
HD = 64
BLK = 128
NEG_INF = -1e30
TINY = 1e-30
EPS = 1e-6

A_HEADS, A_KV = 8, 2
A_G = A_HEADS // A_KV
CMP_LEN, CMP_STRIDE, CMP_HIDDEN = 32, 16, 128
SEL_BLK, SEL_TOPK, A_WINDOW, SEL_FORCE = 64, 16, 512, 1e4
B_HEADS, B_HEADDIM, B_GROUPS, B_STATE, B_CONV, B_CHUNK = 8, 64, 2, 128, 4, 128
B_INNER = B_HEADS * B_HEADDIM
B_CONV_DIM = B_INNER + 2 * B_GROUPS * B_STATE
C_HEADS, C_KV, C_WINDOW = 8, 2, 128
ROPE_THETA = 150000.0
D_SLOTS = 8
D_PATTERNS = ((128, 1), (512, 4), (2048, 16))
D_NPAT = 3
M_HEADS = 4
N_BUCKETS, MAX_DIST = 32, 2048
MIX_WIDTH = A_HEADS * HD + B_INNER + C_HEADS * HD + D_SLOTS * HD + M_HEADS * HD

LANES = 128
VMEM_LIMIT = 56 * 1024 * 1024

_SRC = {"a_q": (0, 512, True), "a_kc": (512, 128, False), "a_vc": (640, 128, False), "a_ks": (768, 128, False),
        "a_vs": (896, 128, False), "a_kw": (1024, 128, False), "a_vw": (1152, 128, False),
        "a_gate": (1280, 24, False), "a_z": (1304, 512, False), "b_xbc": (1816, 1024, False),
        "b_dt": (2840, 8, False), "b_z": (2848, 512, False), "c_q": (3360, 512, True), "c_k": (3872, 128, False),
        "c_v": (4000, 128, False), "c_z": (4128, 512, False), "d_q": (4640, 1536, True), "d_k": (6176, 512, False),
        "d_v": (6688, 512, False), "d_z": (7200, 512, False), "m_q": (7712, 256, True), "m_z": (7968, 256, False)}


def _layout(pieces):
    cols, off = {}, 0
    for name, width in pieces:
        cols[name] = (off, width)
        off += width
    return cols, off


_FCOLS, N_F32 = _layout((("b_xbc", 1024), ("a_z", 512), ("b_z", 512), ("c_z", 512), ("d_z", 512), ("c_q", 512),
                         ("d_q", 1536), ("d_k", 512), ("d_v", 512), ("m_z", 256), ("a_kc", 128), ("a_vc", 128),
                         ("c_k", 128), ("a_gate", 128), ("b_dt", 128)))
_BCOLS, N_BF16 = _layout((("a_q", 512), ("m_q", 256), ("a_ks", 128), ("a_vs", 128), ("a_kw", 128),
                          ("a_vw", 128), ("c_v", 128)))
PROJ_TN = 1408
_C_ORDER = tuple(h for j in range(C_HEADS // C_KV) for h in (j, j + C_HEADS // C_KV))


def _t5_thresholds():
    d = np.arange(0, 4 * MAX_DIST)
    exact = N_BUCKETS // 2
    rel = np.maximum(d, exact).astype(np.float64)
    large = exact + (np.log(rel / exact) / math.log(MAX_DIST / exact) * (N_BUCKETS - exact)).astype(np.int64)
    bucket = np.where(d < exact, d, np.minimum(large, N_BUCKETS - 1))
    return tuple(int(np.argmax(bucket >= b)) for b in range(1, N_BUCKETS))


_THR = _t5_thresholds()
FAR_DIST = _THR[-1]
N_OFF = -(-(FAR_DIST + BLK) // BLK) + 1
TAB_OFF0 = 1
TAB_EDGE = TAB_OFF0 + N_OFF
N_TAB = TAB_EDGE + 1
FAR_WIDTH = 4
CMP_REACH = -(-FAR_DIST // CMP_STRIDE) // 8 * 8 + 8
CMP_WIN = 128
CMP_NEAR = 128
CMP_TAB = CMP_NEAR + CMP_WIN
SEL_TILE = 512
AUG_BASE = LANES
AUG_SEL = AUG_BASE + 8
AUG_END = AUG_SEL + SEL_TILE // SEL_BLK
AUG_K = 2 * LANES
V_ROWS = HD + 16


def _cparams(sem):
    return pltpu.CompilerParams(dimension_semantics=sem, vmem_limit_bytes=VMEM_LIMIT)


def _mx(x):
    return x.astype(MXU_DTYPE)


def _dot(a, b):
    return jnp.dot(a, b, preferred_element_type=F32)


def _dot_nt(a, b):
    return lax.dot_general(a, b, (((1,), (1,)), ((), ())), preferred_element_type=F32)


def _split3(a):
    hi = _mx(a)
    r1 = a - hi.astype(F32)
    mid = _mx(r1)
    lo = _mx(r1 - mid.astype(F32))
    return hi, mid, lo


def _dot_f32_rhs_exact(a, b):
    hi, mid, lo = _split3(a)
    return _dot(hi, b) + _dot(mid, b) + _dot(lo, b)


def _dot_f32_lhs_exact(a, b):
    hi, mid, lo = _split3(b)
    return _dot(a, hi) + _dot(a, mid) + _dot(a, lo)


def _silu(x):
    return x * jax.nn.sigmoid(x)


def _softmax_cols(s, mask, sink=None):
    s = jnp.where(mask, s, NEG_INF)
    m = jnp.max(s, axis=0, keepdims=True)
    if sink is not None:
        m = jnp.maximum(m, sink)
    p = jnp.where(mask, jnp.exp(s - m), 0.0)
    den = jnp.sum(p, axis=0, keepdims=True)
    if sink is not None:
        den = den + jnp.exp(sink - m)
    return p, den, m


def _bias_table_kernel(rel_ref, c0_ref, hi_ref, o_ref, *, a_row, b_lane, scale, rows, masked):
    h = pl.program_id(0)
    t = pl.program_id(1)
    r = lax.broadcasted_iota(jnp.int32, (rows, LANES), 0)
    l = lax.broadcasted_iota(jnp.int32, (rows, LANES), 1)
    dist = (c0_ref[t] + a_row * r + b_lane * l) * scale
    out = jnp.full((rows, LANES), rel_ref[0, h], F32)
    for b in range(1, N_BUCKETS):
        out = jnp.where(dist >= _THR[b - 1], rel_ref[b, h], out)
    if masked:
        out = jnp.where(dist < 0, NEG_INF, jnp.where(dist >= hi_ref[t], NEG_INF, out))
    o_ref[0, 0] = out


def _bias_table(rel_cols, c0, *, a_row, b_lane, scale, rows, hi=None):
    nh = rel_cols.shape[1]
    nt = c0.shape[0]
    return pl.pallas_call(
        functools.partial(_bias_table_kernel, a_row=a_row, b_lane=b_lane, scale=scale, rows=rows,
                          masked=hi is not None),
        grid_spec=pltpu.PrefetchScalarGridSpec(
            num_scalar_prefetch=3, grid=(nh, nt), in_specs=[],
            out_specs=pl.BlockSpec((1, 1, rows, LANES), lambda h, t, *_: (h, t, 0, 0))),
        out_shape=jax.ShapeDtypeStruct((nh, nt, rows, LANES), F32),
        compiler_params=_cparams(("arbitrary", "arbitrary")),
        name="bias_table",
    )(rel_cols, c0, c0 if hi is None else hi)


def _inproj_kernel(x_ref, nw_ref, w_ref, o_ref, h_scr):
    @pl.when(pl.program_id(1) == 0)
    def _():
        x = x_ref[...]
        y = x * lax.rsqrt(jnp.mean(x * x, axis=-1, keepdims=True) + EPS)
        h_scr[...] = _mx(y * nw_ref[...])

    o_ref[...] = _dot(h_scr[...], w_ref[...]).astype(o_ref.dtype)


def _inproj(x2, norm_w, w, out_dtype):
    s = x2.shape[0]
    tm, tn = min(1024, s), PROJ_TN
    return pl.pallas_call(
        _inproj_kernel,
        grid=(s // tm, w.shape[1] // tn),
        in_specs=[pl.BlockSpec((tm, D_MODEL), lambda i, j: (i, 0)),
                  pl.BlockSpec((1, D_MODEL), lambda i, j: (0, 0)),
                  pl.BlockSpec((D_MODEL, tn), lambda i, j: (0, j))],
        out_specs=pl.BlockSpec((tm, tn), lambda i, j: (i, j)),
        out_shape=jax.ShapeDtypeStruct((s, w.shape[1]), out_dtype),
        scratch_shapes=[pltpu.VMEM((tm, D_MODEL), MXU_DTYPE)],
        compiler_params=_cparams(("arbitrary", "arbitrary")),
        name="inproj",
    )(x2, norm_w, w)


def _compress_kernel(u_ref, pe_ref, w1_ref, w2_ref, o_ref, h2_scr, *, ncmp):
    u = u_ref[0]
    pe_first, pe_second = pe_ref[0, 0:1, :], pe_ref[0, 1:2, :]
    h1 = _dot(_mx(u + pe_first), w1_ref[0, 0])
    h2_scr[0:ncmp, :] = _dot(_mx(u + pe_second), w1_ref[0, 1])
    h2_scr[ncmp:ncmp + 8, :] = _dot(_mx(jnp.broadcast_to(pe_second, (8, pe_second.shape[1]))), w1_ref[0, 1])
    pre = h1 + h2_scr[pl.ds(1, ncmp), :]
    o_ref[0] = _dot(_mx(jax.nn.gelu(pre)), w2_ref[0])


def _compress(u, pe, w1, w2, ncmp):
    wide = CMP_STRIDE * A_KV * HD
    return pl.pallas_call(
        functools.partial(_compress_kernel, ncmp=ncmp),
        grid=(2 * A_KV,),
        in_specs=[pl.BlockSpec((1, ncmp, wide), lambda i: (i // A_KV, 0, 0)),
                  pl.BlockSpec((1, 2, wide), lambda i: (i, 0, 0)),
                  pl.BlockSpec((1, 2, wide, CMP_HIDDEN), lambda i: (i, 0, 0, 0)),
                  pl.BlockSpec((1, CMP_HIDDEN, HD), lambda i: (i // A_KV, 0, 0))],
        out_specs=pl.BlockSpec((1, ncmp, HD), lambda i: (i, 0, 0)),
        out_shape=jax.ShapeDtypeStruct((2 * A_KV, ncmp, HD), F32),
        scratch_shapes=[pltpu.VMEM((ncmp + 8, CMP_HIDDEN), F32)],
        compiler_params=_cparams(("arbitrary",)),
        name="nsa_compress",
    )(u, pe, w1, w2)


def _nsa_kernel(q_ref, kc_ref, vct_ref, ks_ref, ka_ref, vst_ref, kw_ref, vwt_ref, g_ref, tb_ref, bc_ref,
                o_ref, s_scr, sel_scr, qa_scr, sa_scr, sb_scr, sc_scr, sd_scr, pcs_scr, qb_scr, qc_scr, qd_scr,
                *, seq, tk):
    hk = pl.program_id(0)
    n = pl.program_id(1)
    ncmp = seq // CMP_STRIDE
    nsel = seq // SEL_BLK
    n_top = min(SEL_TOPK, nsel)
    r_all = A_G * BLK
    q_rows = q_ref[...].astype(F32)
    halves = [q_rows[:, p * LANES:(p + 1) * LANES].T for p in range(A_G // 2)]
    q_t = _mx(jnp.concatenate([h[r * HD:(r + 1) * HD] for h in halves for r in range(2)], axis=1))
    far_row = jnp.concatenate([tb_ref[g, TAB_OFF0 + N_OFF - 1, 0:1, :] for g in range(A_G)], axis=1)

    def tile_bias(o, edge=None):
        idx = jnp.clip(o, -1, N_OFF - 1) + TAB_OFF0
        if edge is not None:
            idx = jnp.where(o == edge, TAB_EDGE, idx)
        return jnp.concatenate([tb_ref[g, idx] for g in range(A_G)], axis=1)

    far_hi = _mx(far_row).astype(F32)
    bias_rows = jnp.concatenate([far_hi, far_row - far_hi, jnp.zeros((AUG_SEL - AUG_BASE - 2, r_all), F32)], axis=0)
    slots = ((sa_scr, qa_scr), (sb_scr, qb_scr), (sc_scr, qc_scr), (sd_scr, qd_scr))
    for _, q_scr in slots:
        q_scr[0:AUG_BASE, :] = jnp.zeros((AUG_BASE, r_all), MXU_DTYPE)
        q_scr[pl.ds(pl.multiple_of(hk * HD, HD), HD), :] = q_t
        q_scr[AUG_BASE:AUG_END, :] = _mx(
            jnp.concatenate([bias_rows, jnp.zeros((AUG_END - AUG_SEL, r_all), F32)], axis=0))
        q_scr[AUG_END:, :] = jnp.zeros((AUG_K - AUG_END, r_all), MXU_DTYPE)
    q_own = qa_scr[0:AUG_BASE, :]

    nwb = A_WINDOW // BLK + 1
    kb0 = jnp.maximum(n - (nwb - 1), 0)
    k0 = pl.multiple_of(kb0 * BLK, BLK)
    s = _dot(kw_ref[pl.ds(k0, nwb * BLK), :], q_own)
    s = s + jnp.concatenate([tile_bias(n - kb0 - i, edge=nwb - 1) for i in range(nwb)], axis=0)
    e = jnp.exp(s - jnp.max(s, axis=0, keepdims=True))
    o_w = _dot(vwt_ref[0, :, pl.ds(k0, nwb * BLK)], _mx(e)) / jnp.sum(e, axis=0, keepdims=True)

    per = SEL_BLK // CMP_STRIDE
    tok = BLK // CMP_STRIDE
    cmp_step = max(ncmp // 4, CMP_WIN)

    def compressed(rows):
        def run():
            s_scr[0:rows, :] = _dot(kc_ref[0, 0:rows, :], qa_scr[...])
            st = jnp.clip((tok * n - CMP_REACH) // 16 * 16, 0, rows - CMP_WIN)
            off = pl.multiple_of(st - tok * n + CMP_NEAR, 8)
            st = pl.multiple_of(st, 16)
            near = jnp.concatenate([bc_ref[g, 0, pl.ds(off, CMP_WIN), :] for g in range(A_G)], axis=1)
            s_scr[pl.ds(st, CMP_WIN), :] = _dot(kc_ref[0, pl.ds(st, CMP_WIN), 0:AUG_BASE], q_own) + near
            jrow = lax.broadcasted_iota(jnp.int32, (rows, BLK), 0)
            qpos = n * BLK + lax.broadcasted_iota(jnp.int32, (rows, BLK), 1)
            visible = jnp.where(jrow * CMP_STRIDE + (CMP_LEN - 1) <= qpos, 0.0, NEG_INF)
            s_c = s_scr[0:rows, :] + jnp.concatenate([visible] * A_G, axis=1)
            m = jnp.max(s_c, axis=0, keepdims=True)
            e = jnp.exp(s_c - m)
            den = jnp.sum(e, axis=0, keepdims=True)
            inv = jnp.where(m > 0.5 * NEG_INF, 1.0 / jnp.maximum(den, TINY), 0.0)
            p_c = e * inv
            o_c = _dot(vct_ref[0, :, 0:rows], _mx(p_c))
            pcs = p_c[:, 0:BLK]
            for g in range(1, A_G):
                pcs = pcs + p_c[:, g * BLK:(g + 1) * BLK]
            pcs_scr[0:8, :] = jnp.zeros((8, BLK), F32)
            pcs_scr[8:8 + rows, :] = pcs
            part = [pcs_scr[pl.ds(8 + i, rows // per, stride=per), :] for i in range(-1, per)]
            imp = part[1] + part[0]
            for i in range(1, per):
                imp = imp + (part[i + 1] + part[i])
            if rows < ncmp:
                imp = jnp.concatenate([imp, jnp.zeros((nsel - rows // per, BLK), F32)], axis=0)
            return o_c, imp
        return run

    variants = [compressed(r) for r in range(cmp_step, ncmp + 1, cmp_step)]
    o_c, imp = lax.switch((tok * n + tok - 1) // cmp_step, variants)

    b_io = lax.broadcasted_iota(jnp.int32, (nsel, BLK), 0)
    b_f = b_io.astype(F32)
    pos = n * BLK + lax.broadcasted_iota(jnp.int32, (nsel, BLK), 1)
    forced = (b_io == pos // SEL_BLK) | (b_io == 0)
    valid = b_io * SEL_BLK <= pos
    imp = jnp.where(forced, -2.0, jnp.where(valid, imp, -1.0))

    def pick_one(_, carry):
        imp, sel = carry
        vals = [imp[8 * g:8 * (g + 1)] for g in range(nsel // 8)]
        idxs = [b_f[8 * g:8 * (g + 1)] for g in range(nsel // 8)]
        while len(vals) > 1:
            later = [vals[a + 1] > vals[a] for a in range(0, len(vals), 2)]
            idxs = [jnp.where(c, idxs[2 * a + 1], idxs[2 * a]) for a, c in enumerate(later)]
            vals = [jnp.where(c, vals[2 * a + 1], vals[2 * a]) for a, c in enumerate(later)]
        v, ix = vals[0], idxs[0]
        top = jnp.max(v, axis=0, keepdims=True)
        first = jnp.min(jnp.where(v == top, ix, float(nsel)), axis=0, keepdims=True)
        hit = b_f == first
        sel = jnp.where(hit, jnp.where(top >= 0.0, 0.0, NEG_INF), sel)
        return jnp.where(hit, -2.0, imp), sel

    _, sel = lax.fori_loop(0, n_top - 2, pick_one, (imp, jnp.where(forced, 0.0, NEG_INF)))
    sel_scr[...] = sel

    nblk_t = tk // BLK
    no_bias_rows = jnp.zeros((AUG_SEL - AUG_BASE, r_all), F32)

    def scores(t, const_rows, q_scr):
        k0 = pl.multiple_of(t * tk, tk)
        picked = sel_scr[pl.ds(pl.multiple_of((AUG_END - AUG_SEL) * t, 8), AUG_END - AUG_SEL), :]
        q_scr[AUG_BASE:AUG_END, :] = _mx(
            jnp.concatenate([const_rows, jnp.concatenate([picked] * A_G, axis=1)], axis=0))
        keys = jnp.concatenate([ks_ref[pl.ds(k0, tk), :], ka_ref[pl.ds(k0, tk), :]], axis=1)
        return _dot(keys, q_scr[...])

    last = (n * BLK) // tk

    def tile_scores(t, near, q_scr):
        if not near:
            return scores(t, bias_rows, q_scr)
        bias = jnp.concatenate([tile_bias(n - (nblk_t * t + i)) for i in range(nblk_t)], axis=0)
        return scores(t, no_bias_rows, q_scr) + bias

    def accumulate(t, s, m_i, acc, m_tile):
        k0 = pl.multiple_of(t * tk, tk)
        m_new = jnp.maximum(m_i, m_tile)
        p = jnp.exp(s - m_new)
        acc = acc * jnp.exp(m_i - m_new) + _dot(vst_ref[0, :, pl.ds(k0, tk)], _mx(p))
        return m_new, acc

    def stage1(t, slot, near):
        buf, q_scr = slot
        s = tile_scores(t, near, q_scr)
        buf[...] = s
        return jnp.max(s, axis=0, keepdims=True)

    def run_pipelined(first, trips, width, near, m_i, acc):
        def trip(u, carry):
            m_i, acc, m_cur = carry
            t0 = first + width * u
            for k in range(width):
                m_next = stage1(t0 + k + 1, slots[(k + 1) % width], near)
                m_i, acc = accumulate(t0 + k, slots[k][0][...], m_i, acc, m_cur)
                m_cur = m_next
            return m_i, acc, m_cur

        return lax.fori_loop(0, trips, trip, (m_i, acc, stage1(first, slots[0], near)))

    def single(t, carry):
        s = tile_scores(t, True, qa_scr)
        return accumulate(t, s, *carry, jnp.max(s, axis=0, keepdims=True))

    far_tiles = jnp.maximum(n - (N_OFF - 2), 0) // nblk_t
    far_trips = jnp.maximum(far_tiles - 1, 0) // FAR_WIDTH
    m_i, acc_s, m_t = run_pipelined(0, far_trips, FAR_WIDTH, False, jnp.full((1, r_all), NEG_INF, F32),
                                    jnp.zeros((V_ROWS, r_all), F32))
    far_done = FAR_WIDTH * far_trips
    has_far = jnp.minimum(far_tiles, 1)
    m_i, acc_s = lax.fori_loop(0, has_far, lambda _, c: accumulate(far_done, sa_scr[...], *c, m_t), (m_i, acc_s))
    start = far_done + has_far
    near_pairs = (last - start) // 2
    m_i, acc_s, m_t = run_pipelined(start, near_pairs, 2, True, m_i, acc_s)
    m_i, acc_s = accumulate(start + 2 * near_pairs, sa_scr[...], m_i, acc_s, m_t)
    _, acc_s = lax.fori_loop(start + 2 * near_pairs + 1, last + 1, single, (m_i, acc_s))
    o_s = acc_s[0:HD] / jnp.maximum(acc_s[HD:HD + 1], TINY)

    gates = jax.nn.sigmoid(g_ref[0, 0])
    o_t = gates[0:1] * o_c + gates[1:2] * o_s + gates[2:3] * o_w
    for pair in range(A_G // 2):
        both = jnp.concatenate([o_t[:, (2 * pair) * BLK:(2 * pair + 1) * BLK],
                                o_t[:, (2 * pair + 1) * BLK:(2 * pair + 2) * BLK]], axis=0)
        o_ref[:, pair * LANES:(pair + 1) * LANES] = both.T


def _nsa(proj_b, kc, vct, key_aug, vst, vwt, gates, tb_a, bc, seq):
    nb = seq // BLK
    ncmp = seq // CMP_STRIDE
    nsel = seq // SEL_BLK
    tk = SEL_TILE
    full = lambda h, n: (h, 0, 0)
    wq = A_G * HD
    resident = lambda name: pl.BlockSpec((seq, LANES), lambda h, n: (0, _BCOLS[name][0] // LANES))
    return pl.pallas_call(
        functools.partial(_nsa_kernel, seq=seq, tk=tk),
        grid=(A_KV, nb),
        in_specs=[pl.BlockSpec((BLK, wq), lambda h, n: (n, _BCOLS["a_q"][0] // wq + h)),
                  pl.BlockSpec((1, ncmp, AUG_K), full),
                  pl.BlockSpec((1, HD, ncmp), full),
                  resident("a_ks"),
                  pl.BlockSpec((seq, LANES), lambda h, n: (0, 0)),
                  pl.BlockSpec((1, V_ROWS, seq), full),
                  resident("a_kw"),
                  pl.BlockSpec((1, HD, seq), full),
                  pl.BlockSpec((1, 1, 3, A_G * BLK), lambda h, n: (h, n, 0, 0)),
                  pl.BlockSpec((A_G, N_TAB, BLK, LANES), lambda h, n: (h, 0, 0, 0)),
                  pl.BlockSpec((A_G, 1, CMP_TAB, LANES), lambda h, n: (h, 0, 0, 0))],
        out_specs=pl.BlockSpec((BLK, A_G * HD), lambda h, n: (n, h)),
        out_shape=jax.ShapeDtypeStruct((seq, A_HEADS * HD), F32),
        scratch_shapes=[pltpu.VMEM((ncmp, A_G * BLK), F32), pltpu.VMEM((nsel, BLK), F32),
                        pltpu.VMEM((AUG_K, A_G * BLK), MXU_DTYPE),
                        *[pltpu.VMEM((tk, A_G * BLK), F32) for _ in range(FAR_WIDTH)],
                        pltpu.VMEM((ncmp + 8, BLK), F32),
                        *[pltpu.VMEM((AUG_K, A_G * BLK), MXU_DTYPE) for _ in range(FAR_WIDTH - 1)]],
        compiler_params=_cparams(("arbitrary", "arbitrary")),
        name="nsa",
    )(proj_b, kc, vct, proj_b, key_aug, vst, proj_b, vwt, gates, tb_a, bc)


def _ssd_kernel(xc_ref, xp_ref, dt_ref, z_ref, cw_ref, cb_ref, dtb_ref, al_ref, d_ref, nw_ref, o_ref,
                xpad_scr, h_scr):
    c = pl.program_id(0)
    L = B_CHUNK

    @pl.when(c == 0)
    def _():
        h_scr[...] = jnp.zeros_like(h_scr)

    xpad_scr[0:8, :] = jnp.where(c > 0, xp_ref[L - 8:L, :], 0.0)
    xpad_scr[8:8 + L, :] = xc_ref[...]
    acc = cw_ref[0:1, :] * xpad_scr[pl.ds(8 - (B_CONV - 1), L), :]
    for i in range(1, B_CONV):
        acc = acc + cw_ref[i:i + 1, :] * xpad_scr[pl.ds(8 - (B_CONV - 1) + i, L), :]
    xa = _silu(acc + cb_ref[...])
    xs = xa[:, :B_INNER]
    bm = xa[:, B_INNER:B_INNER + B_GROUPS * B_STATE]
    cm = xa[:, B_INNER + B_GROUPS * B_STATE:]

    raw = dt_ref[...] + dtb_ref[...]
    dt = jnp.maximum(raw, 0.0) + jnp.log1p(jnp.exp(-jnp.abs(raw)))
    adt = dt * (-jnp.exp(al_ref[...]))
    ri = lax.broadcasted_iota(jnp.int32, (L, L), 0)
    ci = lax.broadcasted_iota(jnp.int32, (L, L), 1)
    tri = _mx(jnp.where(ri >= ci, 1.0, 0.0))
    acs = _dot_f32_lhs_exact(tri, adt)
    acs_t = acs.T
    er = lax.broadcasted_iota(jnp.int32, (LANES, B_INNER), 0)
    ec = lax.broadcasted_iota(jnp.int32, (LANES, B_INNER), 1)
    expand = _mx(jnp.where(ec // B_HEADDIM == er, 1.0, 0.0))
    dt_e = _dot_f32_rhs_exact(dt, expand)
    ea_e = _dot_f32_rhs_exact(jnp.exp(acs), expand)
    we_e = _dot_f32_rhs_exact(jnp.exp(acs[L - 1:L, :] - acs), expand)
    xdt = xs * dt_e
    xw = xdt * we_e
    lane = lax.broadcasted_iota(jnp.int32, (L, LANES), 1)
    hpg = B_HEADS // B_GROUPS
    gw = hpg * B_HEADDIM
    y_groups = []
    for g in range(B_GROUPS):
        b_g = bm[:, g * B_STATE:(g + 1) * B_STATE]
        c_g = _mx(cm[:, g * B_STATE:(g + 1) * B_STATE])
        cb = _dot_nt(c_g, _mx(b_g))
        pairs = []
        for pr in range(hpg // 2):
            x_pair = _mx(xdt[:, g * gw + pr * LANES:g * gw + (pr + 1) * LANES])
            halves = []
            for j in range(2):
                h = g * hpg + 2 * pr + j
                diff = acs[:, h:h + 1] - acs_t[h:h + 1, :]
                decay = jnp.exp(jnp.where(ri >= ci, diff, NEG_INF))
                halves.append(_dot(_mx(cb * decay), x_pair))
            pairs.append(jnp.where(lane < B_HEADDIM, halves[0], halves[1]))
        y_diag = jnp.concatenate(pairs, axis=1)
        h_prev = h_scr[:, g * gw:(g + 1) * gw]
        y_off = _dot(c_g, _mx(h_prev)) * ea_e[:, g * gw:(g + 1) * gw]
        st = _dot(_mx(b_g.T), _mx(xw[:, g * gw:(g + 1) * gw]))
        h_scr[:, g * gw:(g + 1) * gw] = h_prev * ea_e[L - 1:L, g * gw:(g + 1) * gw] + st
        y = y_diag + y_off + xs[:, g * gw:(g + 1) * gw] * d_ref[:, g * gw:(g + 1) * gw]
        yz = y * _silu(z_ref[:, g * gw:(g + 1) * gw])
        ms = jnp.mean(yz * yz, axis=-1, keepdims=True)
        y_groups.append(yz * lax.rsqrt(ms + EPS) * nw_ref[:, g * gw:(g + 1) * gw])
    o_ref[...] = jnp.concatenate(y_groups, axis=1)


def _ssd(proj, conv_w, conv_b, dtb_row, al_row, d_row, nw_row):
    s = proj.shape[0]
    L = B_CHUNK
    cx = _FCOLS["b_xbc"][0] // B_CONV_DIM
    cdt = _FCOLS["b_dt"][0] // LANES
    cz = _FCOLS["b_z"][0] // B_INNER
    row = lambda w: pl.BlockSpec((1, w), lambda c: (0, 0))
    return pl.pallas_call(
        _ssd_kernel,
        grid=(s // L,),
        in_specs=[pl.BlockSpec((L, B_CONV_DIM), lambda c: (c, cx)),
                  pl.BlockSpec((L, B_CONV_DIM), lambda c: (jnp.maximum(c - 1, 0), cx)),
                  pl.BlockSpec((L, LANES), lambda c: (c, cdt)),
                  pl.BlockSpec((L, B_INNER), lambda c: (c, cz)),
                  pl.BlockSpec((B_CONV, B_CONV_DIM), lambda c: (0, 0)),
                  row(B_CONV_DIM), row(LANES), row(LANES), row(B_INNER), row(B_INNER)],
        out_specs=pl.BlockSpec((L, B_INNER), lambda c: (c, 0)),
        out_shape=jax.ShapeDtypeStruct((s, B_INNER), F32),
        scratch_shapes=[pltpu.VMEM((L + 8, B_CONV_DIM), F32), pltpu.VMEM((B_STATE, B_INNER), F32)],
        compiler_params=_cparams(("arbitrary",)),
        name="ssd",
    )(proj, proj, proj, proj, conv_w, conv_b, dtb_row, al_row, d_row, nw_row)


def _rope_kernel(q_ref, k_ref, pos_ref, inv_ref, qo_ref, ko_ref):
    ang = pos_ref[...].astype(F32) * inv_ref[...]
    cos, sin = jnp.cos(ang), jnp.sin(ang)
    lane = lax.broadcasted_iota(jnp.int32, ang.shape, 1)
    low = (lane & (HD - 1)) < HD // 2

    def rot(t):
        partner = jnp.where(low, -pltpu.roll(t, LANES - HD // 2, 1), pltpu.roll(t, HD // 2, 1))
        return t * cos + partner * sin

    for j in range(q_ref.shape[1] // LANES):
        qo_ref[:, j * LANES:(j + 1) * LANES] = _mx(rot(q_ref[:, j * LANES:(j + 1) * LANES]))
    ko_ref[...] = _mx(rot(k_ref[...]))


def _rope(proj, pos_col, inv_row):
    s = proj.shape[0]
    tm = min(512, s)
    wq = C_HEADS * HD
    return pl.pallas_call(
        _rope_kernel,
        grid=(s // tm,),
        in_specs=[pl.BlockSpec((tm, wq), lambda i: (i, _FCOLS["c_q"][0] // wq)),
                  pl.BlockSpec((tm, LANES), lambda i: (i, _FCOLS["c_k"][0] // LANES)),
                  pl.BlockSpec((tm, 1), lambda i: (i, 0)),
                  pl.BlockSpec((1, LANES), lambda i: (0, 0))],
        out_specs=[pl.BlockSpec((tm, wq), lambda i: (i, 0)), pl.BlockSpec((tm, LANES), lambda i: (i, 0))],
        out_shape=[jax.ShapeDtypeStruct((s, wq), MXU_DTYPE), jax.ShapeDtypeStruct((s, LANES), MXU_DTYPE)],
        compiler_params=_cparams(("arbitrary",)),
        name="rope",
    )(proj, proj, pos_col, inv_row)


def _pair_rows(q2, low):
    zero = jnp.zeros_like(q2)
    return jnp.concatenate([jnp.where(low, q2, zero), jnp.where(low, zero, q2)], axis=0)


def _band_kernel(*refs, dil, rows, max_dist, has_bias, has_sink, want_lse):
    refs = list(refs)
    q_ref, k_ref, kh_ref, v_ref, vh_ref = refs[:5]
    pos = 5
    tb_ref = sk_ref = None
    if has_bias:
        tb_ref = refs[pos]
        pos += 1
    if has_sink:
        sk_ref = refs[pos]
        pos += 1
    o_ref = refs[pos]
    lse_ref = refs[pos + 1] if want_lse else None
    j = pl.program_id(0)
    i = pl.program_id(1)
    span = BLK * dil
    qi = lax.broadcasted_iota(jnp.int32, (BLK, 2 * BLK), 0)
    kj = lax.broadcasted_iota(jnp.int32, (BLK, 2 * BLK), 1)
    dist = BLK + qi - kj
    band = jnp.where((dist >= 0) & (dist <= max_dist), 0.0, NEG_INF)
    first = jnp.where(i > 0, band, jnp.where(kj >= BLK, band, NEG_INF))
    low = lax.broadcasted_iota(jnp.int32, (BLK, LANES), 1) < HD
    bias = None
    if has_bias:
        bias = jnp.concatenate([jnp.concatenate([tb_ref[a, 0], tb_ref[a, 1]], axis=1) for a in range(2)], axis=0)
    sink = None
    if has_sink:
        sink = jnp.concatenate([jnp.full((BLK, 1), sk_ref[2 * j + a], F32) for a in range(2)], axis=0)

    def take(ref, start):
        return ref[pl.ds(start, BLK, stride=dil), :] if dil > 1 else ref[start:start + BLK, :]

    def put(ref, start, val):
        if dil > 1:
            ref[pl.ds(start, BLK, stride=dil), :] = val
        else:
            ref[start:start + BLK, :] = val

    for sb in range(rows // span):
        for r in range(dil):
            base = sb * span + r
            qs = _mx(_pair_rows(take(q_ref, base).astype(F32), low))
            if sb == 0:
                k_prev, v_prev, mask = take(kh_ref, r), take(vh_ref, r), first
            else:
                k_prev, v_prev, mask = take(k_ref, base - span), take(v_ref, base - span), band
            k_win = _mx(jnp.concatenate([k_prev, take(k_ref, base)], axis=0))
            v_win = _mx(jnp.concatenate([v_prev, take(v_ref, base)], axis=0))
            s = _dot_nt(qs, k_win) + jnp.concatenate([mask, mask], axis=0)
            if has_bias:
                s = s + bias
            m = jnp.max(s, axis=-1, keepdims=True)
            if has_sink:
                m = jnp.maximum(m, sink)
            e = jnp.exp(s - m)
            den = jnp.sum(e, axis=-1, keepdims=True)
            if has_sink:
                den = den + jnp.exp(sink - m)
            o = _dot(_mx(e), v_win) / den
            put(o_ref, base, jnp.where(low, o[0:BLK], o[BLK:]))
            if want_lse:
                lse = jnp.broadcast_to(m + jnp.log(den), (2 * BLK, LANES))
                put(lse_ref, base, jnp.where(low, lse[0:BLK], lse[BLK:]))


def _band(q_arr, k_arr, v_arr, *, q_col, k_col, v_col, shared_kv, dil, max_dist, tb=None, sinks=None,
          want_lse=False):
    s = q_arr.shape[0]
    span = BLK * dil
    rows = max(min(2048, s), span)
    per = rows // span
    npair = A_HEADS * HD // LANES
    kv = (lambda j: 0) if shared_kv else (lambda j: j)
    in_specs = [pl.BlockSpec((rows, LANES), lambda j, i: (i, q_col + j)),
                pl.BlockSpec((rows, LANES), lambda j, i: (i, k_col + kv(j))),
                pl.BlockSpec((span, LANES), lambda j, i: (jnp.maximum(i * per - 1, 0), k_col + kv(j))),
                pl.BlockSpec((rows, LANES), lambda j, i: (i, v_col + kv(j))),
                pl.BlockSpec((span, LANES), lambda j, i: (jnp.maximum(i * per - 1, 0), v_col + kv(j)))]
    args = [q_arr, k_arr, k_arr, v_arr, v_arr]
    if tb is not None:
        in_specs.append(pl.BlockSpec((2, 2, BLK, LANES), lambda j, i: (j, 0, 0, 0)))
        args.append(tb)
    if sinks is not None:
        in_specs.append(pl.BlockSpec(memory_space=pltpu.SMEM))
        args.append(sinks)
    o_spec = pl.BlockSpec((rows, LANES), lambda j, i: (i, j))
    o_shape = jax.ShapeDtypeStruct((s, npair * LANES), F32)
    return pl.pallas_call(
        functools.partial(_band_kernel, dil=dil, rows=rows, max_dist=max_dist, has_bias=tb is not None,
                          has_sink=sinks is not None, want_lse=want_lse),
        grid=(npair, s // rows),
        in_specs=in_specs,
        out_specs=[o_spec, o_spec] if want_lse else o_spec,
        out_shape=[o_shape, o_shape] if want_lse else o_shape,
        compiler_params=_cparams(("arbitrary", "arbitrary")),
        name="band_attention",
    )(*args)


def _memkv_kernel(mem_ref, nw_ref, w_ref, o_ref):
    x = mem_ref[...]
    y = x * lax.rsqrt(jnp.mean(x * x, axis=-1, keepdims=True) + EPS) * nw_ref[...]
    o_ref[...] = _dot(_mx(y), w_ref[...])


def _memkv(mem2, norm_w, w):
    return pl.pallas_call(
        _memkv_kernel,
        out_shape=jax.ShapeDtypeStruct((mem2.shape[0], w.shape[1]), F32),
        compiler_params=pltpu.CompilerParams(vmem_limit_bytes=VMEM_LIMIT),
        name="mem_kv",
    )(mem2, norm_w, w)


def _memattn_kernel(q_ref, k_ref, v_ref, o_ref):
    tq = q_ref.shape[0]
    low = lax.broadcasted_iota(jnp.int32, (tq, LANES), 1) < HD
    for pr in range(M_HEADS // 2):
        cols = slice(pr * LANES, (pr + 1) * LANES)
        s = _dot_nt(_pair_rows(q_ref[:, cols], low), _mx(k_ref[:, cols]))
        e = jnp.exp(s - jnp.max(s, axis=-1, keepdims=True))
        o = _dot(_mx(e), _mx(v_ref[:, cols])) / jnp.sum(e, axis=-1, keepdims=True)
        o_ref[:, cols] = jnp.where(low, o[0:tq], o[tq:])


def _memattn(proj_b, kv):
    s = proj_b.shape[0]
    tq = min(256, s)
    ml = kv.shape[0]
    wm = M_HEADS * HD
    return pl.pallas_call(
        _memattn_kernel,
        grid=(s // tq,),
        in_specs=[pl.BlockSpec((tq, wm), lambda i: (i, _BCOLS["m_q"][0] // wm)),
                  pl.BlockSpec((ml, wm), lambda i: (0, 0)),
                  pl.BlockSpec((ml, wm), lambda i: (0, 1))],
        out_specs=pl.BlockSpec((tq, wm), lambda i: (i, 0)),
        out_shape=jax.ShapeDtypeStruct((s, wm), F32),
        compiler_params=_cparams(("arbitrary",)),
        name="mem_attention",
    )(proj_b, kv, kv)


def _outproj_kernel(x_ref, a_ref, az_ref, b_ref, c_ref, cz_ref, d0_ref, d1_ref, d2_ref, l0_ref, l1_ref, l2_ref,
                    dz_ref, m_ref, mz_ref, w_ref, nw_ref, o_ref):
    l0, l1, l2 = l0_ref[...], l1_ref[...], l2_ref[...]
    mx = jnp.maximum(jnp.maximum(l0, l1), l2)
    e0, e1, e2 = jnp.exp(l0 - mx), jnp.exp(l1 - mx), jnp.exp(l2 - mx)
    inv = 1.0 / (e0 + e1 + e2)
    d = (e0 * inv) * d0_ref[...] + (e1 * inv) * d1_ref[...] + (e2 * inv) * d2_ref[...]
    pieces = (a_ref[...] * _silu(az_ref[...]), b_ref[...], c_ref[...] * _silu(cz_ref[...]),
              d * _silu(dz_ref[...]), m_ref[...] * _silu(mz_ref[...]))
    y = None
    row = 0
    for piece in pieces:
        w = piece.shape[1]
        part = _dot(_mx(piece), w_ref[row:row + w, :])
        y = part if y is None else y + part
        row += w
    y = y * lax.rsqrt(jnp.mean(y * y, axis=-1, keepdims=True) + EPS) * nw_ref[...]
    o_ref[...] = x_ref[...] + y


def _outproj(x2, proj, a_out, b_out, c_out, d_outs, d_lses, m_out, w, norm_w):
    s = x2.shape[0]
    tm = min(512, s)
    wide = lambda width, col: pl.BlockSpec((tm, width), lambda i: (i, col))
    w512 = A_HEADS * HD
    wm = M_HEADS * HD
    return pl.pallas_call(
        _outproj_kernel,
        grid=(s // tm,),
        in_specs=[wide(D_MODEL, 0), wide(w512, 0), wide(w512, _FCOLS["a_z"][0] // w512), wide(w512, 0),
                  wide(w512, 0), wide(w512, _FCOLS["c_z"][0] // w512),
                  wide(w512, 0), wide(w512, 0), wide(w512, 0), wide(w512, 0), wide(w512, 0), wide(w512, 0),
                  wide(w512, _FCOLS["d_z"][0] // w512), wide(wm, 0), wide(wm, _FCOLS["m_z"][0] // wm),
                  pl.BlockSpec((MIX_WIDTH, D_MODEL), lambda i: (0, 0)),
                  pl.BlockSpec((1, D_MODEL), lambda i: (0, 0))],
        out_specs=wide(D_MODEL, 0),
        out_shape=jax.ShapeDtypeStruct((s, D_MODEL), F32),
        compiler_params=_cparams(("arbitrary",)),
        name="outproj",
    )(x2, a_out, proj, b_out, c_out, proj, *d_outs, *d_lses, proj, m_out, proj, w, norm_w)


def _c_heads(t, axis):
    shape = t.shape
    t = t.reshape(shape[:axis] + (C_HEADS, HD) + shape[axis + 1:])
    return jnp.take(t, jnp.array(_C_ORDER), axis=axis).reshape(shape)


def _repack_w_in(w, cols):
    out = []
    for name, (_, width) in cols.items():
        src, true_w, is_query = _SRC[name]
        piece = w[:, src:src + true_w]
        if is_query:
            piece = piece * HD ** -0.5
        if name in ("c_q", "c_z"):
            piece = _c_heads(piece, 1)
        if true_w < width:
            piece = jnp.pad(piece, ((0, 0), (0, width - true_w)))
        out.append(piece)
    return _mx(jnp.concatenate(out, axis=1))


def _heads_first_t(t, nh):
    s = t.shape[0]
    return jnp.transpose(t.reshape(s, nh, HD), (1, 2, 0))


def kernel(x, mem, positions, pre_norm, post_norm, w_in, w_out, rel_bias, a_cmp_pos, a_cmp_w1, a_cmp_w2,
           b_conv_w, b_conv_b, b_dt_bias, b_a_log, b_d, b_norm, c_sinks, m_norm, m_w_kv):
    b, s, _ = x.shape
    assert b == 1 and s % 2048 == 0, "sequence length must be a multiple of 2048 and batch 1"
    depth = w_in.shape[0]
    nb = s // BLK
    ncmp = s // CMP_STRIDE

    rel_a = rel_bias[:, :A_HEADS]
    rel_d = rel_bias[:, A_HEADS:]
    no_limit = 1 << 30
    c0_a = np.array([-2 * BLK] + [o * BLK for o in range(N_OFF)] + [A_WINDOW], np.int32)
    hi_a = np.array([no_limit] * (N_TAB - 1) + [A_WINDOW], np.int32)
    tb_a = _bias_table(rel_a, jnp.asarray(c0_a), a_row=-1, b_lane=1, scale=1, rows=BLK,
                       hi=jnp.asarray(hi_a))
    bc_a = _bias_table(rel_a, jnp.array([CMP_STRIDE * CMP_NEAR - (CMP_LEN - 1)], jnp.int32),
                       a_row=-CMP_STRIDE, b_lane=1, scale=1, rows=CMP_TAB)
    tb_d = [_bias_table(rel_d[:, p * D_SLOTS:(p + 1) * D_SLOTS], jnp.array([BLK, 0], jnp.int32), a_row=1,
                        b_lane=-1, scale=dil, rows=BLK) for p, (_, dil) in enumerate(D_PATTERNS)]
    key_pos = np.arange(s)
    key_aug_np = np.zeros((s, AUG_K - AUG_BASE), np.float32)
    key_aug_np[:, 0:2] = 1.0
    key_aug_np[key_pos, AUG_SEL - AUG_BASE + (key_pos // SEL_BLK) % (SEL_TILE // SEL_BLK)] = 1.0
    key_aug = jnp.asarray(key_aug_np, MXU_DTYPE)
    val_aug_np = np.zeros((V_ROWS - HD, s), np.float32)
    val_aug_np[0] = 1.0
    val_aug = jnp.asarray(val_aug_np, MXU_DTYPE)[None]
    head_slot = jnp.eye(A_KV, dtype=F32)
    half = ROPE_THETA ** (-jnp.arange(HD // 2, dtype=F32) / (HD // 2))
    inv_row = jnp.tile(half, LANES // (HD // 2)).reshape(1, LANES)
    pos_col = positions.reshape(s, 1)
    mem2 = mem.reshape(mem.shape[1], D_MODEL)

    x2 = x.reshape(s, D_MODEL)
    for layer in range(depth):
        norm_w = pre_norm[layer].reshape(1, D_MODEL)
        proj = _inproj(x2, norm_w, _repack_w_in(w_in[layer], _FCOLS), F32)
        proj_b = _inproj(x2, norm_w, _repack_w_in(w_in[layer], _BCOLS), MXU_DTYPE)
        fcol = lambda name: proj[:, _FCOLS[name][0]:_FCOLS[name][0] + _FCOLS[name][1]]
        bcol = lambda name: proj_b[:, _BCOLS[name][0]:_BCOLS[name][0] + _BCOLS[name][1]]

        wide = CMP_STRIDE * A_KV * HD
        u = jnp.stack([fcol("a_kc").reshape(ncmp, wide), fcol("a_vc").reshape(ncmp, wide)])
        pe = (a_cmp_pos[layer].reshape(2, 1, 2, CMP_STRIDE, 1, HD)
              * head_slot.reshape(1, A_KV, 1, 1, A_KV, 1)).reshape(2 * A_KV, 2, wide)
        w1 = _mx(a_cmp_w1[layer].reshape(2, 1, 2, CMP_STRIDE, 1, HD, CMP_HIDDEN)
                 * head_slot.reshape(1, A_KV, 1, 1, A_KV, 1, 1)).reshape(2 * A_KV, 2, wide, CMP_HIDDEN)
        cmp_out = _compress(u, pe, w1, _mx(a_cmp_w2[layer]), ncmp)
        kc_own = (cmp_out[:A_KV, :, None, :] * head_slot[:, None, :, None]).reshape(A_KV, ncmp, AUG_BASE)
        kc = _mx(jnp.concatenate([kc_own, jnp.ones((A_KV, ncmp, 2), F32),
                                  jnp.zeros((A_KV, ncmp, AUG_K - AUG_BASE - 2), F32)], axis=2))
        vct = _mx(jnp.transpose(cmp_out[A_KV:], (0, 2, 1)))
        gates = jnp.transpose(fcol("a_gate")[:, :3 * A_HEADS].reshape(nb, BLK, A_KV, A_G, 3),
                              (2, 0, 4, 3, 1)).reshape(A_KV, nb, 3, A_G * BLK)
        vst_aug = jnp.concatenate([_heads_first_t(bcol("a_vs"), A_KV),
                                   jnp.broadcast_to(val_aug, (A_KV, V_ROWS - HD, s))], axis=1)
        a_out = _nsa(proj_b, kc, vct, key_aug, vst_aug, _heads_first_t(bcol("a_vw"), A_KV), gates, tb_a, bc_a, s)

        pad_row = lambda v: jnp.pad(v, (0, LANES - v.shape[0])).reshape(1, LANES)
        b_out = _ssd(proj, b_conv_w[layer], b_conv_b[layer].reshape(1, B_CONV_DIM), pad_row(b_dt_bias[layer]),
                     pad_row(b_a_log[layer]), jnp.repeat(b_d[layer], B_HEADDIM).reshape(1, B_INNER),
                     b_norm[layer].reshape(1, B_INNER))

        cq_rot, ck_rot = _rope(proj, pos_col, inv_row)
        c_out = _band(cq_rot, ck_rot, proj_b, q_col=0, k_col=0, v_col=_BCOLS["c_v"][0] // LANES, shared_kv=True,
                      dil=1, max_dist=C_WINDOW - 1, sinks=c_sinks[layer][jnp.array(_C_ORDER)])

        d_outs, d_lses = [], []
        for p, (window, dil) in enumerate(D_PATTERNS):
            o_p, lse_p = _band(proj, proj, proj, q_col=_FCOLS["d_q"][0] // LANES + p * D_SLOTS * HD // LANES,
                               k_col=_FCOLS["d_k"][0] // LANES, v_col=_FCOLS["d_v"][0] // LANES, shared_kv=False,
                               dil=dil, max_dist=window // dil, tb=tb_d[p], want_lse=True)
            d_outs.append(o_p)
            d_lses.append(lse_p)

        kv = _memkv(mem2, m_norm[layer].reshape(1, D_MODEL), _mx(m_w_kv[layer]))
        m_out = _memattn(proj_b, kv)

        c_rows = slice(A_HEADS * HD + B_INNER, A_HEADS * HD + B_INNER + C_HEADS * HD)
        w_o = w_out[layer].at[c_rows].set(_c_heads(w_out[layer][c_rows], 0))
        x2 = _outproj(x2, proj, a_out, b_out, c_out, d_outs, d_lses, m_out, _mx(w_o),
                      post_norm[layer].reshape(1, D_MODEL))
    return x2.reshape(b, s, D_MODEL)
```

```python
import functools
import math

import numpy as np
import jax
import jax.numpy as jnp
from jax import lax
from jax.experimental import pallas as pl
from jax.experimental.pallas import tpu as pltpu

F32 = jnp.float32
MXU_DTYPE = jnp.bfloat16

D_MODEL = 1024
HD = 64
BLK = 128
NEG_INF = -1e30
TINY = 1e-30
EPS = 1e-6

A_HEADS, A_KV = 8, 2
A_G = A_HEADS // A_KV
CMP_LEN, CMP_STRIDE, CMP_HIDDEN = 32, 16, 128
SEL_BLK, SEL_TOPK, A_WINDOW = 64, 16, 512
B_HEADS, B_HEADDIM, B_GROUPS, B_STATE, B_CONV, B_CHUNK = 8, 64, 2, 128, 4, 128
B_INNER = B_HEADS * B_HEADDIM
B_CONV_DIM = B_INNER + 2 * B_GROUPS * B_STATE
C_HEADS, C_KV, C_WINDOW = 8, 2, 128
ROPE_THETA = 150000.0
D_SLOTS = 8
D_PATTERNS = ((128, 1), (512, 4), (2048, 16))
D_NPAT = 3
M_HEADS = 4
N_BUCKETS, MAX_DIST = 32, 2048
MIX_WIDTH = A_HEADS * HD + B_INNER + C_HEADS * HD + D_SLOTS * HD + M_HEADS * HD

LANES = 128
VMEM_LIMIT = 56 * 1024 * 1024

_SRC = {"a_q": (0, 512, True), "a_kc": (512, 128, False), "a_vc": (640, 128, False), "a_ks": (768, 128, False),
        "a_vs": (896, 128, False), "a_kw": (1024, 128, False), "a_vw": (1152, 128, False),
        "a_gate": (1280, 24, False), "a_z": (1304, 512, False), "b_xbc": (1816, 1024, False),
        "b_dt": (2840, 8, False), "b_z": (2848, 512, False), "c_q": (3360, 512, True), "c_k": (3872, 128, False),
        "c_v": (4000, 128, False), "c_z": (4128, 512, False), "d_q": (4640, 1536, True), "d_k": (6176, 512, False),
        "d_v": (6688, 512, False), "d_z": (7200, 512, False), "m_q": (7712, 256, True), "m_z": (7968, 256, False)}


def _layout(pieces):
    cols, off = {}, 0
    for name, width in pieces:
        cols[name] = (off, width)
        off += width
    return cols, off


_FCOLS, N_F32 = _layout((("b_xbc", 1024), ("a_z", 512), ("b_z", 512), ("c_z", 512), ("d_z", 512), ("c_q", 512),
                         ("d_q", 1536), ("d_k", 512), ("d_v", 512), ("m_z", 256), ("a_kc", 128), ("a_vc", 128),
                         ("c_k", 128), ("a_gate", 128), ("b_dt", 128)))
_BCOLS, N_BF16 = _layout((("a_q", 512), ("m_q", 256), ("a_ks", 128), ("a_vs", 128), ("a_kw", 128),
                          ("a_vw", 128), ("c_v", 128)))
PROJ_TN = 1408
_C_ORDER = tuple(h for j in range(C_HEADS // C_KV) for h in (j, j + C_HEADS // C_KV))


def _t5_thresholds():
    d = np.arange(0, 4 * MAX_DIST)
    exact = N_BUCKETS // 2
    rel = np.maximum(d, exact).astype(np.float64)
    large = exact + (np.log(rel / exact) / math.log(MAX_DIST / exact) * (N_BUCKETS - exact)).astype(np.int64)
    bucket = np.where(d < exact, d, np.minimum(large, N_BUCKETS - 1))
    return tuple(int(np.argmax(bucket >= b)) for b in range(1, N_BUCKETS))


_THR = _t5_thresholds()
FAR_DIST = _THR[-1]
N_OFF = -(-(FAR_DIST + BLK) // BLK) + 1
TAB_OFF0 = 1
TAB_EDGE = TAB_OFF0 + N_OFF
N_TAB = TAB_EDGE + 1
FAR_WIDTH = 2
CMP_REACH = -(-FAR_DIST // CMP_STRIDE) // 8 * 8 + 8
CMP_WIN = 128
CMP_NEAR = 128
CMP_TAB = CMP_NEAR + CMP_WIN
SEL_TILE = 512
AUG_BASE = LANES
AUG_SEL = AUG_BASE + 8
AUG_END = AUG_SEL + SEL_TILE // SEL_BLK
AUG_K = 2 * LANES
V_ROWS = HD + 16


def _cparams(sem):
    return pltpu.CompilerParams(dimension_semantics=sem, vmem_limit_bytes=VMEM_LIMIT)


def _mx(x):
    return x.astype(MXU_DTYPE)


def _dot(a, b):
    return jnp.dot(a, b, preferred_element_type=F32)


def _dot_nt(a, b):
    return lax.dot_general(a, b, (((1,), (1,)), ((), ())), preferred_element_type=F32)


def _split3(a):
    hi = _mx(a)
    r1 = a - hi.astype(F32)
    mid = _mx(r1)
    lo = _mx(r1 - mid.astype(F32))
    return hi, mid, lo


def _dot_f32_rhs_exact(a, b):
    hi, mid, lo = _split3(a)
    return _dot(hi, b) + _dot(mid, b) + _dot(lo, b)


def _dot_f32_lhs_exact(a, b):
    hi, mid, lo = _split3(b)
    return _dot(a, hi) + _dot(a, mid) + _dot(a, lo)


def _silu(x):
    return x * jax.nn.sigmoid(x)


def _bias_table_kernel(rel_ref, c0_ref, hi_ref, o_ref, *, a_row, b_lane, scale, rows, masked):
    h = pl.program_id(0)
    t = pl.program_id(1)
    r = lax.broadcasted_iota(jnp.int32, (rows, LANES), 0)
    l = lax.broadcasted_iota(jnp.int32, (rows, LANES), 1)
    dist = (c0_ref[t] + a_row * r + b_lane * l) * scale
    out = jnp.full((rows, LANES), rel_ref[0, h], F32)
    for b in range(1, N_BUCKETS):
        out = jnp.where(dist >= _THR[b - 1], rel_ref[b, h], out)
    if masked:
        out = jnp.where(dist < 0, NEG_INF, jnp.where(dist >= hi_ref[t], NEG_INF, out))
    o_ref[0, 0] = out


def _bias_table(rel_cols, c0, *, a_row, b_lane, scale, rows, hi=None):
    nh = rel_cols.shape[1]
    nt = c0.shape[0]
    return pl.pallas_call(
        functools.partial(_bias_table_kernel, a_row=a_row, b_lane=b_lane, scale=scale, rows=rows,
                          masked=hi is not None),
        grid_spec=pltpu.PrefetchScalarGridSpec(
            num_scalar_prefetch=3, grid=(nh, nt), in_specs=[],
            out_specs=pl.BlockSpec((1, 1, rows, LANES), lambda h, t, *_: (h, t, 0, 0))),
        out_shape=jax.ShapeDtypeStruct((nh, nt, rows, LANES), F32),
        compiler_params=_cparams(("arbitrary", "arbitrary")),
        name="bias_table",
    )(rel_cols, c0, c0 if hi is None else hi)


def _inproj_kernel(x_ref, nw_ref, w_ref, o_ref, h_scr):
    @pl.when(pl.program_id(1) == 0)
    def _():
        x = x_ref[...]
        y = x * lax.rsqrt(jnp.mean(x * x, axis=-1, keepdims=True) + EPS)
        h_scr[...] = _mx(y * nw_ref[...])

    o_ref[...] = _dot(h_scr[...], w_ref[...]).astype(o_ref.dtype)


def _inproj(x2, norm_w, w, out_dtype):
    s = x2.shape[0]
    tm, tn = min(1024, s), PROJ_TN
    return pl.pallas_call(
        _inproj_kernel,
        grid=(s // tm, w.shape[1] // tn),
        in_specs=[pl.BlockSpec((tm, D_MODEL), lambda i, j: (i, 0)),
                  pl.BlockSpec((1, D_MODEL), lambda i, j: (0, 0)),
                  pl.BlockSpec((D_MODEL, tn), lambda i, j: (0, j))],
        out_specs=pl.BlockSpec((tm, tn), lambda i, j: (i, j)),
        out_shape=jax.ShapeDtypeStruct((s, w.shape[1]), out_dtype),
        scratch_shapes=[pltpu.VMEM((tm, D_MODEL), MXU_DTYPE)],
        compiler_params=_cparams(("arbitrary", "arbitrary")),
        name="inproj",
    )(x2, norm_w, w)


def _compress_kernel(u_ref, pe_ref, w1_ref, w2_ref, o_ref, h2_scr, *, ncmp):
    u = u_ref[0]
    pe_first, pe_second = pe_ref[0, 0:1, :], pe_ref[0, 1:2, :]
    h1 = _dot(_mx(u + pe_first), w1_ref[0, 0])
    h2_scr[0:ncmp, :] = _dot(_mx(u + pe_second), w1_ref[0, 1])
    h2_scr[ncmp:ncmp + 8, :] = _dot(_mx(jnp.broadcast_to(pe_second, (8, pe_second.shape[1]))), w1_ref[0, 1])
    pre = h1 + h2_scr[pl.ds(1, ncmp), :]
    o_ref[0] = _dot(_mx(jax.nn.gelu(pre)), w2_ref[0])


def _compress(u, pe, w1, w2, ncmp):
    wide = CMP_STRIDE * A_KV * HD
    return pl.pallas_call(
        functools.partial(_compress_kernel, ncmp=ncmp),
        grid=(2 * A_KV,),
        in_specs=[pl.BlockSpec((1, ncmp, wide), lambda i: (i // A_KV, 0, 0)),
                  pl.BlockSpec((1, 2, wide), lambda i: (i, 0, 0)),
                  pl.BlockSpec((1, 2, wide, CMP_HIDDEN), lambda i: (i, 0, 0, 0)),
                  pl.BlockSpec((1, CMP_HIDDEN, HD), lambda i: (i // A_KV, 0, 0))],
        out_specs=pl.BlockSpec((1, ncmp, HD), lambda i: (i, 0, 0)),
        out_shape=jax.ShapeDtypeStruct((2 * A_KV, ncmp, HD), F32),
        scratch_shapes=[pltpu.VMEM((ncmp + 8, CMP_HIDDEN), F32)],
        compiler_params=_cparams(("arbitrary",)),
        name="nsa_compress",
    )(u, pe, w1, w2)


def _nsa_kernel(q_ref, kc_ref, vct_ref, ks_ref, ka_ref, vst_ref, kw_ref, vwt_ref, g_ref, tb_ref, bc_ref,
                o_ref, s_scr, sel_scr, qa_scr, pcs_scr, *score_bufs, seq, tk):
    sa_scr = score_bufs[0]
    hk = pl.program_id(0)
    n = pl.program_id(1)
    ncmp = seq // CMP_STRIDE
    nsel = seq // SEL_BLK
    n_top = min(SEL_TOPK, nsel)
    r_all = A_G * BLK
    q_rows = q_ref[...].astype(F32)
    halves = [q_rows[:, p * LANES:(p + 1) * LANES].T for p in range(A_G // 2)]
    q_t = _mx(jnp.concatenate([h[r * HD:(r + 1) * HD] for h in halves for r in range(2)], axis=1))
    far_row = jnp.concatenate([tb_ref[g, TAB_OFF0 + N_OFF - 1, 0:1, :] for g in range(A_G)], axis=1)

    def tile_bias(o, edge=None):
        idx = jnp.clip(o, -1, N_OFF - 1) + TAB_OFF0
        if edge is not None:
            idx = jnp.where(o == edge, TAB_EDGE, idx)
        return jnp.concatenate([tb_ref[g, idx] for g in range(A_G)], axis=1)

    far_hi = _mx(far_row).astype(F32)
    bias_rows = jnp.concatenate([far_hi, far_row - far_hi, jnp.zeros((AUG_SEL - AUG_BASE - 2, r_all), F32)], axis=0)
    qa_scr[0:AUG_BASE, :] = jnp.zeros((AUG_BASE, r_all), MXU_DTYPE)
    qa_scr[pl.ds(pl.multiple_of(hk * HD, HD), HD), :] = q_t
    qa_scr[AUG_BASE:AUG_END, :] = _mx(
        jnp.concatenate([bias_rows, jnp.zeros((AUG_END - AUG_SEL, r_all), F32)], axis=0))
    qa_scr[AUG_END:, :] = jnp.zeros((AUG_K - AUG_END, r_all), MXU_DTYPE)
    q_own = qa_scr[0:AUG_BASE, :]

    nwb = A_WINDOW // BLK + 1
    kb0 = jnp.maximum(n - (nwb - 1), 0)
    k0 = pl.multiple_of(kb0 * BLK, BLK)
    s = _dot(kw_ref[pl.ds(k0, nwb * BLK), :], q_own)
    s = s + jnp.concatenate([tile_bias(n - kb0 - i, edge=nwb - 1) for i in range(nwb)], axis=0)
    e = jnp.exp(s - jnp.max(s, axis=0, keepdims=True))
    o_w = _dot(vwt_ref[0, :, pl.ds(k0, nwb * BLK)], _mx(e)) / jnp.sum(e, axis=0, keepdims=True)

    per = SEL_BLK // CMP_STRIDE
    tok = BLK // CMP_STRIDE

    def compressed(rows):
        def run():
            s_scr[0:rows, :] = _dot(kc_ref[0, 0:rows, :], qa_scr[...])
            st = jnp.clip((tok * n - CMP_REACH) // 16 * 16, 0, rows - CMP_WIN)
            off = pl.multiple_of(st - tok * n + CMP_NEAR, 8)
            st = pl.multiple_of(st, 16)
            near = jnp.concatenate([bc_ref[g, 0, pl.ds(off, CMP_WIN), :] for g in range(A_G)], axis=1)
            s_scr[pl.ds(st, CMP_WIN), :] = _dot(kc_ref[0, pl.ds(st, CMP_WIN), 0:AUG_BASE], q_own) + near
            jrow = lax.broadcasted_iota(jnp.int32, (rows, BLK), 0)
            qpos = n * BLK + lax.broadcasted_iota(jnp.int32, (rows, BLK), 1)
            visible = jnp.where(jrow * CMP_STRIDE + (CMP_LEN - 1) <= qpos, 0.0, NEG_INF)
            s_c = s_scr[0:rows, :] + jnp.concatenate([visible] * A_G, axis=1)
            m = jnp.max(s_c, axis=0, keepdims=True)
            e = jnp.exp(s_c - m)
            den = jnp.sum(e, axis=0, keepdims=True)
            inv = jnp.where(m > 0.5 * NEG_INF, 1.0 / jnp.maximum(den, TINY), 0.0)
            p_c = e * inv
            o_c = _dot(vct_ref[0, :, 0:rows], _mx(p_c))
            pcs = p_c[:, 0:BLK]
            for g in range(1, A_G):
                pcs = pcs + p_c[:, g * BLK:(g + 1) * BLK]
            pcs_scr[0:8, :] = jnp.zeros((8, BLK), F32)
            pcs_scr[8:8 + rows, :] = pcs
            part = [pcs_scr[pl.ds(8 + i, rows // per, stride=per), :] for i in range(-1, per)]
            imp = part[1] + part[0]
            for i in range(1, per):
                imp = imp + (part[i + 1] + part[i])
            if rows < ncmp:
                imp = jnp.concatenate([imp, jnp.zeros((nsel - rows // per, BLK), F32)], axis=0)
            return o_c, imp
        return run

    o_c, imp = compressed(ncmp)()

    b_io = lax.broadcasted_iota(jnp.int32, (nsel, BLK), 0)
    b_f = b_io.astype(F32)
    pos = n * BLK + lax.broadcasted_iota(jnp.int32, (nsel, BLK), 1)
    forced = (b_io == pos // SEL_BLK) | (b_io == 0)
    valid = b_io * SEL_BLK <= pos
    imp = jnp.where(forced, -2.0, jnp.where(valid, imp, -1.0))

    def pick_one(_, carry):
        imp, sel = carry
        vals = [imp[8 * g:8 * (g + 1)] for g in range(nsel // 8)]
        idxs = [b_f[8 * g:8 * (g + 1)] for g in range(nsel // 8)]
        while len(vals) > 1:
            later = [vals[a + 1] > vals[a] for a in range(0, len(vals), 2)]
            idxs = [jnp.where(c, idxs[2 * a + 1], idxs[2 * a]) for a, c in enumerate(later)]
            vals = [jnp.where(c, vals[2 * a + 1], vals[2 * a]) for a, c in enumerate(later)]
        v, ix = vals[0], idxs[0]
        top = jnp.max(v, axis=0, keepdims=True)
        first = jnp.min(jnp.where(v == top, ix, float(nsel)), axis=0, keepdims=True)
        hit = b_f == first
        sel = jnp.where(hit, jnp.where(top >= 0.0, 0.0, NEG_INF), sel)
        return jnp.where(hit, -2.0, imp), sel

    _, sel = lax.fori_loop(0, n_top - 2, pick_one, (imp, jnp.where(forced, 0.0, NEG_INF)))
    sel_scr[...] = sel

    nblk_t = tk // BLK
    no_bias_rows = jnp.zeros((AUG_SEL - AUG_BASE, r_all), F32)

    def scores(t, const_rows):
        k0 = pl.multiple_of(t * tk, tk)
        picked = sel_scr[pl.ds(pl.multiple_of((AUG_END - AUG_SEL) * t, 8), AUG_END - AUG_SEL), :]
        aug = _mx(jnp.concatenate([const_rows, jnp.concatenate([picked] * A_G, axis=1)], axis=0))
        q_aug = jnp.concatenate([q_own, aug, jnp.zeros((AUG_K - AUG_END, r_all), MXU_DTYPE)], axis=0)
        keys = jnp.concatenate([ks_ref[pl.ds(k0, tk), :], ka_ref[pl.ds(k0, tk), :]], axis=1)
        return _dot(keys, q_aug)

    last = (n * BLK) // tk

    def tile_scores(t, near):
        if not near:
            return scores(t, bias_rows)
        bias = jnp.concatenate([tile_bias(n - (nblk_t * t + i)) for i in range(nblk_t)], axis=0)
        return scores(t, no_bias_rows) + bias

    def accumulate(t, s, m_i, acc, m_tile):
        k0 = pl.multiple_of(t * tk, tk)
        m_new = jnp.maximum(m_i, m_tile)
        p = jnp.exp(s - m_new)
        acc = acc * jnp.exp(m_i - m_new) + _dot(vst_ref[0, :, pl.ds(k0, tk)], _mx(p))
        return m_new, acc

    def stage1(t, buf, near):
        s = tile_scores(t, near)
        buf[...] = s
        return jnp.max(s, axis=0, keepdims=True)

    def run_pipelined(first, trips, width, near, m_i, acc):
        bufs = score_bufs[:width]

        def trip(u, carry):
            m_i, acc, m_cur = carry
            t0 = first + width * u
            for k in range(width):
                m_next = stage1(t0 + k + 1, bufs[(k + 1) % width], near)
                m_i, acc = accumulate(t0 + k, bufs[k][...], m_i, acc, m_cur)
                m_cur = m_next
            return m_i, acc, m_cur

        return lax.fori_loop(0, trips, trip, (m_i, acc, stage1(first, sa_scr, near)))

    def single(t, carry):
        s = tile_scores(t, True)
        return accumulate(t, s, *carry, jnp.max(s, axis=0, keepdims=True))

    far_tiles = jnp.maximum(n - (N_OFF - 2), 0) // nblk_t
    far_trips = jnp.maximum(far_tiles - 1, 0) // FAR_WIDTH
    m_i, acc_s, m_t = run_pipelined(0, far_trips, FAR_WIDTH, False, jnp.full((1, r_all), NEG_INF, F32),
                                    jnp.zeros((V_ROWS, r_all), F32))
    far_done = FAR_WIDTH * far_trips
    has_far = jnp.minimum(far_tiles, 1)
    m_i, acc_s = lax.fori_loop(0, has_far, lambda _, c: accumulate(far_done, sa_scr[...], *c, m_t), (m_i, acc_s))
    start = far_done + has_far
    near_pairs = (last - start) // 2
    m_i, acc_s, m_t = run_pipelined(start, near_pairs, 2, True, m_i, acc_s)
    m_i, acc_s = accumulate(start + 2 * near_pairs, sa_scr[...], m_i, acc_s, m_t)
    _, acc_s = lax.fori_loop(start + 2 * near_pairs + 1, last + 1, single, (m_i, acc_s))
    o_s = acc_s[0:HD] / jnp.maximum(acc_s[HD:HD + 1], TINY)

    gates = jax.nn.sigmoid(g_ref[0, 0])
    o_t = gates[0:1] * o_c + gates[1:2] * o_s + gates[2:3] * o_w
    for pair in range(A_G // 2):
        both = jnp.concatenate([o_t[:, (2 * pair) * BLK:(2 * pair + 1) * BLK],
                                o_t[:, (2 * pair + 1) * BLK:(2 * pair + 2) * BLK]], axis=0)
        o_ref[:, pair * LANES:(pair + 1) * LANES] = both.T


def _nsa(proj_b, kc, vct, key_aug, vst, vwt, gates, tb_a, bc, seq):
    nb = seq // BLK
    ncmp = seq // CMP_STRIDE
    nsel = seq // SEL_BLK
    tk = SEL_TILE
    full = lambda h, n: (h, 0, 0)
    wq = A_G * HD
    resident = lambda name: pl.BlockSpec((seq, LANES), lambda h, n: (0, _BCOLS[name][0] // LANES))
    return pl.pallas_call(
        functools.partial(_nsa_kernel, seq=seq, tk=tk),
        grid=(A_KV, nb),
        in_specs=[pl.BlockSpec((BLK, wq), lambda h, n: (n, _BCOLS["a_q"][0] // wq + h)),
                  pl.BlockSpec((1, ncmp, AUG_K), full),
                  pl.BlockSpec((1, HD, ncmp), full),
                  resident("a_ks"),
                  pl.BlockSpec((seq, LANES), lambda h, n: (0, 0)),
                  pl.BlockSpec((1, V_ROWS, seq), full),
                  resident("a_kw"),
                  pl.BlockSpec((1, HD, seq), full),
                  pl.BlockSpec((1, 1, 3, A_G * BLK), lambda h, n: (h, n, 0, 0)),
                  pl.BlockSpec((A_G, N_TAB, BLK, LANES), lambda h, n: (h, 0, 0, 0)),
                  pl.BlockSpec((A_G, 1, CMP_TAB, LANES), lambda h, n: (h, 0, 0, 0))],
        out_specs=pl.BlockSpec((BLK, A_G * HD), lambda h, n: (n, h)),
        out_shape=jax.ShapeDtypeStruct((seq, A_HEADS * HD), F32),
        scratch_shapes=[pltpu.VMEM((ncmp, A_G * BLK), F32), pltpu.VMEM((nsel, BLK), F32),
                        pltpu.VMEM((AUG_K, A_G * BLK), MXU_DTYPE), pltpu.VMEM((ncmp + 8, BLK), F32),
                        *[pltpu.VMEM((tk, A_G * BLK), F32) for _ in range(FAR_WIDTH)]],
        compiler_params=_cparams(("arbitrary", "arbitrary")),
        name="nsa",
    )(proj_b, kc, vct, proj_b, key_aug, vst, proj_b, vwt, gates, tb_a, bc)


def _ssd_kernel(xc_ref, xp_ref, dt_ref, z_ref, cw_ref, cb_ref, dtb_ref, al_ref, d_ref, nw_ref, o_ref,
                xpad_scr, h_scr):
    c = pl.program_id(0)
    L = B_CHUNK

    @pl.when(c == 0)
    def _():
        h_scr[...] = jnp.zeros_like(h_scr)

    xpad_scr[0:8, :] = jnp.where(c > 0, xp_ref[L - 8:L, :], 0.0)
    xpad_scr[8:8 + L, :] = xc_ref[...]
    acc = cw_ref[0:1, :] * xpad_scr[pl.ds(8 - (B_CONV - 1), L), :]
    for i in range(1, B_CONV):
        acc = acc + cw_ref[i:i + 1, :] * xpad_scr[pl.ds(8 - (B_CONV - 1) + i, L), :]
    xa = _silu(acc + cb_ref[...])
    xs = xa[:, :B_INNER]
    bm = xa[:, B_INNER:B_INNER + B_GROUPS * B_STATE]
    cm = xa[:, B_INNER + B_GROUPS * B_STATE:]

    raw = dt_ref[...] + dtb_ref[...]
    dt = jnp.maximum(raw, 0.0) + jnp.log1p(jnp.exp(-jnp.abs(raw)))
    adt = dt * (-jnp.exp(al_ref[...]))
    ri = lax.broadcasted_iota(jnp.int32, (L, L), 0)
    ci = lax.broadcasted_iota(jnp.int32, (L, L), 1)
    tri = _mx(jnp.where(ri >= ci, 1.0, 0.0))
    acs = _dot_f32_lhs_exact(tri, adt)
    acs_t = acs.T
    er = lax.broadcasted_iota(jnp.int32, (LANES, B_INNER), 0)
    ec = lax.broadcasted_iota(jnp.int32, (LANES, B_INNER), 1)
    expand = _mx(jnp.where(ec // B_HEADDIM == er, 1.0, 0.0))
    dt_e = _dot_f32_rhs_exact(dt, expand)
    ea_e = _dot_f32_rhs_exact(jnp.exp(acs), expand)
    we_e = _dot_f32_rhs_exact(jnp.exp(acs[L - 1:L, :] - acs), expand)
    xdt = xs * dt_e
    xw = xdt * we_e
    lane = lax.broadcasted_iota(jnp.int32, (L, LANES), 1)
    hpg = B_HEADS // B_GROUPS
    gw = hpg * B_HEADDIM
    y_groups = []
    for g in range(B_GROUPS):
        b_g = bm[:, g * B_STATE:(g + 1) * B_STATE]
        c_g = _mx(cm[:, g * B_STATE:(g + 1) * B_STATE])
        cb = _dot_nt(c_g, _mx(b_g))
        pairs = []
        for pr in range(hpg // 2):
            x_pair = _mx(xdt[:, g * gw + pr * LANES:g * gw + (pr + 1) * LANES])
            halves = []
            for j in range(2):
                h = g * hpg + 2 * pr + j
                diff = acs[:, h:h + 1] - acs_t[h:h + 1, :]
                decay = jnp.exp(jnp.where(ri >= ci, diff, NEG_INF))
                halves.append(_dot(_mx(cb * decay), x_pair))
            pairs.append(jnp.where(lane < B_HEADDIM, halves[0], halves[1]))
        y_diag = jnp.concatenate(pairs, axis=1)
        h_prev = h_scr[:, g * gw:(g + 1) * gw]
        y_off = _dot(c_g, _mx(h_prev)) * ea_e[:, g * gw:(g + 1) * gw]
        st = _dot(_mx(b_g.T), _mx(xw[:, g * gw:(g + 1) * gw]))
        h_scr[:, g * gw:(g + 1) * gw] = h_prev * ea_e[L - 1:L, g * gw:(g + 1) * gw] + st
        y = y_diag + y_off + xs[:, g * gw:(g + 1) * gw] * d_ref[:, g * gw:(g + 1) * gw]
        yz = y * _silu(z_ref[:, g * gw:(g + 1) * gw])
        ms = jnp.mean(yz * yz, axis=-1, keepdims=True)
        y_groups.append(yz * lax.rsqrt(ms + EPS) * nw_ref[:, g * gw:(g + 1) * gw])
    o_ref[...] = jnp.concatenate(y_groups, axis=1)


def _ssd(proj, conv_w, conv_b, dtb_row, al_row, d_row, nw_row):
    s = proj.shape[0]
    L = B_CHUNK
    cx = _FCOLS["b_xbc"][0] // B_CONV_DIM
    cdt = _FCOLS["b_dt"][0] // LANES
    cz = _FCOLS["b_z"][0] // B_INNER
    row = lambda w: pl.BlockSpec((1, w), lambda c: (0, 0))
    return pl.pallas_call(
        _ssd_kernel,
        grid=(s // L,),
        in_specs=[pl.BlockSpec((L, B_CONV_DIM), lambda c: (c, cx)),
                  pl.BlockSpec((L, B_CONV_DIM), lambda c: (jnp.maximum(c - 1, 0), cx)),
                  pl.BlockSpec((L, LANES), lambda c: (c, cdt)),
                  pl.BlockSpec((L, B_INNER), lambda c: (c, cz)),
                  pl.BlockSpec((B_CONV, B_CONV_DIM), lambda c: (0, 0)),
                  row(B_CONV_DIM), row(LANES), row(LANES), row(B_INNER), row(B_INNER)],
        out_specs=pl.BlockSpec((L, B_INNER), lambda c: (c, 0)),
        out_shape=jax.ShapeDtypeStruct((s, B_INNER), F32),
        scratch_shapes=[pltpu.VMEM((L + 8, B_CONV_DIM), F32), pltpu.VMEM((B_STATE, B_INNER), F32)],
        compiler_params=_cparams(("arbitrary",)),
        name="ssd",
    )(proj, proj, proj, proj, conv_w, conv_b, dtb_row, al_row, d_row, nw_row)


def _rope_kernel(q_ref, k_ref, pos_ref, inv_ref, qo_ref, ko_ref):
    ang = pos_ref[...].astype(F32) * inv_ref[...]
    cos, sin = jnp.cos(ang), jnp.sin(ang)
    lane = lax.broadcasted_iota(jnp.int32, ang.shape, 1)
    low = (lane & (HD - 1)) < HD // 2

    def rot(t):
        partner = jnp.where(low, -pltpu.roll(t, LANES - HD // 2, 1), pltpu.roll(t, HD // 2, 1))
        return t * cos + partner * sin

    for j in range(q_ref.shape[1] // LANES):
        qo_ref[:, j * LANES:(j + 1) * LANES] = _mx(rot(q_ref[:, j * LANES:(j + 1) * LANES]))
    ko_ref[...] = _mx(rot(k_ref[...]))


def _rope(proj, pos_col, inv_row):
    s = proj.shape[0]
    tm = min(512, s)
    wq = C_HEADS * HD
    return pl.pallas_call(
        _rope_kernel,
        grid=(s // tm,),
        in_specs=[pl.BlockSpec((tm, wq), lambda i: (i, _FCOLS["c_q"][0] // wq)),
                  pl.BlockSpec((tm, LANES), lambda i: (i, _FCOLS["c_k"][0] // LANES)),
                  pl.BlockSpec((tm, 1), lambda i: (i, 0)),
                  pl.BlockSpec((1, LANES), lambda i: (0, 0))],
        out_specs=[pl.BlockSpec((tm, wq), lambda i: (i, 0)), pl.BlockSpec((tm, LANES), lambda i: (i, 0))],
        out_shape=[jax.ShapeDtypeStruct((s, wq), MXU_DTYPE), jax.ShapeDtypeStruct((s, LANES), MXU_DTYPE)],
        compiler_params=_cparams(("arbitrary",)),
        name="rope",
    )(proj, proj, pos_col, inv_row)


def _pair_rows(q2, low):
    zero = jnp.zeros_like(q2)
    return jnp.concatenate([jnp.where(low, q2, zero), jnp.where(low, zero, q2)], axis=0)


def _band_kernel(*refs, dil, rows, max_dist, has_bias, has_sink, want_lse):
    refs = list(refs)
    q_ref, k_ref, kh_ref, v_ref, vh_ref = refs[:5]
    pos = 5
    tb_ref = sk_ref = None
    if has_bias:
        tb_ref = refs[pos]
        pos += 1
    if has_sink:
        sk_ref = refs[pos]
        pos += 1
    o_ref = refs[pos]
    lse_ref = refs[pos + 1] if want_lse else None
    j = pl.program_id(0)
    i = pl.program_id(1)
    span = BLK * dil
    qi = lax.broadcasted_iota(jnp.int32, (BLK, 2 * BLK), 0)
    kj = lax.broadcasted_iota(jnp.int32, (BLK, 2 * BLK), 1)
    dist = BLK + qi - kj
    band = jnp.where((dist >= 0) & (dist <= max_dist), 0.0, NEG_INF)
    first = jnp.where(i > 0, band, jnp.where(kj >= BLK, band, NEG_INF))
    low = lax.broadcasted_iota(jnp.int32, (BLK, LANES), 1) < HD
    bias = None
    if has_bias:
        bias = jnp.concatenate([jnp.concatenate([tb_ref[a, 0], tb_ref[a, 1]], axis=1) for a in range(2)], axis=0)
    sink = None
    if has_sink:
        sink = jnp.concatenate([jnp.full((BLK, 1), sk_ref[2 * j + a], F32) for a in range(2)], axis=0)

    def take(ref, start):
        return ref[pl.ds(start, BLK, stride=dil), :] if dil > 1 else ref[start:start + BLK, :]

    def put(ref, start, val):
        if dil > 1:
            ref[pl.ds(start, BLK, stride=dil), :] = val
        else:
            ref[start:start + BLK, :] = val

    for sb in range(rows // span):
        for r in range(dil):
            base = sb * span + r
            qs = _mx(_pair_rows(take(q_ref, base).astype(F32), low))
            if sb == 0:
                k_prev, v_prev, mask = take(kh_ref, r), take(vh_ref, r), first
            else:
                k_prev, v_prev, mask = take(k_ref, base - span), take(v_ref, base - span), band
            k_win = _mx(jnp.concatenate([k_prev, take(k_ref, base)], axis=0))
            v_win = _mx(jnp.concatenate([v_prev, take(v_ref, base)], axis=0))
            s = _dot_nt(qs, k_win) + jnp.concatenate([mask, mask], axis=0)
            if has_bias:
                s = s + bias
            m = jnp.max(s, axis=-1, keepdims=True)
            if has_sink:
                m = jnp.maximum(m, sink)
            e = jnp.exp(s - m)
            den = jnp.sum(e, axis=-1, keepdims=True)
            if has_sink:
                den = den + jnp.exp(sink - m)
            o = _dot(_mx(e), v_win) / den
            put(o_ref, base, jnp.where(low, o[0:BLK], o[BLK:]))
            if want_lse:
                lse = jnp.broadcast_to(m + jnp.log(den), (2 * BLK, LANES))
                put(lse_ref, base, jnp.where(low, lse[0:BLK], lse[BLK:]))


def _band(q_arr, k_arr, v_arr, *, q_col, k_col, v_col, shared_kv, dil, max_dist, tb=None, sinks=None,
          want_lse=False):
    s = q_arr.shape[0]
    span = BLK * dil
    rows = max(min(2048, s), span)
    per = rows // span
    npair = A_HEADS * HD // LANES
    kv = (lambda j: 0) if shared_kv else (lambda j: j)
    in_specs = [pl.BlockSpec((rows, LANES), lambda j, i: (i, q_col + j)),
                pl.BlockSpec((rows, LANES), lambda j, i: (i, k_col + kv(j))),
                pl.BlockSpec((span, LANES), lambda j, i: (jnp.maximum(i * per - 1, 0), k_col + kv(j))),
                pl.BlockSpec((rows, LANES), lambda j, i: (i, v_col + kv(j))),
                pl.BlockSpec((span, LANES), lambda j, i: (jnp.maximum(i * per - 1, 0), v_col + kv(j)))]
    args = [q_arr, k_arr, k_arr, v_arr, v_arr]
    if tb is not None:
        in_specs.append(pl.BlockSpec((2, 2, BLK, LANES), lambda j, i: (j, 0, 0, 0)))
        args.append(tb)
    if sinks is not None:
        in_specs.append(pl.BlockSpec(memory_space=pltpu.SMEM))
        args.append(sinks)
    o_spec = pl.BlockSpec((rows, LANES), lambda j, i: (i, j))
    o_shape = jax.ShapeDtypeStruct((s, npair * LANES), F32)
    return pl.pallas_call(
        functools.partial(_band_kernel, dil=dil, rows=rows, max_dist=max_dist, has_bias=tb is not None,
                          has_sink=sinks is not None, want_lse=want_lse),
        grid=(npair, s // rows),
        in_specs=in_specs,
        out_specs=[o_spec, o_spec] if want_lse else o_spec,
        out_shape=[o_shape, o_shape] if want_lse else o_shape,
        compiler_params=_cparams(("arbitrary", "arbitrary")),
        name="band_attention",
    )(*args)


def _memkv_kernel(mem_ref, nw_ref, w_ref, o_ref):
    x = mem_ref[...]
    y = x * lax.rsqrt(jnp.mean(x * x, axis=-1, keepdims=True) + EPS) * nw_ref[...]
    o_ref[...] = _dot(_mx(y), w_ref[...])


def _memkv(mem2, norm_w, w):
    return pl.pallas_call(
        _memkv_kernel,
        out_shape=jax.ShapeDtypeStruct((mem2.shape[0], w.shape[1]), F32),
        compiler_params=pltpu.CompilerParams(vmem_limit_bytes=VMEM_LIMIT),
        name="mem_kv",
    )(mem2, norm_w, w)


def _memattn_kernel(q_ref, k_ref, v_ref, o_ref):
    tq = q_ref.shape[0]
    low = lax.broadcasted_iota(jnp.int32, (tq, LANES), 1) < HD
    for pr in range(M_HEADS // 2):
        cols = slice(pr * LANES, (pr + 1) * LANES)
        s = _dot_nt(_pair_rows(q_ref[:, cols], low), _mx(k_ref[:, cols]))
        e = jnp.exp(s - jnp.max(s, axis=-1, keepdims=True))
        o = _dot(_mx(e), _mx(v_ref[:, cols])) / jnp.sum(e, axis=-1, keepdims=True)
        o_ref[:, cols] = jnp.where(low, o[0:tq], o[tq:])


def _memattn(proj_b, kv):
    s = proj_b.shape[0]
    tq = min(256, s)
    ml = kv.shape[0]
    wm = M_HEADS * HD
    return pl.pallas_call(
        _memattn_kernel,
        grid=(s // tq,),
        in_specs=[pl.BlockSpec((tq, wm), lambda i: (i, _BCOLS["m_q"][0] // wm)),
                  pl.BlockSpec((ml, wm), lambda i: (0, 0)),
                  pl.BlockSpec((ml, wm), lambda i: (0, 1))],
        out_specs=pl.BlockSpec((tq, wm), lambda i: (i, 0)),
        out_shape=jax.ShapeDtypeStruct((s, wm), F32),
        compiler_params=_cparams(("arbitrary",)),
        name="mem_attention",
    )(proj_b, kv, kv)


def _outproj_kernel(x_ref, a_ref, az_ref, b_ref, c_ref, cz_ref, d0_ref, d1_ref, d2_ref, l0_ref, l1_ref, l2_ref,
                    dz_ref, m_ref, mz_ref, w_ref, nw_ref, o_ref):
    l0, l1, l2 = l0_ref[...], l1_ref[...], l2_ref[...]
    mx = jnp.maximum(jnp.maximum(l0, l1), l2)
    e0, e1, e2 = jnp.exp(l0 - mx), jnp.exp(l1 - mx), jnp.exp(l2 - mx)
    inv = 1.0 / (e0 + e1 + e2)
    d = (e0 * inv) * d0_ref[...] + (e1 * inv) * d1_ref[...] + (e2 * inv) * d2_ref[...]
    pieces = (a_ref[...] * _silu(az_ref[...]), b_ref[...], c_ref[...] * _silu(cz_ref[...]),
              d * _silu(dz_ref[...]), m_ref[...] * _silu(mz_ref[...]))
    y = None
    row = 0
    for piece in pieces:
        w = piece.shape[1]
        part = _dot(_mx(piece), w_ref[row:row + w, :])
        y = part if y is None else y + part
        row += w
    y = y * lax.rsqrt(jnp.mean(y * y, axis=-1, keepdims=True) + EPS) * nw_ref[...]
    o_ref[...] = x_ref[...] + y


def _outproj(x2, proj, a_out, b_out, c_out, d_outs, d_lses, m_out, w, norm_w):
    s = x2.shape[0]
    tm = min(512, s)
    wide = lambda width, col: pl.BlockSpec((tm, width), lambda i: (i, col))
    w512 = A_HEADS * HD
    wm = M_HEADS * HD
    return pl.pallas_call(
        _outproj_kernel,
        grid=(s // tm,),
        in_specs=[wide(D_MODEL, 0), wide(w512, 0), wide(w512, _FCOLS["a_z"][0] // w512), wide(w512, 0),
                  wide(w512, 0), wide(w512, _FCOLS["c_z"][0] // w512),
                  wide(w512, 0), wide(w512, 0), wide(w512, 0), wide(w512, 0), wide(w512, 0), wide(w512, 0),
                  wide(w512, _FCOLS["d_z"][0] // w512), wide(wm, 0), wide(wm, _FCOLS["m_z"][0] // wm),
                  pl.BlockSpec((MIX_WIDTH, D_MODEL), lambda i: (0, 0)),
                  pl.BlockSpec((1, D_MODEL), lambda i: (0, 0))],
        out_specs=wide(D_MODEL, 0),
        out_shape=jax.ShapeDtypeStruct((s, D_MODEL), F32),
        compiler_params=_cparams(("arbitrary",)),
        name="outproj",
    )(x2, a_out, proj, b_out, c_out, proj, *d_outs, *d_lses, proj, m_out, proj, w, norm_w)


def _c_heads(t, axis):
    shape = t.shape
    t = t.reshape(shape[:axis] + (C_HEADS, HD) + shape[axis + 1:])
    return jnp.take(t, jnp.array(_C_ORDER), axis=axis).reshape(shape)


def _repack_w_in(w, cols):
    out = []
    for name, (_, width) in cols.items():
        src, true_w, is_query = _SRC[name]
        piece = w[:, src:src + true_w]
        if is_query:
            piece = piece * HD ** -0.5
        if name in ("c_q", "c_z"):
            piece = _c_heads(piece, 1)
        if true_w < width:
            piece = jnp.pad(piece, ((0, 0), (0, width - true_w)))
        out.append(piece)
    return _mx(jnp.concatenate(out, axis=1))


def _heads_first_t(t, nh):
    s = t.shape[0]
    return jnp.transpose(t.reshape(s, nh, HD), (1, 2, 0))


def kernel(x, mem, positions, pre_norm, post_norm, w_in, w_out, rel_bias, a_cmp_pos, a_cmp_w1, a_cmp_w2,
           b_conv_w, b_conv_b, b_dt_bias, b_a_log, b_d, b_norm, c_sinks, m_norm, m_w_kv):
    b, s, _ = x.shape
    assert b == 1 and s % 2048 == 0, "sequence length must be a multiple of 2048 and batch 1"
    depth = w_in.shape[0]
    nb = s // BLK
    ncmp = s // CMP_STRIDE

    rel_a = rel_bias[:, :A_HEADS]
    rel_d = rel_bias[:, A_HEADS:]
    no_limit = 1 << 30
    c0_a = np.array([-2 * BLK] + [o * BLK for o in range(N_OFF)] + [A_WINDOW], np.int32)
    hi_a = np.array([no_limit] * (N_TAB - 1) + [A_WINDOW], np.int32)
    tb_a = _bias_table(rel_a, jnp.asarray(c0_a), a_row=-1, b_lane=1, scale=1, rows=BLK,
                       hi=jnp.asarray(hi_a))
    bc_a = _bias_table(rel_a, jnp.array([CMP_STRIDE * CMP_NEAR - (CMP_LEN - 1)], jnp.int32),
                       a_row=-CMP_STRIDE, b_lane=1, scale=1, rows=CMP_TAB)
    tb_d = [_bias_table(rel_d[:, p * D_SLOTS:(p + 1) * D_SLOTS], jnp.array([BLK, 0], jnp.int32), a_row=1,
                        b_lane=-1, scale=dil, rows=BLK) for p, (_, dil) in enumerate(D_PATTERNS)]
    key_pos = np.arange(s)
    key_aug_np = np.zeros((s, AUG_K - AUG_BASE), np.float32)
    key_aug_np[:, 0:2] = 1.0
    key_aug_np[key_pos, AUG_SEL - AUG_BASE + (key_pos // SEL_BLK) % (SEL_TILE // SEL_BLK)] = 1.0
    key_aug = jnp.asarray(key_aug_np, MXU_DTYPE)
    val_aug_np = np.zeros((V_ROWS - HD, s), np.float32)
    val_aug_np[0] = 1.0
    val_aug = jnp.asarray(val_aug_np, MXU_DTYPE)[None]
    head_slot = jnp.eye(A_KV, dtype=F32)
    half = ROPE_THETA ** (-jnp.arange(HD // 2, dtype=F32) / (HD // 2))
    inv_row = jnp.tile(half, LANES // (HD // 2)).reshape(1, LANES)
    pos_col = positions.reshape(s, 1)
    mem2 = mem.reshape(mem.shape[1], D_MODEL)

    x2 = x.reshape(s, D_MODEL)
    for layer in range(depth):
        norm_w = pre_norm[layer].reshape(1, D_MODEL)
        proj = _inproj(x2, norm_w, _repack_w_in(w_in[layer], _FCOLS), F32)
        proj_b = _inproj(x2, norm_w, _repack_w_in(w_in[layer], _BCOLS), MXU_DTYPE)
        fcol = lambda name: proj[:, _FCOLS[name][0]:_FCOLS[name][0] + _FCOLS[name][1]]
        bcol = lambda name: proj_b[:, _BCOLS[name][0]:_BCOLS[name][0] + _BCOLS[name][1]]

        wide = CMP_STRIDE * A_KV * HD
        u = jnp.stack([fcol("a_kc").reshape(ncmp, wide), fcol("a_vc").reshape(ncmp, wide)])
        pe = (a_cmp_pos[layer].reshape(2, 1, 2, CMP_STRIDE, 1, HD)
              * head_slot.reshape(1, A_KV, 1, 1, A_KV, 1)).reshape(2 * A_KV, 2, wide)
        w1 = _mx(a_cmp_w1[layer].reshape(2, 1, 2, CMP_STRIDE, 1, HD, CMP_HIDDEN)
                 * head_slot.reshape(1, A_KV, 1, 1, A_KV, 1, 1)).reshape(2 * A_KV, 2, wide, CMP_HIDDEN)
        cmp_out = _compress(u, pe, w1, _mx(a_cmp_w2[layer]), ncmp)
        kc_own = (cmp_out[:A_KV, :, None, :] * head_slot[:, None, :, None]).reshape(A_KV, ncmp, AUG_BASE)
        kc = _mx(jnp.concatenate([kc_own, jnp.ones((A_KV, ncmp, 2), F32),
                                  jnp.zeros((A_KV, ncmp, AUG_K - AUG_BASE - 2), F32)], axis=2))
        vct = _mx(jnp.transpose(cmp_out[A_KV:], (0, 2, 1)))
        gates = jnp.transpose(fcol("a_gate")[:, :3 * A_HEADS].reshape(nb, BLK, A_KV, A_G, 3),
                              (2, 0, 4, 3, 1)).reshape(A_KV, nb, 3, A_G * BLK)
        vst_aug = jnp.concatenate([_heads_first_t(bcol("a_vs"), A_KV),
                                   jnp.broadcast_to(val_aug, (A_KV, V_ROWS - HD, s))], axis=1)
        a_out = _nsa(proj_b, kc, vct, key_aug, vst_aug, _heads_first_t(bcol("a_vw"), A_KV), gates, tb_a, bc_a, s)

        pad_row = lambda v: jnp.pad(v, (0, LANES - v.shape[0])).reshape(1, LANES)
        b_out = _ssd(proj, b_conv_w[layer], b_conv_b[layer].reshape(1, B_CONV_DIM), pad_row(b_dt_bias[layer]),
                     pad_row(b_a_log[layer]), jnp.repeat(b_d[layer], B_HEADDIM).reshape(1, B_INNER),
                     b_norm[layer].reshape(1, B_INNER))

        cq_rot, ck_rot = _rope(proj, pos_col, inv_row)
        c_out = _band(cq_rot, ck_rot, proj_b, q_col=0, k_col=0, v_col=_BCOLS["c_v"][0] // LANES, shared_kv=True,
                      dil=1, max_dist=C_WINDOW - 1, sinks=c_sinks[layer][jnp.array(_C_ORDER)])

        d_outs, d_lses = [], []
        for p, (window, dil) in enumerate(D_PATTERNS):
            o_p, lse_p = _band(proj, proj, proj, q_col=_FCOLS["d_q"][0] // LANES + p * D_SLOTS * HD // LANES,
                               k_col=_FCOLS["d_k"][0] // LANES, v_col=_FCOLS["d_v"][0] // LANES, shared_kv=False,
                               dil=dil, max_dist=window // dil, tb=tb_d[p], want_lse=True)
            d_outs.append(o_p)
            d_lses.append(lse_p)

        kv = _memkv(mem2, m_norm[layer].reshape(1, D_MODEL), _mx(m_w_kv[layer]))
        m_out = _memattn(proj_b, kv)

        c_rows = slice(A_HEADS * HD + B_INNER, A_HEADS * HD + B_INNER + C_HEADS * HD)
        w_o = w_out[layer].at[c_rows].set(_c_heads(w_out[layer][c_rows], 0))
        x2 = _outproj(x2, proj, a_out, b_out, c_out, d_outs, d_lses, m_out, _mx(w_o),
                      post_norm[layer].reshape(1, D_MODEL))
    return x2.reshape(b, s, D_MODEL)
```

```python
import functools
import math

import numpy as np
import jax
import jax.numpy as jnp
from jax import lax
from jax.experimental import pallas as pl
from jax.experimental.pallas import tpu as pltpu

F32 = jnp.float32
MXU_DTYPE = jnp.bfloat16

D_MODEL = 1024
HD = 64
BLK = 128
NEG_INF = -1e30
TINY = 1e-30
EPS = 1e-6

A_HEADS, A_KV = 8, 2
A_G = A_HEADS // A_KV
CMP_LEN, CMP_STRIDE, CMP_HIDDEN = 32, 16, 128
SEL_BLK, SEL_TOPK, A_WINDOW = 64, 16, 512
B_HEADS, B_HEADDIM, B_GROUPS, B_STATE, B_CONV, B_CHUNK = 8, 64, 2, 128, 4, 128
B_INNER = B_HEADS * B_HEADDIM
B_CONV_DIM = B_INNER + 2 * B_GROUPS * B_STATE
C_HEADS, C_KV, C_WINDOW = 8, 2, 128
ROPE_THETA = 150000.0
D_SLOTS = 8
D_PATTERNS = ((128, 1), (512, 4), (2048, 16))
D_NPAT = 3
M_HEADS = 4
N_BUCKETS, MAX_DIST = 32, 2048
MIX_WIDTH = A_HEADS * HD + B_INNER + C_HEADS * HD + D_SLOTS * HD + M_HEADS * HD

LANES = 128
VMEM_LIMIT = 56 * 1024 * 1024

_SRC = {"a_q": (0, 512, True), "a_kc": (512, 128, False), "a_vc": (640, 128, False), "a_ks": (768, 128, False),
        "a_vs": (896, 128, False), "a_kw": (1024, 128, False), "a_vw": (1152, 128, False),
        "a_gate": (1280, 24, False), "a_z": (1304, 512, False), "b_xbc": (1816, 1024, False),
        "b_dt": (2840, 8, False), "b_z": (2848, 512, False), "c_q": (3360, 512, True), "c_k": (3872, 128, False),
        "c_v": (4000, 128, False), "c_z": (4128, 512, False), "d_q": (4640, 1536, True), "d_k": (6176, 512, False),
        "d_v": (6688, 512, False), "d_z": (7200, 512, False), "m_q": (7712, 256, True), "m_z": (7968, 256, False)}


def _layout(pieces):
    cols, off = {}, 0
    for name, width in pieces:
        cols[name] = (off, width)
        off += width
    return cols, off


_FCOLS, N_F32 = _layout((("b_xbc", 1024), ("a_z", 512), ("b_z", 512), ("c_z", 512), ("d_z", 512), ("c_q", 512),
                         ("d_q", 1536), ("d_k", 512), ("d_v", 512), ("m_z", 256), ("a_kc", 128), ("a_vc", 128),
                         ("c_k", 128), ("a_gate", 128), ("b_dt", 128)))
_BCOLS, N_BF16 = _layout((("a_q", 512), ("m_q", 256), ("a_ks", 128), ("a_vs", 128), ("a_kw", 128),
                          ("a_vw", 128), ("c_v", 128)))
PROJ_TN = 1408
_C_ORDER = tuple(h for j in range(C_HEADS // C_KV) for h in (j, j + C_HEADS // C_KV))


def _t5_thresholds():
    d = np.arange(0, 4 * MAX_DIST)
    exact = N_BUCKETS // 2
    rel = np.maximum(d, exact).astype(np.float64)
    large = exact + (np.log(rel / exact) / math.log(MAX_DIST / exact) * (N_BUCKETS - exact)).astype(np.int64)
    bucket = np.where(d < exact, d, np.minimum(large, N_BUCKETS - 1))
    return tuple(int(np.argmax(bucket >= b)) for b in range(1, N_BUCKETS))


_THR = _t5_thresholds()
FAR_DIST = _THR[-1]
N_OFF = -(-(FAR_DIST + BLK) // BLK) + 1
TAB_OFF0 = 1
TAB_EDGE = TAB_OFF0 + N_OFF
N_TAB = TAB_EDGE + 1
FAR_WIDTH = 4
CMP_REACH = -(-FAR_DIST // CMP_STRIDE) // 8 * 8 + 8
CMP_WIN = 128
CMP_NEAR = 128
CMP_TAB = CMP_NEAR + CMP_WIN
SEL_TILE = 512
AUG_BASE = LANES
AUG_SEL = AUG_BASE + 8
AUG_END = AUG_SEL + SEL_TILE // SEL_BLK
AUG_K = 2 * LANES
V_ROWS = HD + 16


def _cparams(sem):
    return pltpu.CompilerParams(dimension_semantics=sem, vmem_limit_bytes=VMEM_LIMIT)


def _mx(x):
    return x.astype(MXU_DTYPE)


def _dot(a, b):
    return jnp.dot(a, b, preferred_element_type=F32)


def _dot_nt(a, b):
    return lax.dot_general(a, b, (((1,), (1,)), ((), ())), preferred_element_type=F32)


def _split3(a):
    hi = _mx(a)
    r1 = a - hi.astype(F32)
    mid = _mx(r1)
    lo = _mx(r1 - mid.astype(F32))
    return hi, mid, lo


def _dot_f32_rhs_exact(a, b):
    hi, mid, lo = _split3(a)
    return _dot(hi, b) + _dot(mid, b) + _dot(lo, b)


def _dot_f32_lhs_exact(a, b):
    hi, mid, lo = _split3(b)
    return _dot(a, hi) + _dot(a, mid) + _dot(a, lo)


def _silu(x):
    return x * jax.nn.sigmoid(x)


def _bias_table_kernel(rel_ref, c0_ref, hi_ref, o_ref, *, a_row, b_lane, scale, rows, masked):
    h = pl.program_id(0)
    t = pl.program_id(1)
    r = lax.broadcasted_iota(jnp.int32, (rows, LANES), 0)
    l = lax.broadcasted_iota(jnp.int32, (rows, LANES), 1)
    dist = (c0_ref[t] + a_row * r + b_lane * l) * scale
    out = jnp.full((rows, LANES), rel_ref[0, h], F32)
    for b in range(1, N_BUCKETS):
        out = jnp.where(dist >= _THR[b - 1], rel_ref[b, h], out)
    if masked:
        out = jnp.where(dist < 0, NEG_INF, jnp.where(dist >= hi_ref[t], NEG_INF, out))
    o_ref[0, 0] = out


def _bias_table(rel_cols, c0, *, a_row, b_lane, scale, rows, hi=None):
    nh = rel_cols.shape[1]
    nt = c0.shape[0]
    return pl.pallas_call(
        functools.partial(_bias_table_kernel, a_row=a_row, b_lane=b_lane, scale=scale, rows=rows,
                          masked=hi is not None),
        grid_spec=pltpu.PrefetchScalarGridSpec(
            num_scalar_prefetch=3, grid=(nh, nt), in_specs=[],
            out_specs=pl.BlockSpec((1, 1, rows, LANES), lambda h, t, *_: (h, t, 0, 0))),
        out_shape=jax.ShapeDtypeStruct((nh, nt, rows, LANES), F32),
        compiler_params=_cparams(("arbitrary", "arbitrary")),
        name="bias_table",
    )(rel_cols, c0, c0 if hi is None else hi)


def _inproj_kernel(x_ref, nw_ref, w_ref, o_ref, h_scr):
    @pl.when(pl.program_id(1) == 0)
    def _():
        x = x_ref[...]
        y = x * lax.rsqrt(jnp.mean(x * x, axis=-1, keepdims=True) + EPS)
        h_scr[...] = _mx(y * nw_ref[...])

    o_ref[...] = _dot(h_scr[...], w_ref[...]).astype(o_ref.dtype)


def _inproj(x2, norm_w, w, out_dtype):
    s = x2.shape[0]
    tm, tn = min(1024, s), PROJ_TN
    return pl.pallas_call(
        _inproj_kernel,
        grid=(s // tm, w.shape[1] // tn),
        in_specs=[pl.BlockSpec((tm, D_MODEL), lambda i, j: (i, 0)),
                  pl.BlockSpec((1, D_MODEL), lambda i, j: (0, 0)),
                  pl.BlockSpec((D_MODEL, tn), lambda i, j: (0, j))],
        out_specs=pl.BlockSpec((tm, tn), lambda i, j: (i, j)),
        out_shape=jax.ShapeDtypeStruct((s, w.shape[1]), out_dtype),
        scratch_shapes=[pltpu.VMEM((tm, D_MODEL), MXU_DTYPE)],
        compiler_params=_cparams(("arbitrary", "arbitrary")),
        name="inproj",
    )(x2, norm_w, w)


def _compress_kernel(u_ref, pe_ref, w1_ref, w2_ref, o_ref, h2_scr, *, ncmp):
    u = u_ref[0]
    pe_first, pe_second = pe_ref[0, 0:1, :], pe_ref[0, 1:2, :]
    h1 = _dot(_mx(u + pe_first), w1_ref[0, 0])
    h2_scr[0:ncmp, :] = _dot(_mx(u + pe_second), w1_ref[0, 1])
    h2_scr[ncmp:ncmp + 8, :] = _dot(_mx(jnp.broadcast_to(pe_second, (8, pe_second.shape[1]))), w1_ref[0, 1])
    pre = h1 + h2_scr[pl.ds(1, ncmp), :]
    o_ref[0] = _dot(_mx(jax.nn.gelu(pre)), w2_ref[0])


def _compress(u, pe, w1, w2, ncmp):
    wide = CMP_STRIDE * A_KV * HD
    return pl.pallas_call(
        functools.partial(_compress_kernel, ncmp=ncmp),
        grid=(2 * A_KV,),
        in_specs=[pl.BlockSpec((1, ncmp, wide), lambda i: (i // A_KV, 0, 0)),
                  pl.BlockSpec((1, 2, wide), lambda i: (i, 0, 0)),
                  pl.BlockSpec((1, 2, wide, CMP_HIDDEN), lambda i: (i, 0, 0, 0)),
                  pl.BlockSpec((1, CMP_HIDDEN, HD), lambda i: (i // A_KV, 0, 0))],
        out_specs=pl.BlockSpec((1, ncmp, HD), lambda i: (i, 0, 0)),
        out_shape=jax.ShapeDtypeStruct((2 * A_KV, ncmp, HD), F32),
        scratch_shapes=[pltpu.VMEM((ncmp + 8, CMP_HIDDEN), F32)],
        compiler_params=_cparams(("arbitrary",)),
        name="nsa_compress",
    )(u, pe, w1, w2)


def _nsa_kernel(q_ref, kc_ref, vct_ref, ks_ref, ka_ref, vst_ref, kw_ref, vwt_ref, g_ref, tb_ref, bc_ref,
                o_ref, s_scr, sel_scr, qa_scr, pcs_scr, *score_bufs, seq, tk):
    sa_scr = score_bufs[0]
    hk = pl.program_id(0)
    n = pl.program_id(1)
    ncmp = seq // CMP_STRIDE
    nsel = seq // SEL_BLK
    n_top = min(SEL_TOPK, nsel)
    r_all = A_G * BLK
    q_rows = q_ref[...].astype(F32)
    halves = [q_rows[:, p * LANES:(p + 1) * LANES].T for p in range(A_G // 2)]
    q_t = _mx(jnp.concatenate([h[r * HD:(r + 1) * HD] for h in halves for r in range(2)], axis=1))
    far_row = jnp.concatenate([tb_ref[g, TAB_OFF0 + N_OFF - 1, 0:1, :] for g in range(A_G)], axis=1)

    def tile_bias(o, edge=None):
        idx = jnp.clip(o, -1, N_OFF - 1) + TAB_OFF0
        if edge is not None:
            idx = jnp.where(o == edge, TAB_EDGE, idx)
        return jnp.concatenate([tb_ref[g, idx] for g in range(A_G)], axis=1)

    far_hi = _mx(far_row).astype(F32)
    bias_rows = jnp.concatenate([far_hi, far_row - far_hi, jnp.zeros((AUG_SEL - AUG_BASE - 2, r_all), F32)], axis=0)
    qa_scr[0:AUG_BASE, :] = jnp.zeros((AUG_BASE, r_all), MXU_DTYPE)
    qa_scr[pl.ds(pl.multiple_of(hk * HD, HD), HD), :] = q_t
    qa_scr[AUG_BASE:AUG_END, :] = _mx(
        jnp.concatenate([bias_rows, jnp.zeros((AUG_END - AUG_SEL, r_all), F32)], axis=0))
    qa_scr[AUG_END:, :] = jnp.zeros((AUG_K - AUG_END, r_all), MXU_DTYPE)
    q_own = qa_scr[0:AUG_BASE, :]

    tok = BLK // CMP_STRIDE
    s_scr[...] = _dot(kc_ref[0], qa_scr[...])
    st = jnp.clip((tok * n - CMP_REACH) // 16 * 16, 0, ncmp - CMP_WIN)
    off = pl.multiple_of(st - tok * n + CMP_NEAR, 8)
    st = pl.multiple_of(st, 16)
    near = jnp.concatenate([bc_ref[g, 0, pl.ds(off, CMP_WIN), :] for g in range(A_G)], axis=1)
    s_scr[pl.ds(st, CMP_WIN), :] = _dot(kc_ref[0, pl.ds(st, CMP_WIN), 0:AUG_BASE], q_own) + near

    nwb = A_WINDOW // BLK + 1
    kb0 = jnp.maximum(n - (nwb - 1), 0)
    k0 = pl.multiple_of(kb0 * BLK, BLK)
    s = _dot(kw_ref[pl.ds(k0, nwb * BLK), :], q_own)
    s = s + jnp.concatenate([tile_bias(n - kb0 - i, edge=nwb - 1) for i in range(nwb)], axis=0)
    e = jnp.exp(s - jnp.max(s, axis=0, keepdims=True))
    o_w = _dot(vwt_ref[0, :, pl.ds(k0, nwb * BLK)], _mx(e)) / jnp.sum(e, axis=0, keepdims=True)

    jrow = lax.broadcasted_iota(jnp.int32, (ncmp, BLK), 0)
    qpos = n * BLK + lax.broadcasted_iota(jnp.int32, (ncmp, BLK), 1)
    visible = jnp.where(jrow * CMP_STRIDE + (CMP_LEN - 1) <= qpos, 0.0, NEG_INF)
    s_c = s_scr[...] + jnp.concatenate([visible] * A_G, axis=1)
    m = jnp.max(s_c, axis=0, keepdims=True)
    e = jnp.exp(s_c - m)
    den = jnp.sum(e, axis=0, keepdims=True)
    inv = jnp.where(m > 0.5 * NEG_INF, 1.0 / jnp.maximum(den, TINY), 0.0)
    p_c = e * inv
    o_c = _dot(vct_ref[0], _mx(p_c))
    pcs = p_c[:, 0:BLK]
    for g in range(1, A_G):
        pcs = pcs + p_c[:, g * BLK:(g + 1) * BLK]
    per = SEL_BLK // CMP_STRIDE
    pcs_scr[0:8, :] = jnp.zeros((8, BLK), F32)
    pcs_scr[8:8 + ncmp, :] = pcs
    part = [pcs_scr[pl.ds(8 + i, nsel, stride=per), :] for i in range(-1, per)]
    imp = part[1] + part[0]
    for i in range(1, per):
        imp = imp + (part[i + 1] + part[i])

    b_io = lax.broadcasted_iota(jnp.int32, (nsel, BLK), 0)
    b_f = b_io.astype(F32)
    pos = n * BLK + lax.broadcasted_iota(jnp.int32, (nsel, BLK), 1)
    forced = (b_io == pos // SEL_BLK) | (b_io == 0)
    valid = b_io * SEL_BLK <= pos
    imp = jnp.where(forced, -2.0, jnp.where(valid, imp, -1.0))

    def pick_one(_, carry):
        imp, sel = carry
        vals = [imp[8 * g:8 * (g + 1)] for g in range(nsel // 8)]
        idxs = [b_f[8 * g:8 * (g + 1)] for g in range(nsel // 8)]
        while len(vals) > 1:
            later = [vals[a + 1] > vals[a] for a in range(0, len(vals), 2)]
            idxs = [jnp.where(c, idxs[2 * a + 1], idxs[2 * a]) for a, c in enumerate(later)]
            vals = [jnp.where(c, vals[2 * a + 1], vals[2 * a]) for a, c in enumerate(later)]
        v, ix = vals[0], idxs[0]
        top = jnp.max(v, axis=0, keepdims=True)
        first = jnp.min(jnp.where(v == top, ix, float(nsel)), axis=0, keepdims=True)
        hit = b_f == first
        sel = jnp.where(hit, jnp.where(top >= 0.0, 0.0, NEG_INF), sel)
        return jnp.where(hit, -2.0, imp), sel

    _, sel = lax.fori_loop(0, n_top - 2, pick_one, (imp, jnp.where(forced, 0.0, NEG_INF)))
    sel_scr[...] = sel

    nblk_t = tk // BLK
    no_bias_rows = jnp.zeros((AUG_SEL - AUG_BASE, r_all), F32)

    def scores(t, const_rows):
        k0 = pl.multiple_of(t * tk, tk)
        picked = sel_scr[pl.ds(pl.multiple_of((AUG_END - AUG_SEL) * t, 8), AUG_END - AUG_SEL), :]
        aug = _mx(jnp.concatenate([const_rows, jnp.concatenate([picked] * A_G, axis=1)], axis=0))
        q_aug = jnp.concatenate([q_own, aug, jnp.zeros((AUG_K - AUG_END, r_all), MXU_DTYPE)], axis=0)
        keys = jnp.concatenate([ks_ref[pl.ds(k0, tk), :], ka_ref[pl.ds(k0, tk), :]], axis=1)
        return _dot(keys, q_aug)

    last = (n * BLK) // tk

    def tile_scores(t, near):
        if not near:
            return scores(t, bias_rows)
        bias = jnp.concatenate([tile_bias(n - (nblk_t * t + i)) for i in range(nblk_t)], axis=0)
        return scores(t, no_bias_rows) + bias

    def accumulate(t, s, m_i, acc, m_tile):
        k0 = pl.multiple_of(t * tk, tk)
        m_new = jnp.maximum(m_i, m_tile)
        p = jnp.exp(s - m_new)
        acc = acc * jnp.exp(m_i - m_new) + _dot(vst_ref[0, :, pl.ds(k0, tk)], _mx(p))
        return m_new, acc

    def stage1(t, buf, near):
        s = tile_scores(t, near)
        buf[...] = s
        return jnp.max(s, axis=0, keepdims=True)

    def run_pipelined(first, trips, width, near, m_i, acc):
        bufs = score_bufs[:width]

        def trip(u, carry):
            m_i, acc, m_cur = carry
            t0 = first + width * u
            for k in range(width):
                m_next = stage1(t0 + k + 1, bufs[(k + 1) % width], near)
                m_i, acc = accumulate(t0 + k, bufs[k][...], m_i, acc, m_cur)
                m_cur = m_next
            return m_i, acc, m_cur

        return lax.fori_loop(0, trips, trip, (m_i, acc, stage1(first, sa_scr, near)))

    def single(t, carry):
        s = tile_scores(t, True)
        return accumulate(t, s, *carry, jnp.max(s, axis=0, keepdims=True))

    far_tiles = jnp.maximum(n - (N_OFF - 2), 0) // nblk_t
    far_trips = jnp.maximum(far_tiles - 1, 0) // FAR_WIDTH
    m_i, acc_s, m_t = run_pipelined(0, far_trips, FAR_WIDTH, False, jnp.full((1, r_all), NEG_INF, F32),
                                    jnp.zeros((V_ROWS, r_all), F32))
    far_done = FAR_WIDTH * far_trips
    has_far = jnp.minimum(far_tiles, 1)
    m_i, acc_s = lax.fori_loop(0, has_far, lambda _, c: accumulate(far_done, sa_scr[...], *c, m_t), (m_i, acc_s))
    start = far_done + has_far
    near_pairs = (last - start) // 2
    m_i, acc_s, m_t = run_pipelined(start, near_pairs, 2, True, m_i, acc_s)
    m_i, acc_s = accumulate(start + 2 * near_pairs, sa_scr[...], m_i, acc_s, m_t)
    _, acc_s = lax.fori_loop(start + 2 * near_pairs + 1, last + 1, single, (m_i, acc_s))
    o_s = acc_s[0:HD] / jnp.maximum(acc_s[HD:HD + 1], TINY)

    gates = jax.nn.sigmoid(g_ref[0, 0])
    o_t = gates[0:1] * o_c + gates[1:2] * o_s + gates[2:3] * o_w
    for pair in range(A_G // 2):
        both = jnp.concatenate([o_t[:, (2 * pair) * BLK:(2 * pair + 1) * BLK],
                                o_t[:, (2 * pair + 1) * BLK:(2 * pair + 2) * BLK]], axis=0)
        o_ref[:, pair * LANES:(pair + 1) * LANES] = both.T


def _nsa(proj_b, kc, vct, key_aug, vst, vwt, gates, tb_a, bc, seq):
    nb = seq // BLK
    ncmp = seq // CMP_STRIDE
    nsel = seq // SEL_BLK
    tk = SEL_TILE
    full = lambda h, n: (h, 0, 0)
    wq = A_G * HD
    resident = lambda name: pl.BlockSpec((seq, LANES), lambda h, n: (0, _BCOLS[name][0] // LANES))
    return pl.pallas_call(
        functools.partial(_nsa_kernel, seq=seq, tk=tk),
        grid=(A_KV, nb),
        in_specs=[pl.BlockSpec((BLK, wq), lambda h, n: (n, _BCOLS["a_q"][0] // wq + h)),
                  pl.BlockSpec((1, ncmp, AUG_K), full),
                  pl.BlockSpec((1, HD, ncmp), full),
                  resident("a_ks"),
                  pl.BlockSpec((seq, LANES), lambda h, n: (0, 0)),
                  pl.BlockSpec((1, V_ROWS, seq), full),
                  resident("a_kw"),
                  pl.BlockSpec((1, HD, seq), full),
                  pl.BlockSpec((1, 1, 3, A_G * BLK), lambda h, n: (h, n, 0, 0)),
                  pl.BlockSpec((A_G, N_TAB, BLK, LANES), lambda h, n: (h, 0, 0, 0)),
                  pl.BlockSpec((A_G, 1, CMP_TAB, LANES), lambda h, n: (h, 0, 0, 0))],
        out_specs=pl.BlockSpec((BLK, A_G * HD), lambda h, n: (n, h)),
        out_shape=jax.ShapeDtypeStruct((seq, A_HEADS * HD), F32),
        scratch_shapes=[pltpu.VMEM((ncmp, A_G * BLK), F32), pltpu.VMEM((nsel, BLK), F32),
                        pltpu.VMEM((AUG_K, A_G * BLK), MXU_DTYPE), pltpu.VMEM((ncmp + 8, BLK), F32),
                        *[pltpu.VMEM((tk, A_G * BLK), F32) for _ in range(FAR_WIDTH)]],
        compiler_params=_cparams(("arbitrary", "arbitrary")),
        name="nsa",
    )(proj_b, kc, vct, proj_b, key_aug, vst, proj_b, vwt, gates, tb_a, bc)


def _ssd_kernel(xc_ref, xp_ref, dt_ref, z_ref, cw_ref, cb_ref, dtb_ref, al_ref, d_ref, nw_ref, o_ref,
                xpad_scr, h_scr):
    c = pl.program_id(0)
    L = B_CHUNK

    @pl.when(c == 0)
    def _():
        h_scr[...] = jnp.zeros_like(h_scr)

    xpad_scr[0:8, :] = jnp.where(c > 0, xp_ref[L - 8:L, :], 0.0)
    xpad_scr[8:8 + L, :] = xc_ref[...]
    acc = cw_ref[0:1, :] * xpad_scr[pl.ds(8 - (B_CONV - 1), L), :]
    for i in range(1, B_CONV):
        acc = acc + cw_ref[i:i + 1, :] * xpad_scr[pl.ds(8 - (B_CONV - 1) + i, L), :]
    xa = _silu(acc + cb_ref[...])
    xs = xa[:, :B_INNER]
    bm = xa[:, B_INNER:B_INNER + B_GROUPS * B_STATE]
    cm = xa[:, B_INNER + B_GROUPS * B_STATE:]

    raw = dt_ref[...] + dtb_ref[...]
    dt = jnp.maximum(raw, 0.0) + jnp.log1p(jnp.exp(-jnp.abs(raw)))
    adt = dt * (-jnp.exp(al_ref[...]))
    ri = lax.broadcasted_iota(jnp.int32, (L, L), 0)
    ci = lax.broadcasted_iota(jnp.int32, (L, L), 1)
    tri = _mx(jnp.where(ri >= ci, 1.0, 0.0))
    acs = _dot_f32_lhs_exact(tri, adt)
    acs_t = acs.T
    er = lax.broadcasted_iota(jnp.int32, (LANES, B_INNER), 0)
    ec = lax.broadcasted_iota(jnp.int32, (LANES, B_INNER), 1)
    expand = _mx(jnp.where(ec // B_HEADDIM == er, 1.0, 0.0))
    dt_e = _dot_f32_rhs_exact(dt, expand)
    ea_e = _dot_f32_rhs_exact(jnp.exp(acs), expand)
    we_e = _dot_f32_rhs_exact(jnp.exp(acs[L - 1:L, :] - acs), expand)
    xdt = xs * dt_e
    xw = xdt * we_e
    lane = lax.broadcasted_iota(jnp.int32, (L, LANES), 1)
    hpg = B_HEADS // B_GROUPS
    gw = hpg * B_HEADDIM
    y_groups = []
    for g in range(B_GROUPS):
        b_g = bm[:, g * B_STATE:(g + 1) * B_STATE]
        c_g = _mx(cm[:, g * B_STATE:(g + 1) * B_STATE])
        cb = _dot_nt(c_g, _mx(b_g))
        pairs = []
        for pr in range(hpg // 2):
            x_pair = _mx(xdt[:, g * gw + pr * LANES:g * gw + (pr + 1) * LANES])
            halves = []
            for j in range(2):
                h = g * hpg + 2 * pr + j
                diff = acs[:, h:h + 1] - acs_t[h:h + 1, :]
                decay = jnp.exp(jnp.where(ri >= ci, diff, NEG_INF))
                halves.append(_dot(_mx(cb * decay), x_pair))
            pairs.append(jnp.where(lane < B_HEADDIM, halves[0], halves[1]))
        y_diag = jnp.concatenate(pairs, axis=1)
        h_prev = h_scr[:, g * gw:(g + 1) * gw]
        y_off = _dot(c_g, _mx(h_prev)) * ea_e[:, g * gw:(g + 1) * gw]
        st = _dot(_mx(b_g.T), _mx(xw[:, g * gw:(g + 1) * gw]))
        h_scr[:, g * gw:(g + 1) * gw] = h_prev * ea_e[L - 1:L, g * gw:(g + 1) * gw] + st
        y = y_diag + y_off + xs[:, g * gw:(g + 1) * gw] * d_ref[:, g * gw:(g + 1) * gw]
        yz = y * _silu(z_ref[:, g * gw:(g + 1) * gw])
        ms = jnp.mean(yz * yz, axis=-1, keepdims=True)
        y_groups.append(yz * lax.rsqrt(ms + EPS) * nw_ref[:, g * gw:(g + 1) * gw])
    o_ref[...] = jnp.concatenate(y_groups, axis=1)


def _ssd(proj, conv_w, conv_b, dtb_row, al_row, d_row, nw_row):
    s = proj.shape[0]
    L = B_CHUNK
    cx = _FCOLS["b_xbc"][0] // B_CONV_DIM
    cdt = _FCOLS["b_dt"][0] // LANES
    cz = _FCOLS["b_z"][0] // B_INNER
    row = lambda w: pl.BlockSpec((1, w), lambda c: (0, 0))
    return pl.pallas_call(
        _ssd_kernel,
        grid=(s // L,),
        in_specs=[pl.BlockSpec((L, B_CONV_DIM), lambda c: (c, cx)),
                  pl.BlockSpec((L, B_CONV_DIM), lambda c: (jnp.maximum(c - 1, 0), cx)),
                  pl.BlockSpec((L, LANES), lambda c: (c, cdt)),
                  pl.BlockSpec((L, B_INNER), lambda c: (c, cz)),
                  pl.BlockSpec((B_CONV, B_CONV_DIM), lambda c: (0, 0)),
                  row(B_CONV_DIM), row(LANES), row(LANES), row(B_INNER), row(B_INNER)],
        out_specs=pl.BlockSpec((L, B_INNER), lambda c: (c, 0)),
        out_shape=jax.ShapeDtypeStruct((s, B_INNER), F32),
        scratch_shapes=[pltpu.VMEM((L + 8, B_CONV_DIM), F32), pltpu.VMEM((B_STATE, B_INNER), F32)],
        compiler_params=_cparams(("arbitrary",)),
        name="ssd",
    )(proj, proj, proj, proj, conv_w, conv_b, dtb_row, al_row, d_row, nw_row)


def _rope_kernel(q_ref, k_ref, pos_ref, inv_ref, qo_ref, ko_ref):
    ang = pos_ref[...].astype(F32) * inv_ref[...]
    cos, sin = jnp.cos(ang), jnp.sin(ang)
    lane = lax.broadcasted_iota(jnp.int32, ang.shape, 1)
    low = (lane & (HD - 1)) < HD // 2

    def rot(t):
        partner = jnp.where(low, -pltpu.roll(t, LANES - HD // 2, 1), pltpu.roll(t, HD // 2, 1))
        return t * cos + partner * sin

    for j in range(q_ref.shape[1] // LANES):
        qo_ref[:, j * LANES:(j + 1) * LANES] = _mx(rot(q_ref[:, j * LANES:(j + 1) * LANES]))
    ko_ref[...] = _mx(rot(k_ref[...]))


def _rope(proj, pos_col, inv_row):
    s = proj.shape[0]
    tm = min(512, s)
    wq = C_HEADS * HD
    return pl.pallas_call(
        _rope_kernel,
        grid=(s // tm,),
        in_specs=[pl.BlockSpec((tm, wq), lambda i: (i, _FCOLS["c_q"][0] // wq)),
                  pl.BlockSpec((tm, LANES), lambda i: (i, _FCOLS["c_k"][0] // LANES)),
                  pl.BlockSpec((tm, 1), lambda i: (i, 0)),
                  pl.BlockSpec((1, LANES), lambda i: (0, 0))],
        out_specs=[pl.BlockSpec((tm, wq), lambda i: (i, 0)), pl.BlockSpec((tm, LANES), lambda i: (i, 0))],
        out_shape=[jax.ShapeDtypeStruct((s, wq), MXU_DTYPE), jax.ShapeDtypeStruct((s, LANES), MXU_DTYPE)],
        compiler_params=_cparams(("arbitrary",)),
        name="rope",
    )(proj, proj, pos_col, inv_row)


def _pair_rows(q2, low):
    zero = jnp.zeros_like(q2)
    return jnp.concatenate([jnp.where(low, q2, zero), jnp.where(low, zero, q2)], axis=0)


def _band_kernel(*refs, dil, rows, max_dist, has_bias, has_sink, want_lse):
    refs = list(refs)
    q_ref, k_ref, kh_ref, v_ref, vh_ref = refs[:5]
    pos = 5
    tb_ref = sk_ref = None
    if has_bias:
        tb_ref = refs[pos]
        pos += 1
    if has_sink:
        sk_ref = refs[pos]
        pos += 1
    o_ref = refs[pos]
    lse_ref = refs[pos + 1] if want_lse else None
    j = pl.program_id(0)
    i = pl.program_id(1)
    span = BLK * dil
    qi = lax.broadcasted_iota(jnp.int32, (BLK, 2 * BLK), 0)
    kj = lax.broadcasted_iota(jnp.int32, (BLK, 2 * BLK), 1)
    dist = BLK + qi - kj
    band = jnp.where((dist >= 0) & (dist <= max_dist), 0.0, NEG_INF)
    first = jnp.where(i > 0, band, jnp.where(kj >= BLK, band, NEG_INF))
    low = lax.broadcasted_iota(jnp.int32, (BLK, LANES), 1) < HD
    bias = None
    if has_bias:
        bias = jnp.concatenate([jnp.concatenate([tb_ref[a, 0], tb_ref[a, 1]], axis=1) for a in range(2)], axis=0)
    sink = None
    if has_sink:
        sink = jnp.concatenate([jnp.full((BLK, 1), sk_ref[2 * j + a], F32) for a in range(2)], axis=0)

    def take(ref, start):
        return ref[pl.ds(start, BLK, stride=dil), :] if dil > 1 else ref[start:start + BLK, :]

    def put(ref, start, val):
        if dil > 1:
            ref[pl.ds(start, BLK, stride=dil), :] = val
        else:
            ref[start:start + BLK, :] = val

    for sb in range(rows // span):
        for r in range(dil):
            base = sb * span + r
            qs = _mx(_pair_rows(take(q_ref, base).astype(F32), low))
            if sb == 0:
                k_prev, v_prev, mask = take(kh_ref, r), take(vh_ref, r), first
            else:
                k_prev, v_prev, mask = take(k_ref, base - span), take(v_ref, base - span), band
            k_win = _mx(jnp.concatenate([k_prev, take(k_ref, base)], axis=0))
            v_win = _mx(jnp.concatenate([v_prev, take(v_ref, base)], axis=0))
            s = _dot_nt(qs, k_win) + jnp.concatenate([mask, mask], axis=0)
            if has_bias:
                s = s + bias
            m = jnp.max(s, axis=-1, keepdims=True)
            if has_sink:
                m = jnp.maximum(m, sink)
            e = jnp.exp(s - m)
            den = jnp.sum(e, axis=-1, keepdims=True)
            if has_sink:
                den = den + jnp.exp(sink - m)
            o = _dot(_mx(e), v_win) / den
            put(o_ref, base, jnp.where(low, o[0:BLK], o[BLK:]))
            if want_lse:
                lse = jnp.broadcast_to(m + jnp.log(den), (2 * BLK, LANES))
                put(lse_ref, base, jnp.where(low, lse[0:BLK], lse[BLK:]))


def _band(q_arr, k_arr, v_arr, *, q_col, k_col, v_col, shared_kv, dil, max_dist, tb=None, sinks=None,
          want_lse=False):
    s = q_arr.shape[0]
    span = BLK * dil
    rows = max(min(2048, s), span)
    per = rows // span
    npair = A_HEADS * HD // LANES
    kv = (lambda j: 0) if shared_kv else (lambda j: j)
    in_specs = [pl.BlockSpec((rows, LANES), lambda j, i: (i, q_col + j)),
                pl.BlockSpec((rows, LANES), lambda j, i: (i, k_col + kv(j))),
                pl.BlockSpec((span, LANES), lambda j, i: (jnp.maximum(i * per - 1, 0), k_col + kv(j))),
                pl.BlockSpec((rows, LANES), lambda j, i: (i, v_col + kv(j))),
                pl.BlockSpec((span, LANES), lambda j, i: (jnp.maximum(i * per - 1, 0), v_col + kv(j)))]
    args = [q_arr, k_arr, k_arr, v_arr, v_arr]
    if tb is not None:
        in_specs.append(pl.BlockSpec((2, 2, BLK, LANES), lambda j, i: (j, 0, 0, 0)))
        args.append(tb)
    if sinks is not None:
        in_specs.append(pl.BlockSpec(memory_space=pltpu.SMEM))
        args.append(sinks)
    o_spec = pl.BlockSpec((rows, LANES), lambda j, i: (i, j))
    o_shape = jax.ShapeDtypeStruct((s, npair * LANES), F32)
    return pl.pallas_call(
        functools.partial(_band_kernel, dil=dil, rows=rows, max_dist=max_dist, has_bias=tb is not None,
                          has_sink=sinks is not None, want_lse=want_lse),
        grid=(npair, s // rows),
        in_specs=in_specs,
        out_specs=[o_spec, o_spec] if want_lse else o_spec,
        out_shape=[o_shape, o_shape] if want_lse else o_shape,
        compiler_params=_cparams(("arbitrary", "arbitrary")),
        name="band_attention",
    )(*args)


def _memkv_kernel(mem_ref, nw_ref, w_ref, o_ref):
    x = mem_ref[...]
    y = x * lax.rsqrt(jnp.mean(x * x, axis=-1, keepdims=True) + EPS) * nw_ref[...]
    o_ref[...] = _dot(_mx(y), w_ref[...])


def _memkv(mem2, norm_w, w):
    return pl.pallas_call(
        _memkv_kernel,
        out_shape=jax.ShapeDtypeStruct((mem2.shape[0], w.shape[1]), F32),
        compiler_params=pltpu.CompilerParams(vmem_limit_bytes=VMEM_LIMIT),
        name="mem_kv",
    )(mem2, norm_w, w)


def _memattn_kernel(q_ref, k_ref, v_ref, o_ref):
    tq = q_ref.shape[0]
    low = lax.broadcasted_iota(jnp.int32, (tq, LANES), 1) < HD
    for pr in range(M_HEADS // 2):
        cols = slice(pr * LANES, (pr + 1) * LANES)
        s = _dot_nt(_pair_rows(q_ref[:, cols], low), _mx(k_ref[:, cols]))
        e = jnp.exp(s - jnp.max(s, axis=-1, keepdims=True))
        o = _dot(_mx(e), _mx(v_ref[:, cols])) / jnp.sum(e, axis=-1, keepdims=True)
        o_ref[:, cols] = jnp.where(low, o[0:tq], o[tq:])


def _memattn(proj_b, kv):
    s = proj_b.shape[0]
    tq = min(256, s)
    ml = kv.shape[0]
    wm = M_HEADS * HD
    return pl.pallas_call(
        _memattn_kernel,
        grid=(s // tq,),
        in_specs=[pl.BlockSpec((tq, wm), lambda i: (i, _BCOLS["m_q"][0] // wm)),
                  pl.BlockSpec((ml, wm), lambda i: (0, 0)),
                  pl.BlockSpec((ml, wm), lambda i: (0, 1))],
        out_specs=pl.BlockSpec((tq, wm), lambda i: (i, 0)),
        out_shape=jax.ShapeDtypeStruct((s, wm), F32),
        compiler_params=_cparams(("arbitrary",)),
        name="mem_attention",
    )(proj_b, kv, kv)


def _outproj_kernel(x_ref, a_ref, az_ref, b_ref, c_ref, cz_ref, d0_ref, d1_ref, d2_ref, l0_ref, l1_ref, l2_ref,
                    dz_ref, m_ref, mz_ref, w_ref, nw_ref, o_ref):
    l0, l1, l2 = l0_ref[...], l1_ref[...], l2_ref[...]
    mx = jnp.maximum(jnp.maximum(l0, l1), l2)
    e0, e1, e2 = jnp.exp(l0 - mx), jnp.exp(l1 - mx), jnp.exp(l2 - mx)
    inv = 1.0 / (e0 + e1 + e2)
    d = (e0 * inv) * d0_ref[...] + (e1 * inv) * d1_ref[...] + (e2 * inv) * d2_ref[...]
    pieces = (a_ref[...] * _silu(az_ref[...]), b_ref[...], c_ref[...] * _silu(cz_ref[...]),
              d * _silu(dz_ref[...]), m_ref[...] * _silu(mz_ref[...]))
    y = None
    row = 0
    for piece in pieces:
        w = piece.shape[1]
        part = _dot(_mx(piece), w_ref[row:row + w, :])
        y = part if y is None else y + part
        row += w
    y = y * lax.rsqrt(jnp.mean(y * y, axis=-1, keepdims=True) + EPS) * nw_ref[...]
    o_ref[...] = x_ref[...] + y


def _outproj(x2, proj, a_out, b_out, c_out, d_outs, d_lses, m_out, w, norm_w):
    s = x2.shape[0]
    tm = min(512, s)
    wide = lambda width, col: pl.BlockSpec((tm, width), lambda i: (i, col))
    w512 = A_HEADS * HD
    wm = M_HEADS * HD
    return pl.pallas_call(
        _outproj_kernel,
        grid=(s // tm,),
        in_specs=[wide(D_MODEL, 0), wide(w512, 0), wide(w512, _FCOLS["a_z"][0] // w512), wide(w512, 0),
                  wide(w512, 0), wide(w512, _FCOLS["c_z"][0] // w512),
                  wide(w512, 0), wide(w512, 0), wide(w512, 0), wide(w512, 0), wide(w512, 0), wide(w512, 0),
                  wide(w512, _FCOLS["d_z"][0] // w512), wide(wm, 0), wide(wm, _FCOLS["m_z"][0] // wm),
                  pl.BlockSpec((MIX_WIDTH, D_MODEL), lambda i: (0, 0)),
                  pl.BlockSpec((1, D_MODEL), lambda i: (0, 0))],
        out_specs=wide(D_MODEL, 0),
        out_shape=jax.ShapeDtypeStruct((s, D_MODEL), F32),
        compiler_params=_cparams(("arbitrary",)),
        name="outproj",
    )(x2, a_out, proj, b_out, c_out, proj, *d_outs, *d_lses, proj, m_out, proj, w, norm_w)


def _c_heads(t, axis):
    shape = t.shape
    t = t.reshape(shape[:axis] + (C_HEADS, HD) + shape[axis + 1:])
    return jnp.take(t, jnp.array(_C_ORDER), axis=axis).reshape(shape)


def _repack_w_in(w, cols):
    out = []
    for name, (_, width) in cols.items():
        src, true_w, is_query = _SRC[name]
        piece = w[:, src:src + true_w]
        if is_query:
            piece = piece * HD ** -0.5
        if name in ("c_q", "c_z"):
            piece = _c_heads(piece, 1)
        if true_w < width:
            piece = jnp.pad(piece, ((0, 0), (0, width - true_w)))
        out.append(piece)
    return _mx(jnp.concatenate(out, axis=1))


def _heads_first_t(t, nh):
    s = t.shape[0]
    return jnp.transpose(t.reshape(s, nh, HD), (1, 2, 0))


def kernel(x, mem, positions, pre_norm, post_norm, w_in, w_out, rel_bias, a_cmp_pos, a_cmp_w1, a_cmp_w2,
           b_conv_w, b_conv_b, b_dt_bias, b_a_log, b_d, b_norm, c_sinks, m_norm, m_w_kv):
    b, s, _ = x.shape
    assert b == 1 and s % 2048 == 0, "sequence length must be a multiple of 2048 and batch 1"
    depth = w_in.shape[0]
    nb = s // BLK
    ncmp = s // CMP_STRIDE

    rel_a = rel_bias[:, :A_HEADS]
    rel_d = rel_bias[:, A_HEADS:]
    no_limit = 1 << 30
    c0_a = np.array([-2 * BLK] + [o * BLK for o in range(N_OFF)] + [A_WINDOW], np.int32)
    hi_a = np.array([no_limit] * (N_TAB - 1) + [A_WINDOW], np.int32)
    tb_a = _bias_table(rel_a, jnp.asarray(c0_a), a_row=-1, b_lane=1, scale=1, rows=BLK,
                       hi=jnp.asarray(hi_a))
    bc_a = _bias_table(rel_a, jnp.array([CMP_STRIDE * CMP_NEAR - (CMP_LEN - 1)], jnp.int32),
                       a_row=-CMP_STRIDE, b_lane=1, scale=1, rows=CMP_TAB)
    tb_d = [_bias_table(rel_d[:, p * D_SLOTS:(p + 1) * D_SLOTS], jnp.array([BLK, 0], jnp.int32), a_row=1,
                        b_lane=-1, scale=dil, rows=BLK) for p, (_, dil) in enumerate(D_PATTERNS)]
    key_pos = np.arange(s)
    key_aug_np = np.zeros((s, AUG_K - AUG_BASE), np.float32)
    key_aug_np[:, 0:2] = 1.0
    key_aug_np[key_pos, AUG_SEL - AUG_BASE + (key_pos // SEL_BLK) % (SEL_TILE // SEL_BLK)] = 1.0
    key_aug = jnp.asarray(key_aug_np, MXU_DTYPE)
    val_aug_np = np.zeros((V_ROWS - HD, s), np.float32)
    val_aug_np[0] = 1.0
    val_aug = jnp.asarray(val_aug_np, MXU_DTYPE)[None]
    head_slot = jnp.eye(A_KV, dtype=F32)
    half = ROPE_THETA ** (-jnp.arange(HD // 2, dtype=F32) / (HD // 2))
    inv_row = jnp.tile(half, LANES // (HD // 2)).reshape(1, LANES)
    pos_col = positions.reshape(s, 1)
    mem2 = mem.reshape(mem.shape[1], D_MODEL)

    x2 = x.reshape(s, D_MODEL)
    for layer in range(depth):
        norm_w = pre_norm[layer].reshape(1, D_MODEL)
        proj = _inproj(x2, norm_w, _repack_w_in(w_in[layer], _FCOLS), F32)
        proj_b = _inproj(x2, norm_w, _repack_w_in(w_in[layer], _BCOLS), MXU_DTYPE)
        fcol = lambda name: proj[:, _FCOLS[name][0]:_FCOLS[name][0] + _FCOLS[name][1]]
        bcol = lambda name: proj_b[:, _BCOLS[name][0]:_BCOLS[name][0] + _BCOLS[name][1]]

        wide = CMP_STRIDE * A_KV * HD
        u = jnp.stack([fcol("a_kc").reshape(ncmp, wide), fcol("a_vc").reshape(ncmp, wide)])
        pe = (a_cmp_pos[layer].reshape(2, 1, 2, CMP_STRIDE, 1, HD)
              * head_slot.reshape(1, A_KV, 1, 1, A_KV, 1)).reshape(2 * A_KV, 2, wide)
        w1 = _mx(a_cmp_w1[layer].reshape(2, 1, 2, CMP_STRIDE, 1, HD, CMP_HIDDEN)
                 * head_slot.reshape(1, A_KV, 1, 1, A_KV, 1, 1)).reshape(2 * A_KV, 2, wide, CMP_HIDDEN)
        cmp_out = _compress(u, pe, w1, _mx(a_cmp_w2[layer]), ncmp)
        kc_own = (cmp_out[:A_KV, :, None, :] * head_slot[:, None, :, None]).reshape(A_KV, ncmp, AUG_BASE)
        kc = _mx(jnp.concatenate([kc_own, jnp.ones((A_KV, ncmp, 2), F32),
                                  jnp.zeros((A_KV, ncmp, AUG_K - AUG_BASE - 2), F32)], axis=2))
        vct = _mx(jnp.transpose(cmp_out[A_KV:], (0, 2, 1)))
        gates = jnp.transpose(fcol("a_gate")[:, :3 * A_HEADS].reshape(nb, BLK, A_KV, A_G, 3),
                              (2, 0, 4, 3, 1)).reshape(A_KV, nb, 3, A_G * BLK)
        vst_aug = jnp.concatenate([_heads_first_t(bcol("a_vs"), A_KV),
                                   jnp.broadcast_to(val_aug, (A_KV, V_ROWS - HD, s))], axis=1)
        a_out = _nsa(proj_b, kc, vct, key_aug, vst_aug, _heads_first_t(bcol("a_vw"), A_KV), gates, tb_a, bc_a, s)

        pad_row = lambda v: jnp.pad(v, (0, LANES - v.shape[0])).reshape(1, LANES)
        b_out = _ssd(proj, b_conv_w[layer], b_conv_b[layer].reshape(1, B_CONV_DIM), pad_row(b_dt_bias[layer]),
                     pad_row(b_a_log[layer]), jnp.repeat(b_d[layer], B_HEADDIM).reshape(1, B_INNER),
                     b_norm[layer].reshape(1, B_INNER))

        cq_rot, ck_rot = _rope(proj, pos_col, inv_row)
        c_out = _band(cq_rot, ck_rot, proj_b, q_col=0, k_col=0, v_col=_BCOLS["c_v"][0] // LANES, shared_kv=True,
                      dil=1, max_dist=C_WINDOW - 1, sinks=c_sinks[layer][jnp.array(_C_ORDER)])

        d_outs, d_lses = [], []
        for p, (window, dil) in enumerate(D_PATTERNS):
            o_p, lse_p = _band(proj, proj, proj, q_col=_FCOLS["d_q"][0] // LANES + p * D_SLOTS * HD // LANES,
                               k_col=_FCOLS["d_k"][0] // LANES, v_col=_FCOLS["d_v"][0] // LANES, shared_kv=False,
                               dil=dil, max_dist=window // dil, tb=tb_d[p], want_lse=True)
            d_outs.append(o_p)
            d_lses.append(lse_p)

        kv = _memkv(mem2, m_norm[layer].reshape(1, D_MODEL), _mx(m_w_kv[layer]))
        m_out = _memattn(proj_b, kv)

        c_rows = slice(A_HEADS * HD + B_INNER, A_HEADS * HD + B_INNER + C_HEADS * HD)
        w_o = w_out[layer].at[c_rows].set(_c_heads(w_out[layer][c_rows], 0))
        x2 = _outproj(x2, proj, a_out, b_out, c_out, d_outs, d_lses, m_out, _mx(w_o),
                      post_norm[layer].reshape(1, D_MODEL))
    return x2.reshape(b, s, D_MODEL)
```

```python
import functools
import math

import numpy as np
import jax
import jax.numpy as jnp
from jax import lax
from jax.experimental import pallas as pl
from jax.experimental.pallas import tpu as pltpu

F32 = jnp.float32
MXU_DTYPE = jnp.bfloat16

D_MODEL = 1024
HD = 64
BLK = 128
NEG_INF = -1e30
TINY = 1e-30
EPS = 1e-6

A_HEADS, A_KV = 8, 2
A_G = A_HEADS // A_KV
CMP_LEN, CMP_STRIDE, CMP_HIDDEN = 32, 16, 128
SEL_BLK, SEL_TOPK, A_WINDOW = 64, 16, 512
B_HEADS, B_HEADDIM, B_GROUPS, B_STATE, B_CONV, B_CHUNK = 8, 64, 2, 128, 4, 128
B_INNER = B_HEADS * B_HEADDIM
B_CONV_DIM = B_INNER + 2 * B_GROUPS * B_STATE
C_HEADS, C_KV, C_WINDOW = 8, 2, 128
ROPE_THETA = 150000.0
D_SLOTS = 8
D_PATTERNS = ((128, 1), (512, 4), (2048, 16))
D_NPAT = 3
M_HEADS = 4
N_BUCKETS, MAX_DIST = 32, 2048
MIX_WIDTH = A_HEADS * HD + B_INNER + C_HEADS * HD + D_SLOTS * HD + M_HEADS * HD

LANES = 128
VMEM_LIMIT = 56 * 1024 * 1024

_SRC = {"a_q": (0, 512, True), "a_kc": (512, 128, False), "a_vc": (640, 128, False), "a_ks": (768, 128, False),
        "a_vs": (896, 128, False), "a_kw": (1024, 128, False), "a_vw": (1152, 128, False),
        "a_gate": (1280, 24, False), "a_z": (1304, 512, False), "b_xbc": (1816, 1024, False),
        "b_dt": (2840, 8, False), "b_z": (2848, 512, False), "c_q": (3360, 512, True), "c_k": (3872, 128, False),
        "c_v": (4000, 128, False), "c_z": (4128, 512, False), "d_q": (4640, 1536, True), "d_k": (6176, 512, False),
        "d_v": (6688, 512, False), "d_z": (7200, 512, False), "m_q": (7712, 256, True), "m_z": (7968, 256, False)}


def _layout(pieces):
    cols, off = {}, 0
    for name, width in pieces:
        cols[name] = (off, width)
        off += width
    return cols, off


_FCOLS, N_F32 = _layout((("b_xbc", 1024), ("a_z", 512), ("b_z", 512), ("c_z", 512), ("d_z", 512), ("c_q", 512),
                         ("d_q", 1536), ("d_k", 512), ("d_v", 512), ("m_z", 256), ("a_kc", 128), ("a_vc", 128),
                         ("c_k", 128), ("a_gate", 128), ("b_dt", 128)))
_BCOLS, N_BF16 = _layout((("a_q", 512), ("m_q", 256), ("a_ks", 128), ("a_vs", 128), ("a_kw", 128),
                          ("a_vw", 128), ("c_v", 128)))
PROJ_TN = 1408
_C_ORDER = tuple(h for j in range(C_HEADS // C_KV) for h in (j, j + C_HEADS // C_KV))


def _t5_thresholds():
    d = np.arange(0, 4 * MAX_DIST)
    exact = N_BUCKETS // 2
    rel = np.maximum(d, exact).astype(np.float64)
    large = exact + (np.log(rel / exact) / math.log(MAX_DIST / exact) * (N_BUCKETS - exact)).astype(np.int64)
    bucket = np.where(d < exact, d, np.minimum(large, N_BUCKETS - 1))
    return tuple(int(np.argmax(bucket >= b)) for b in range(1, N_BUCKETS))


_THR = _t5_thresholds()
FAR_DIST = _THR[-1]
N_OFF = -(-(FAR_DIST + BLK) // BLK) + 1
TAB_OFF0 = 1
TAB_EDGE = TAB_OFF0 + N_OFF
N_TAB = TAB_EDGE + 1
FAR_WIDTH = 4
CMP_REACH = -(-FAR_DIST // CMP_STRIDE) // 8 * 8 + 8
CMP_WIN = 128
CMP_NEAR = 128
CMP_TAB = CMP_NEAR + CMP_WIN
SEL_TILE = 512
AUG_BASE = LANES
AUG_SEL = AUG_BASE + 8
AUG_END = AUG_SEL + SEL_TILE // SEL_BLK
AUG_K = 2 * LANES
V_ROWS = HD + 16


def _cparams(sem):
    return pltpu.CompilerParams(dimension_semantics=sem, vmem_limit_bytes=VMEM_LIMIT)


def _mx(x):
    return x.astype(MXU_DTYPE)


def _dot(a, b):
    return jnp.dot(a, b, preferred_element_type=F32)


def _dot_nt(a, b):
    return lax.dot_general(a, b, (((1,), (1,)), ((), ())), preferred_element_type=F32)


def _split3(a):
    hi = _mx(a)
    r1 = a - hi.astype(F32)
    mid = _mx(r1)
    lo = _mx(r1 - mid.astype(F32))
    return hi, mid, lo


def _dot_f32_rhs_exact(a, b):
    hi, mid, lo = _split3(a)
    return _dot(hi, b) + _dot(mid, b) + _dot(lo, b)


def _dot_f32_lhs_exact(a, b):
    hi, mid, lo = _split3(b)
    return _dot(a, hi) + _dot(a, mid) + _dot(a, lo)


def _silu(x):
    return x * jax.nn.sigmoid(x)


def _bias_table_kernel(rel_ref, c0_ref, hi_ref, o_ref, *, a_row, b_lane, scale, rows, masked):
    h = pl.program_id(0)
    t = pl.program_id(1)
    r = lax.broadcasted_iota(jnp.int32, (rows, LANES), 0)
    l = lax.broadcasted_iota(jnp.int32, (rows, LANES), 1)
    dist = (c0_ref[t] + a_row * r + b_lane * l) * scale
    out = jnp.full((rows, LANES), rel_ref[0, h], F32)
    for b in range(1, N_BUCKETS):
        out = jnp.where(dist >= _THR[b - 1], rel_ref[b, h], out)
    if masked:
        out = jnp.where(dist < 0, NEG_INF, jnp.where(dist >= hi_ref[t], NEG_INF, out))
    o_ref[0, 0] = out


def _bias_table(rel_cols, c0, *, a_row, b_lane, scale, rows, hi=None):
    nh = rel_cols.shape[1]
    nt = c0.shape[0]
    return pl.pallas_call(
        functools.partial(_bias_table_kernel, a_row=a_row, b_lane=b_lane, scale=scale, rows=rows,
                          masked=hi is not None),
        grid_spec=pltpu.PrefetchScalarGridSpec(
            num_scalar_prefetch=3, grid=(nh, nt), in_specs=[],
            out_specs=pl.BlockSpec((1, 1, rows, LANES), lambda h, t, *_: (h, t, 0, 0))),
        out_shape=jax.ShapeDtypeStruct((nh, nt, rows, LANES), F32),
        compiler_params=_cparams(("arbitrary", "arbitrary")),
        name="bias_table",
    )(rel_cols, c0, c0 if hi is None else hi)


def _inproj_kernel(x_ref, nw_ref, w_ref, o_ref, h_scr):
    @pl.when(pl.program_id(1) == 0)
    def _():
        x = x_ref[...]
        y = x * lax.rsqrt(jnp.mean(x * x, axis=-1, keepdims=True) + EPS)
        h_scr[...] = _mx(y * nw_ref[...])

    o_ref[...] = _dot(h_scr[...], w_ref[...]).astype(o_ref.dtype)


def _inproj(x2, norm_w, w, out_dtype):
    s = x2.shape[0]
    tm, tn = min(1024, s), PROJ_TN
    return pl.pallas_call(
        _inproj_kernel,
        grid=(s // tm, w.shape[1] // tn),
        in_specs=[pl.BlockSpec((tm, D_MODEL), lambda i, j: (i, 0)),
                  pl.BlockSpec((1, D_MODEL), lambda i, j: (0, 0)),
                  pl.BlockSpec((D_MODEL, tn), lambda i, j: (0, j))],
        out_specs=pl.BlockSpec((tm, tn), lambda i, j: (i, j)),
        out_shape=jax.ShapeDtypeStruct((s, w.shape[1]), out_dtype),
        scratch_shapes=[pltpu.VMEM((tm, D_MODEL), MXU_DTYPE)],
        compiler_params=_cparams(("arbitrary", "arbitrary")),
        name="inproj",
    )(x2, norm_w, w)


def _compress_kernel(u_ref, pe_ref, w1_ref, w2_ref, o_ref, h2_scr, *, ncmp):
    u = u_ref[0]
    pe_first, pe_second = pe_ref[0, 0:1, :], pe_ref[0, 1:2, :]
    h1 = _dot(_mx(u + pe_first), w1_ref[0, 0])
    h2_scr[0:ncmp, :] = _dot(_mx(u + pe_second), w1_ref[0, 1])
    h2_scr[ncmp:ncmp + 8, :] = _dot(_mx(jnp.broadcast_to(pe_second, (8, pe_second.shape[1]))), w1_ref[0, 1])
    pre = h1 + h2_scr[pl.ds(1, ncmp), :]
    o_ref[0] = _dot(_mx(jax.nn.gelu(pre)), w2_ref[0])


def _compress(u, pe, w1, w2, ncmp):
    wide = CMP_STRIDE * A_KV * HD
    return pl.pallas_call(
        functools.partial(_compress_kernel, ncmp=ncmp),
        grid=(2 * A_KV,),
        in_specs=[pl.BlockSpec((1, ncmp, wide), lambda i: (i // A_KV, 0, 0)),
                  pl.BlockSpec((1, 2, wide), lambda i: (i, 0, 0)),
                  pl.BlockSpec((1, 2, wide, CMP_HIDDEN), lambda i: (i, 0, 0, 0)),
                  pl.BlockSpec((1, CMP_HIDDEN, HD), lambda i: (i // A_KV, 0, 0))],
        out_specs=pl.BlockSpec((1, ncmp, HD), lambda i: (i, 0, 0)),
        out_shape=jax.ShapeDtypeStruct((2 * A_KV, ncmp, HD), F32),
        scratch_shapes=[pltpu.VMEM((ncmp + 8, CMP_HIDDEN), F32)],
        compiler_params=_cparams(("arbitrary",)),
        name="nsa_compress",
    )(u, pe, w1, w2)


def _nsa_kernel(q_ref, kc_ref, vct_ref, ks_ref, ka_ref, vst_ref, kw_ref, vwt_ref, g_ref, tb_ref, bc_ref,
                o_ref, s_scr, sel_scr, qa_scr, pcs_scr, *score_bufs, seq, tk):
    sa_scr = score_bufs[0]
    hk = pl.program_id(0)
    n = pl.program_id(1)
    ncmp = seq // CMP_STRIDE
    nsel = seq // SEL_BLK
    n_top = min(SEL_TOPK, nsel)
    r_all = A_G * BLK
    q_rows = q_ref[...].astype(F32)
    halves = [q_rows[:, p * LANES:(p + 1) * LANES].T for p in range(A_G // 2)]
    q_t = _mx(jnp.concatenate([h[r * HD:(r + 1) * HD] for h in halves for r in range(2)], axis=1))
    far_row = jnp.concatenate([tb_ref[g, TAB_OFF0 + N_OFF - 1, 0:1, :] for g in range(A_G)], axis=1)

    def tile_bias(o, edge=None):
        idx = jnp.clip(o, -1, N_OFF - 1) + TAB_OFF0
        if edge is not None:
            idx = jnp.where(o == edge, TAB_EDGE, idx)
        return jnp.concatenate([tb_ref[g, idx] for g in range(A_G)], axis=1)

    far_hi = _mx(far_row).astype(F32)
    bias_rows = jnp.concatenate([far_hi, far_row - far_hi, jnp.zeros((AUG_SEL - AUG_BASE - 2, r_all), F32)], axis=0)
    qa_scr[0:AUG_BASE, :] = jnp.zeros((AUG_BASE, r_all), MXU_DTYPE)
    qa_scr[pl.ds(pl.multiple_of(hk * HD, HD), HD), :] = q_t
    qa_scr[AUG_BASE:AUG_END, :] = _mx(
        jnp.concatenate([bias_rows, jnp.zeros((AUG_END - AUG_SEL, r_all), F32)], axis=0))
    qa_scr[AUG_END:, :] = jnp.zeros((AUG_K - AUG_END, r_all), MXU_DTYPE)
    q_own = qa_scr[0:AUG_BASE, :]

    tok = BLK // CMP_STRIDE
    s_scr[...] = _dot(kc_ref[0], qa_scr[...])
    st = jnp.clip((tok * n - CMP_REACH) // 16 * 16, 0, ncmp - CMP_WIN)
    off = pl.multiple_of(st - tok * n + CMP_NEAR, 8)
    st = pl.multiple_of(st, 16)
    near = jnp.concatenate([bc_ref[g, 0, pl.ds(off, CMP_WIN), :] for g in range(A_G)], axis=1)
    s_scr[pl.ds(st, CMP_WIN), :] = _dot(kc_ref[0, pl.ds(st, CMP_WIN), 0:AUG_BASE], q_own) + near

    nwb = A_WINDOW // BLK + 1
    kb0 = jnp.maximum(n - (nwb - 1), 0)
    k0 = pl.multiple_of(kb0 * BLK, BLK)
    s = _dot(kw_ref[pl.ds(k0, nwb * BLK), :], q_own)
    s = s + jnp.concatenate([tile_bias(n - kb0 - i, edge=nwb - 1) for i in range(nwb)], axis=0)
    e = jnp.exp(s - jnp.max(s, axis=0, keepdims=True))
    o_w = _dot(vwt_ref[0, :, pl.ds(k0, nwb * BLK)], _mx(e)) / jnp.sum(e, axis=0, keepdims=True)

    jrow = lax.broadcasted_iota(jnp.int32, (ncmp, BLK), 0)
    qpos = n * BLK + lax.broadcasted_iota(jnp.int32, (ncmp, BLK), 1)
    visible = jnp.where(jrow * CMP_STRIDE + (CMP_LEN - 1) <= qpos, 0.0, NEG_INF)
    s_c = s_scr[...] + jnp.concatenate([visible] * A_G, axis=1)
    m = jnp.max(s_c, axis=0, keepdims=True)
    e = jnp.exp(s_c - m)
    den = jnp.sum(e, axis=0, keepdims=True)
    inv = jnp.where(m > 0.5 * NEG_INF, 1.0 / jnp.maximum(den, TINY), 0.0)
    p_c = e * inv
    o_c = _dot(vct_ref[0], _mx(p_c))
    pcs = p_c[:, 0:BLK]
    for g in range(1, A_G):
        pcs = pcs + p_c[:, g * BLK:(g + 1) * BLK]
    per = SEL_BLK // CMP_STRIDE
    pcs_scr[0:8, :] = jnp.zeros((8, BLK), F32)
    pcs_scr[8:8 + ncmp, :] = pcs
    part = [pcs_scr[pl.ds(8 + i, nsel, stride=per), :] for i in range(-1, per)]
    imp = part[1] + part[0]
    for i in range(1, per):
        imp = imp + (part[i + 1] + part[i])

    b_io = lax.broadcasted_iota(jnp.int32, (nsel, BLK), 0)
    b_f = b_io.astype(F32)
    pos = n * BLK + lax.broadcasted_iota(jnp.int32, (nsel, BLK), 1)
    forced = (b_io == pos // SEL_BLK) | (b_io == 0)
    valid = b_io * SEL_BLK <= pos
    imp = jnp.where(forced, -2.0, jnp.where(valid, imp, -1.0))

    def pick_one(_, carry):
        imp, sel = carry
        vals = [imp[8 * g:8 * (g + 1)] for g in range(nsel // 8)]
        idxs = [b_f[8 * g:8 * (g + 1)] for g in range(nsel // 8)]
        while len(vals) > 1:
            later = [vals[a + 1] > vals[a] for a in range(0, len(vals), 2)]
            idxs = [jnp.where(c, idxs[2 * a + 1], idxs[2 * a]) for a, c in enumerate(later)]
            vals = [jnp.where(c, vals[2 * a + 1], vals[2 * a]) for a, c in enumerate(later)]
        v, ix = vals[0], idxs[0]
        top = jnp.max(v, axis=0, keepdims=True)
        first = jnp.min(jnp.where(v == top, ix, float(nsel)), axis=0, keepdims=True)
        hit = b_f == first
        sel = jnp.where(hit, jnp.where(top >= 0.0, 0.0, NEG_INF), sel)
        return jnp.where(hit, -2.0, imp), sel

    _, sel = lax.fori_loop(0, n_top - 2, pick_one, (imp, jnp.where(forced, 0.0, NEG_INF)))
    sel_scr[...] = sel

    nblk_t = tk // BLK
    no_bias_rows = jnp.zeros((AUG_SEL - AUG_BASE, r_all), F32)

    def scores(t, const_rows):
        k0 = pl.multiple_of(t * tk, tk)
        picked = sel_scr[pl.ds(pl.multiple_of((AUG_END - AUG_SEL) * t, 8), AUG_END - AUG_SEL), :]
        aug = _mx(jnp.concatenate([const_rows, jnp.concatenate([picked] * A_G, axis=1)], axis=0))
        q_aug = jnp.concatenate([q_own, aug, jnp.zeros((AUG_K - AUG_END, r_all), MXU_DTYPE)], axis=0)
        keys = jnp.concatenate([ks_ref[pl.ds(k0, tk), :], ka_ref[pl.ds(k0, tk), :]], axis=1)
        return _dot(keys, q_aug)

    last = (n * BLK) // tk

    def tile_scores(t, near):
        if not near:
            return scores(t, bias_rows)
        bias = jnp.concatenate([tile_bias(n - (nblk_t * t + i)) for i in range(nblk_t)], axis=0)
        return scores(t, no_bias_rows) + bias

    def accumulate(t, s, m_i, acc, m_tile):
        k0 = pl.multiple_of(t * tk, tk)
        m_new = jnp.maximum(m_i, m_tile)
        p = jnp.exp(s - m_new)
        acc = acc * jnp.exp(m_i - m_new) + _dot(vst_ref[0, :, pl.ds(k0, tk)], _mx(p))
        return m_new, acc

    def stage1(t, buf, near):
        s = tile_scores(t, near)
        buf[...] = s
        return jnp.max(s, axis=0, keepdims=True)

    def run_pipelined(first, trips, width, near, m_i, acc):
        bufs = score_bufs[:width]

        def trip(u, carry):
            m_i, acc, m_cur = carry
            t0 = first + width * u
            for k in range(width):
                m_next = stage1(t0 + k + 1, bufs[(k + 1) % width], near)
                m_i, acc = accumulate(t0 + k, bufs[k][...], m_i, acc, m_cur)
                m_cur = m_next
            return m_i, acc, m_cur

        return lax.fori_loop(0, trips, trip, (m_i, acc, stage1(first, sa_scr, near)))

    def single(t, carry):
        s = tile_scores(t, True)
        return accumulate(t, s, *carry, jnp.max(s, axis=0, keepdims=True))

    far_tiles = jnp.maximum(n - (N_OFF - 2), 0) // nblk_t
    far_trips = jnp.maximum(far_tiles - 1, 0) // FAR_WIDTH
    m_i, acc_s, m_t = run_pipelined(0, far_trips, FAR_WIDTH, False, jnp.full((1, r_all), NEG_INF, F32),
                                    jnp.zeros((V_ROWS, r_all), F32))
    far_done = FAR_WIDTH * far_trips
    has_far = jnp.minimum(far_tiles, 1)
    m_i, acc_s = lax.fori_loop(0, has_far, lambda _, c: accumulate(far_done, sa_scr[...], *c, m_t), (m_i, acc_s))
    start = far_done + has_far
    near_pairs = (last - start) // 2
    m_i, acc_s, m_t = run_pipelined(start, near_pairs, 2, True, m_i, acc_s)
    m_i, acc_s = accumulate(start + 2 * near_pairs, sa_scr[...], m_i, acc_s, m_t)
    _, acc_s = lax.fori_loop(start + 2 * near_pairs + 1, last + 1, single, (m_i, acc_s))
    o_s = acc_s[0:HD] / jnp.maximum(acc_s[HD:HD + 1], TINY)

    gates = jax.nn.sigmoid(g_ref[0, 0])
    o_t = gates[0:1] * o_c + gates[1:2] * o_s + gates[2:3] * o_w
    for pair in range(A_G // 2):
        both = jnp.concatenate([o_t[:, (2 * pair) * BLK:(2 * pair + 1) * BLK],
                                o_t[:, (2 * pair + 1) * BLK:(2 * pair + 2) * BLK]], axis=0)
        o_ref[:, pair * LANES:(pair + 1) * LANES] = both.T


def _nsa(proj_b, kc, vct, key_aug, vst, vwt, gates, tb_a, bc, seq):
    nb = seq // BLK
    ncmp = seq // CMP_STRIDE
    nsel = seq // SEL_BLK
    tk = SEL_TILE
    full = lambda h, n: (h, 0, 0)
    wq = A_G * HD
    resident = lambda name: pl.BlockSpec((seq, LANES), lambda h, n: (0, _BCOLS[name][0] // LANES))
    return pl.pallas_call(
        functools.partial(_nsa_kernel, seq=seq, tk=tk),
        grid=(A_KV, nb),
        in_specs=[pl.BlockSpec((BLK, wq), lambda h, n: (n, _BCOLS["a_q"][0] // wq + h)),
                  pl.BlockSpec((1, ncmp, AUG_K), full),
                  pl.BlockSpec((1, HD, ncmp), full),
                  resident("a_ks"),
                  pl.BlockSpec((seq, LANES), lambda h, n: (0, 0)),
                  pl.BlockSpec((1, V_ROWS, seq), full),
                  resident("a_kw"),
                  pl.BlockSpec((1, HD, seq), full),
                  pl.BlockSpec((1, 1, 3, A_G * BLK), lambda h, n: (h, n, 0, 0)),
                  pl.BlockSpec((A_G, N_TAB, BLK, LANES), lambda h, n: (h, 0, 0, 0)),
                  pl.BlockSpec((A_G, 1, CMP_TAB, LANES), lambda h, n: (h, 0, 0, 0))],
        out_specs=pl.BlockSpec((BLK, A_G * HD), lambda h, n: (n, h)),
        out_shape=jax.ShapeDtypeStruct((seq, A_HEADS * HD), F32),
        scratch_shapes=[pltpu.VMEM((ncmp, A_G * BLK), F32), pltpu.VMEM((nsel, BLK), F32),
                        pltpu.VMEM((AUG_K, A_G * BLK), MXU_DTYPE), pltpu.VMEM((ncmp + 8, BLK), F32),
                        *[pltpu.VMEM((tk, A_G * BLK), F32) for _ in range(FAR_WIDTH)]],
        compiler_params=_cparams(("arbitrary", "arbitrary")),
        name="nsa",
    )(proj_b, kc, vct, proj_b, key_aug, vst, proj_b, vwt, gates, tb_a, bc)


def _ssd_kernel(xc_ref, xp_ref, dt_ref, z_ref, cw_ref, cb_ref, dtb_ref, al_ref, d_ref, nw_ref, o_ref,
                xpad_scr, h_scr):
    c = pl.program_id(0)
    L = B_CHUNK

    @pl.when(c == 0)
    def _():
        h_scr[...] = jnp.zeros_like(h_scr)

    xpad_scr[0:8, :] = jnp.where(c > 0, xp_ref[L - 8:L, :], 0.0)
    xpad_scr[8:8 + L, :] = xc_ref[...]
    acc = cw_ref[0:1, :] * xpad_scr[pl.ds(8 - (B_CONV - 1), L), :]
    for i in range(1, B_CONV):
        acc = acc + cw_ref[i:i + 1, :] * xpad_scr[pl.ds(8 - (B_CONV - 1) + i, L), :]
    xa = _silu(acc + cb_ref[...])
    xs = xa[:, :B_INNER]
    bm = xa[:, B_INNER:B_INNER + B_GROUPS * B_STATE]
    cm = xa[:, B_INNER + B_GROUPS * B_STATE:]

    raw = dt_ref[...] + dtb_ref[...]
    dt = jnp.maximum(raw, 0.0) + jnp.log1p(jnp.exp(-jnp.abs(raw)))
    adt = dt * (-jnp.exp(al_ref[...]))
    ri = lax.broadcasted_iota(jnp.int32, (L, L), 0)
    ci = lax.broadcasted_iota(jnp.int32, (L, L), 1)
    tri = _mx(jnp.where(ri >= ci, 1.0, 0.0))
    acs = _dot_f32_lhs_exact(tri, adt)
    acs_t = acs.T
    er = lax.broadcasted_iota(jnp.int32, (LANES, B_INNER), 0)
    ec = lax.broadcasted_iota(jnp.int32, (LANES, B_INNER), 1)
    expand = _mx(jnp.where(ec // B_HEADDIM == er, 1.0, 0.0))
    dt_e = _dot_f32_rhs_exact(dt, expand)
    ea_e = _dot_f32_rhs_exact(jnp.exp(acs), expand)
    we_e = _dot_f32_rhs_exact(jnp.exp(acs[L - 1:L, :] - acs), expand)
    xdt = xs * dt_e
    xw = xdt * we_e
    lane = lax.broadcasted_iota(jnp.int32, (L, LANES), 1)
    hpg = B_HEADS // B_GROUPS
    gw = hpg * B_HEADDIM
    y_groups = []
    for g in range(B_GROUPS):
        b_g = bm[:, g * B_STATE:(g + 1) * B_STATE]
        c_g = _mx(cm[:, g * B_STATE:(g + 1) * B_STATE])
        cb = _dot_nt(c_g, _mx(b_g))
        pairs = []
        for pr in range(hpg // 2):
            x_pair = _mx(xdt[:, g * gw + pr * LANES:g * gw + (pr + 1) * LANES])
            halves = []
            for j in range(2):
                h = g * hpg + 2 * pr + j
                diff = acs[:, h:h + 1] - acs_t[h:h + 1, :]
                decay = jnp.exp(jnp.where(ri >= ci, diff, NEG_INF))
                halves.append(_dot(_mx(cb * decay), x_pair))
            pairs.append(jnp.where(lane < B_HEADDIM, halves[0], halves[1]))
        y_diag = jnp.concatenate(pairs, axis=1)
        h_prev = h_scr[:, g * gw:(g + 1) * gw]
        y_off = _dot(c_g, _mx(h_prev)) * ea_e[:, g * gw:(g + 1) * gw]
        st = _dot(_mx(b_g.T), _mx(xw[:, g * gw:(g + 1) * gw]))
        h_scr[:, g * gw:(g + 1) * gw] = h_prev * ea_e[L - 1:L, g * gw:(g + 1) * gw] + st
        y = y_diag + y_off + xs[:, g * gw:(g + 1) * gw] * d_ref[:, g * gw:(g + 1) * gw]
        yz = y * _silu(z_ref[:, g * gw:(g + 1) * gw])
        ms = jnp.mean(yz * yz, axis=-1, keepdims=True)
        y_groups.append(yz * lax.rsqrt(ms + EPS) * nw_ref[:, g * gw:(g + 1) * gw])
    o_ref[...] = jnp.concatenate(y_groups, axis=1)


def _ssd(proj, conv_w, conv_b, dtb_row, al_row, d_row, nw_row):
    s = proj.shape[0]
    L = B_CHUNK
    cx = _FCOLS["b_xbc"][0] // B_CONV_DIM
    cdt = _FCOLS["b_dt"][0] // LANES
    cz = _FCOLS["b_z"][0] // B_INNER
    row = lambda w: pl.BlockSpec((1, w), lambda c: (0, 0))
    return pl.pallas_call(
        _ssd_kernel,
        grid=(s // L,),
        in_specs=[pl.BlockSpec((L, B_CONV_DIM), lambda c: (c, cx)),
                  pl.BlockSpec((L, B_CONV_DIM), lambda c: (jnp.maximum(c - 1, 0), cx)),
                  pl.BlockSpec((L, LANES), lambda c: (c, cdt)),
                  pl.BlockSpec((L, B_INNER), lambda c: (c, cz)),
                  pl.BlockSpec((B_CONV, B_CONV_DIM), lambda c: (0, 0)),
                  row(B_CONV_DIM), row(LANES), row(LANES), row(B_INNER), row(B_INNER)],
        out_specs=pl.BlockSpec((L, B_INNER), lambda c: (c, 0)),
        out_shape=jax.ShapeDtypeStruct((s, B_INNER), F32),
        scratch_shapes=[pltpu.VMEM((L + 8, B_CONV_DIM), F32), pltpu.VMEM((B_STATE, B_INNER), F32)],
        compiler_params=_cparams(("arbitrary",)),
        name="ssd",
    )(proj, proj, proj, proj, conv_w, conv_b, dtb_row, al_row, d_row, nw_row)


def _rope_kernel(q_ref, k_ref, pos_ref, inv_ref, qo_ref, ko_ref):
    ang = pos_ref[...].astype(F32) * inv_ref[...]
    cos, sin = jnp.cos(ang), jnp.sin(ang)
    lane = lax.broadcasted_iota(jnp.int32, ang.shape, 1)
    low = (lane & (HD - 1)) < HD // 2

    def rot(t):
        partner = jnp.where(low, -pltpu.roll(t, LANES - HD // 2, 1), pltpu.roll(t, HD // 2, 1))
        return t * cos + partner * sin

    for j in range(q_ref.shape[1] // LANES):
        qo_ref[:, j * LANES:(j + 1) * LANES] = _mx(rot(q_ref[:, j * LANES:(j + 1) * LANES]))
    ko_ref[...] = _mx(rot(k_ref[...]))


def _rope(proj, pos_col, inv_row):
    s = proj.shape[0]
    tm = min(512, s)
    wq = C_HEADS * HD
    return pl.pallas_call(
        _rope_kernel,
        grid=(s // tm,),
        in_specs=[pl.BlockSpec((tm, wq), lambda i: (i, _FCOLS["c_q"][0] // wq)),
                  pl.BlockSpec((tm, LANES), lambda i: (i, _FCOLS["c_k"][0] // LANES)),
                  pl.BlockSpec((tm, 1), lambda i: (i, 0)),
                  pl.BlockSpec((1, LANES), lambda i: (0, 0))],
        out_specs=[pl.BlockSpec((tm, wq), lambda i: (i, 0)), pl.BlockSpec((tm, LANES), lambda i: (i, 0))],
        out_shape=[jax.ShapeDtypeStruct((s, wq), MXU_DTYPE), jax.ShapeDtypeStruct((s, LANES), MXU_DTYPE)],
        compiler_params=_cparams(("arbitrary",)),
        name="rope",
    )(proj, proj, pos_col, inv_row)


def _pair_rows(q2, low):
    zero = jnp.zeros_like(q2)
    return jnp.concatenate([jnp.where(low, q2, zero), jnp.where(low, zero, q2)], axis=0)


def _band_kernel(*refs, dil, rows, max_dist, has_bias, has_sink, want_lse):
    refs = list(refs)
    q_ref, k_ref, kh_ref, v_ref, vh_ref = refs[:5]
    pos = 5
    tb_ref = sk_ref = None
    if has_bias:
        tb_ref = refs[pos]
        pos += 1
    if has_sink:
        sk_ref = refs[pos]
        pos += 1
    o_ref = refs[pos]
    lse_ref = refs[pos + 1] if want_lse else None
    j = pl.program_id(0)
    i = pl.program_id(1)
    span = BLK * dil
    qi = lax.broadcasted_iota(jnp.int32, (BLK, 2 * BLK), 0)
    kj = lax.broadcasted_iota(jnp.int32, (BLK, 2 * BLK), 1)
    dist = BLK + qi - kj
    band = jnp.where((dist >= 0) & (dist <= max_dist), 0.0, NEG_INF)
    first = jnp.where(i > 0, band, jnp.where(kj >= BLK, band, NEG_INF))
    low = lax.broadcasted_iota(jnp.int32, (BLK, LANES), 1) < HD
    bias = None
    if has_bias:
        bias = jnp.concatenate([jnp.concatenate([tb_ref[a, 0], tb_ref[a, 1]], axis=1) for a in range(2)], axis=0)
    sink = None
    if has_sink:
        sink = jnp.concatenate([jnp.full((BLK, 1), sk_ref[2 * j + a], F32) for a in range(2)], axis=0)

    def take(ref, start):
        return ref[pl.ds(start, BLK, stride=dil), :] if dil > 1 else ref[start:start + BLK, :]

    def put(ref, start, val):
        if dil > 1:
            ref[pl.ds(start, BLK, stride=dil), :] = val
        else:
            ref[start:start + BLK, :] = val

    for sb in range(rows // span):
        for r in range(dil):
            base = sb * span + r
            qs = _mx(_pair_rows(take(q_ref, base).astype(F32), low))
            if sb == 0:
                k_prev, v_prev, mask = take(kh_ref, r), take(vh_ref, r), first
            else:
                k_prev, v_prev, mask = take(k_ref, base - span), take(v_ref, base - span), band
            k_win = _mx(jnp.concatenate([k_prev, take(k_ref, base)], axis=0))
            v_win = _mx(jnp.concatenate([v_prev, take(v_ref, base)], axis=0))
            s = _dot_nt(qs, k_win) + jnp.concatenate([mask, mask], axis=0)
            if has_bias:
                s = s + bias
            m = jnp.max(s, axis=-1, keepdims=True)
            if has_sink:
                m = jnp.maximum(m, sink)
            e = jnp.exp(s - m)
            ones = jnp.ones((2 * BLK, LANES), MXU_DTYPE)
            o_den = _dot(_mx(e), jnp.concatenate([v_win, ones], axis=1))
            den = o_den[:, LANES:]
            if has_sink:
                den = den + jnp.exp(sink - m)
            o = o_den[:, :LANES] / den
            put(o_ref, base, jnp.where(low, o[0:BLK], o[BLK:]))
            if want_lse:
                lse = m + jnp.log(den)
                put(lse_ref, base, jnp.where(low, lse[0:BLK], lse[BLK:]))


def _band(q_arr, k_arr, v_arr, *, q_col, k_col, v_col, shared_kv, dil, max_dist, tb=None, sinks=None,
          want_lse=False):
    s = q_arr.shape[0]
    span = BLK * dil
    rows = max(min(2048, s), span)
    per = rows // span
    npair = A_HEADS * HD // LANES
    kv = (lambda j: 0) if shared_kv else (lambda j: j)
    in_specs = [pl.BlockSpec((rows, LANES), lambda j, i: (i, q_col + j)),
                pl.BlockSpec((rows, LANES), lambda j, i: (i, k_col + kv(j))),
                pl.BlockSpec((span, LANES), lambda j, i: (jnp.maximum(i * per - 1, 0), k_col + kv(j))),
                pl.BlockSpec((rows, LANES), lambda j, i: (i, v_col + kv(j))),
                pl.BlockSpec((span, LANES), lambda j, i: (jnp.maximum(i * per - 1, 0), v_col + kv(j)))]
    args = [q_arr, k_arr, k_arr, v_arr, v_arr]
    if tb is not None:
        in_specs.append(pl.BlockSpec((2, 2, BLK, LANES), lambda j, i: (j, 0, 0, 0)))
        args.append(tb)
    if sinks is not None:
        in_specs.append(pl.BlockSpec(memory_space=pltpu.SMEM))
        args.append(sinks)
    o_spec = pl.BlockSpec((rows, LANES), lambda j, i: (i, j))
    o_shape = jax.ShapeDtypeStruct((s, npair * LANES), F32)
    return pl.pallas_call(
        functools.partial(_band_kernel, dil=dil, rows=rows, max_dist=max_dist, has_bias=tb is not None,
                          has_sink=sinks is not None, want_lse=want_lse),
        grid=(npair, s // rows),
        in_specs=in_specs,
        out_specs=[o_spec, o_spec] if want_lse else o_spec,
        out_shape=[o_shape, o_shape] if want_lse else o_shape,
        compiler_params=_cparams(("arbitrary", "arbitrary")),
        name="band_attention",
    )(*args)


def _memkv_kernel(mem_ref, nw_ref, w_ref, o_ref):
    x = mem_ref[...]
    y = x * lax.rsqrt(jnp.mean(x * x, axis=-1, keepdims=True) + EPS) * nw_ref[...]
    o_ref[...] = _dot(_mx(y), w_ref[...])


def _memkv(mem2, norm_w, w):
    return pl.pallas_call(
        _memkv_kernel,
        out_shape=jax.ShapeDtypeStruct((mem2.shape[0], w.shape[1]), F32),
        compiler_params=pltpu.CompilerParams(vmem_limit_bytes=VMEM_LIMIT),
        name="mem_kv",
    )(mem2, norm_w, w)


def _memattn_kernel(q_ref, k_ref, v_ref, o_ref):
    tq = q_ref.shape[0]
    low = lax.broadcasted_iota(jnp.int32, (tq, LANES), 1) < HD
    for pr in range(M_HEADS // 2):
        cols = slice(pr * LANES, (pr + 1) * LANES)
        s = _dot_nt(_pair_rows(q_ref[:, cols], low), _mx(k_ref[:, cols]))
        e = jnp.exp(s - jnp.max(s, axis=-1, keepdims=True))
        o = _dot(_mx(e), _mx(v_ref[:, cols])) / jnp.sum(e, axis=-1, keepdims=True)
        o_ref[:, cols] = jnp.where(low, o[0:tq], o[tq:])


def _memattn(proj_b, kv):
    s = proj_b.shape[0]
    tq = min(256, s)
    ml = kv.shape[0]
    wm = M_HEADS * HD
    return pl.pallas_call(
        _memattn_kernel,
        grid=(s // tq,),
        in_specs=[pl.BlockSpec((tq, wm), lambda i: (i, _BCOLS["m_q"][0] // wm)),
                  pl.BlockSpec((ml, wm), lambda i: (0, 0)),
                  pl.BlockSpec((ml, wm), lambda i: (0, 1))],
        out_specs=pl.BlockSpec((tq, wm), lambda i: (i, 0)),
        out_shape=jax.ShapeDtypeStruct((s, wm), F32),
        compiler_params=_cparams(("arbitrary",)),
        name="mem_attention",
    )(proj_b, kv, kv)


def _outproj_kernel(x_ref, a_ref, az_ref, b_ref, c_ref, cz_ref, d0_ref, d1_ref, d2_ref, l0_ref, l1_ref, l2_ref,
                    dz_ref, m_ref, mz_ref, w_ref, nw_ref, o_ref):
    l0, l1, l2 = l0_ref[...], l1_ref[...], l2_ref[...]
    mx = jnp.maximum(jnp.maximum(l0, l1), l2)
    e0, e1, e2 = jnp.exp(l0 - mx), jnp.exp(l1 - mx), jnp.exp(l2 - mx)
    inv = 1.0 / (e0 + e1 + e2)
    d = (e0 * inv) * d0_ref[...] + (e1 * inv) * d1_ref[...] + (e2 * inv) * d2_ref[...]
    pieces = (a_ref[...] * _silu(az_ref[...]), b_ref[...], c_ref[...] * _silu(cz_ref[...]),
              d * _silu(dz_ref[...]), m_ref[...] * _silu(mz_ref[...]))
    y = None
    row = 0
    for piece in pieces:
        w = piece.shape[1]
        part = _dot(_mx(piece), w_ref[row:row + w, :])
        y = part if y is None else y + part
        row += w
    y = y * lax.rsqrt(jnp.mean(y * y, axis=-1, keepdims=True) + EPS) * nw_ref[...]
    o_ref[...] = x_ref[...] + y


def _outproj(x2, proj, a_out, b_out, c_out, d_outs, d_lses, m_out, w, norm_w):
    s = x2.shape[0]
    tm = min(512, s)
    wide = lambda width, col: pl.BlockSpec((tm, width), lambda i: (i, col))
    w512 = A_HEADS * HD
    wm = M_HEADS * HD
    return pl.pallas_call(
        _outproj_kernel,
        grid=(s // tm,),
        in_specs=[wide(D_MODEL, 0), wide(w512, 0), wide(w512, _FCOLS["a_z"][0] // w512), wide(w512, 0),
                  wide(w512, 0), wide(w512, _FCOLS["c_z"][0] // w512),
                  wide(w512, 0), wide(w512, 0), wide(w512, 0), wide(w512, 0), wide(w512, 0), wide(w512, 0),
                  wide(w512, _FCOLS["d_z"][0] // w512), wide(wm, 0), wide(wm, _FCOLS["m_z"][0] // wm),
                  pl.BlockSpec((MIX_WIDTH, D_MODEL), lambda i: (0, 0)),
                  pl.BlockSpec((1, D_MODEL), lambda i: (0, 0))],
        out_specs=wide(D_MODEL, 0),
        out_shape=jax.ShapeDtypeStruct((s, D_MODEL), F32),
        compiler_params=_cparams(("arbitrary",)),
        name="outproj",
    )(x2, a_out, proj, b_out, c_out, proj, *d_outs, *d_lses, proj, m_out, proj, w, norm_w)


def _c_heads(t, axis):
    shape = t.shape
    t = t.reshape(shape[:axis] + (C_HEADS, HD) + shape[axis + 1:])
    return jnp.take(t, jnp.array(_C_ORDER), axis=axis).reshape(shape)


def _repack_w_in(w, cols):
    out = []
    for name, (_, width) in cols.items():
        src, true_w, is_query = _SRC[name]
        piece = w[:, src:src + true_w]
        if is_query:
            piece = piece * HD ** -0.5
        if name in ("c_q", "c_z"):
            piece = _c_heads(piece, 1)
        if true_w < width:
            piece = jnp.pad(piece, ((0, 0), (0, width - true_w)))
        out.append(piece)
    return _mx(jnp.concatenate(out, axis=1))


def _heads_first_t(t, nh):
    s = t.shape[0]
    return jnp.transpose(t.reshape(s, nh, HD), (1, 2, 0))


def kernel(x, mem, positions, pre_norm, post_norm, w_in, w_out, rel_bias, a_cmp_pos, a_cmp_w1, a_cmp_w2,
           b_conv_w, b_conv_b, b_dt_bias, b_a_log, b_d, b_norm, c_sinks, m_norm, m_w_kv):
    b, s, _ = x.shape
    assert b == 1 and s % 2048 == 0, "sequence length must be a multiple of 2048 and batch 1"
    depth = w_in.shape[0]
    nb = s // BLK
    ncmp = s // CMP_STRIDE

    rel_a = rel_bias[:, :A_HEADS]
    rel_d = rel_bias[:, A_HEADS:]
    no_limit = 1 << 30
    c0_a = np.array([-2 * BLK] + [o * BLK for o in range(N_OFF)] + [A_WINDOW], np.int32)
    hi_a = np.array([no_limit] * (N_TAB - 1) + [A_WINDOW], np.int32)
    tb_a = _bias_table(rel_a, jnp.asarray(c0_a), a_row=-1, b_lane=1, scale=1, rows=BLK,
                       hi=jnp.asarray(hi_a))
    bc_a = _bias_table(rel_a, jnp.array([CMP_STRIDE * CMP_NEAR - (CMP_LEN - 1)], jnp.int32),
                       a_row=-CMP_STRIDE, b_lane=1, scale=1, rows=CMP_TAB)
    tb_d = [_bias_table(rel_d[:, p * D_SLOTS:(p + 1) * D_SLOTS], jnp.array([BLK, 0], jnp.int32), a_row=1,
                        b_lane=-1, scale=dil, rows=BLK) for p, (_, dil) in enumerate(D_PATTERNS)]
    key_pos = np.arange(s)
    key_aug_np = np.zeros((s, AUG_K - AUG_BASE), np.float32)
    key_aug_np[:, 0:2] = 1.0
    key_aug_np[key_pos, AUG_SEL - AUG_BASE + (key_pos // SEL_BLK) % (SEL_TILE // SEL_BLK)] = 1.0
    key_aug = jnp.asarray(key_aug_np, MXU_DTYPE)
    val_aug_np = np.zeros((V_ROWS - HD, s), np.float32)
    val_aug_np[0] = 1.0
    val_aug = jnp.asarray(val_aug_np, MXU_DTYPE)[None]
    head_slot = jnp.eye(A_KV, dtype=F32)
    half = ROPE_THETA ** (-jnp.arange(HD // 2, dtype=F32) / (HD // 2))
    inv_row = jnp.tile(half, LANES // (HD // 2)).reshape(1, LANES)
    pos_col = positions.reshape(s, 1)
    mem2 = mem.reshape(mem.shape[1], D_MODEL)

    x2 = x.reshape(s, D_MODEL)
    for layer in range(depth):
        norm_w = pre_norm[layer].reshape(1, D_MODEL)
        proj = _inproj(x2, norm_w, _repack_w_in(w_in[layer], _FCOLS), F32)
        proj_b = _inproj(x2, norm_w, _repack_w_in(w_in[layer], _BCOLS), MXU_DTYPE)
        fcol = lambda name: proj[:, _FCOLS[name][0]:_FCOLS[name][0] + _FCOLS[name][1]]
        bcol = lambda name: proj_b[:, _BCOLS[name][0]:_BCOLS[name][0] + _BCOLS[name][1]]

        wide = CMP_STRIDE * A_KV * HD
        u = jnp.stack([fcol("a_kc").reshape(ncmp, wide), fcol("a_vc").reshape(ncmp, wide)])
        pe = (a_cmp_pos[layer].reshape(2, 1, 2, CMP_STRIDE, 1, HD)
              * head_slot.reshape(1, A_KV, 1, 1, A_KV, 1)).reshape(2 * A_KV, 2, wide)
        w1 = _mx(a_cmp_w1[layer].reshape(2, 1, 2, CMP_STRIDE, 1, HD, CMP_HIDDEN)
                 * head_slot.reshape(1, A_KV, 1, 1, A_KV, 1, 1)).reshape(2 * A_KV, 2, wide, CMP_HIDDEN)
        cmp_out = _compress(u, pe, w1, _mx(a_cmp_w2[layer]), ncmp)
        kc_own = (cmp_out[:A_KV, :, None, :] * head_slot[:, None, :, None]).reshape(A_KV, ncmp, AUG_BASE)
        kc = _mx(jnp.concatenate([kc_own, jnp.ones((A_KV, ncmp, 2), F32),
                                  jnp.zeros((A_KV, ncmp, AUG_K - AUG_BASE - 2), F32)], axis=2))
        vct = _mx(jnp.transpose(cmp_out[A_KV:], (0, 2, 1)))
        gates = jnp.transpose(fcol("a_gate")[:, :3 * A_HEADS].reshape(nb, BLK, A_KV, A_G, 3),
                              (2, 0, 4, 3, 1)).reshape(A_KV, nb, 3, A_G * BLK)
        vst_aug = jnp.concatenate([_heads_first_t(bcol("a_vs"), A_KV),
                                   jnp.broadcast_to(val_aug, (A_KV, V_ROWS - HD, s))], axis=1)
        a_out = _nsa(proj_b, kc, vct, key_aug, vst_aug, _heads_first_t(bcol("a_vw"), A_KV), gates, tb_a, bc_a, s)

        pad_row = lambda v: jnp.pad(v, (0, LANES - v.shape[0])).reshape(1, LANES)
        b_out = _ssd(proj, b_conv_w[layer], b_conv_b[layer].reshape(1, B_CONV_DIM), pad_row(b_dt_bias[layer]),
                     pad_row(b_a_log[layer]), jnp.repeat(b_d[layer], B_HEADDIM).reshape(1, B_INNER),
                     b_norm[layer].reshape(1, B_INNER))

        cq_rot, ck_rot = _rope(proj, pos_col, inv_row)
        c_out = _band(cq_rot, ck_rot, proj_b, q_col=0, k_col=0, v_col=_BCOLS["c_v"][0] // LANES, shared_kv=True,
                      dil=1, max_dist=C_WINDOW - 1, sinks=c_sinks[layer][jnp.array(_C_ORDER)])

        d_outs, d_lses = [], []
        for p, (window, dil) in enumerate(D_PATTERNS):
            o_p, lse_p = _band(proj, proj, proj, q_col=_FCOLS["d_q"][0] // LANES + p * D_SLOTS * HD // LANES,
                               k_col=_FCOLS["d_k"][0] // LANES, v_col=_FCOLS["d_v"][0] // LANES, shared_kv=False,
                               dil=dil, max_dist=window // dil, tb=tb_d[p], want_lse=True)
            d_outs.append(o_p)
            d_lses.append(lse_p)

        kv = _memkv(mem2, m_norm[layer].reshape(1, D_MODEL), _mx(m_w_kv[layer]))
        m_out = _memattn(proj_b, kv)

        c_rows = slice(A_HEADS * HD + B_INNER, A_HEADS * HD + B_INNER + C_HEADS * HD)
        w_o = w_out[layer].at[c_rows].set(_c_heads(w_out[layer][c_rows], 0))
        x2 = _outproj(x2, proj, a_out, b_out, c_out, d_outs, d_lses, m_out, _mx(w_o),
                      post_norm[layer].reshape(1, D_MODEL))
    return x2.reshape(b, s, D_MODEL)
```
